```python
import math
import jax
import jax.numpy as jnp
from jax import lax
import numpy as np

D_MODEL = 1024
BATCH = 32
SEQ = 256
DEPTH = 4
DEC_BATCH = 8
DEC_SEQ = 4096
PAST_LEN = 256

GRID_W = 64
CHUNK = 128
Q_BLOCK = 128
EPS = 1e-6
N_EVEN = (DEPTH + 1) // 2
N_ODD = DEPTH // 2
N_DIR = 2
H_A = 4
DK_A = D_MODEL // (2 * H_A)
DV_A = DK_A
W_A = H_A * DV_A
CONV_W = 3
H_B = 4
DK_B = D_MODEL // (2 * H_B)
DV_B = DK_B
W_B = H_B * DV_B
EVEN_SPLITS = [2 * W_A, 3 * W_A, 4 * W_A, 4 * W_A + 4 * H_A, 4 * W_A + 4 * H_A + W_B,
               4 * W_A + 4 * H_A + 2 * W_B, 4 * W_A + 4 * H_A + 3 * W_B]
EVEN_IN = 4 * W_A + 4 * H_A + 4 * W_B
H_C = 8
DH_C = D_MODEL // (2 * H_C)
DV_C = 2 * DH_C
ROPE_AXIS_DIM = DH_C // 2
ROPE_BASE = 10000.0
N_EXPERTS = 16
N_GROUPS = 4
EXPERTS_PER_GROUP = N_EXPERTS // N_GROUPS
TOP_K = 2
D_EXPERT = D_MODEL
MOE_BLOCK = 512

kernel_name = 'hybrid_diffusion_mlstm_retention_diffattn_moe'

F32 = jnp.float32


def rms_norm(x, gain):
    xf = x.astype(F32)
    y = xf * lax.rsqrt(jnp.mean(xf * xf, axis=-1, keepdims=True) + EPS)
    return (y * gain.astype(F32)).astype(x.dtype)


def modulation(cond, w, b):
    m = (jax.nn.silu(cond) @ w + b)[:, None, :]
    return jnp.split(m, 6, axis=-1)


def modulate(x, gain, shift, scale):
    return rms_norm(x, gain) * (1.0 + scale) + shift


def rev(a):
    return jnp.flip(a, axis=1)


def to_chunks(x):
    b, t = x.shape[0], x.shape[1]
    return jnp.moveaxis(x.reshape((b, t // CHUNK, CHUNK) + x.shape[2:]), 1, 0)


def from_chunks(x):
    x = jnp.moveaxis(x, 0, 1)
    return x.reshape((x.shape[0], x.shape[1] * x.shape[2]) + x.shape[3:])


def centred_conv(x, w):
    pad = CONV_W // 2
    return lax.conv_general_dilated(x, w[:, None, :].astype(x.dtype), (1,), [(pad, pad)],
                                    dimension_numbers=('NWC', 'WIO', 'NWC'),
                                    feature_group_count=x.shape[-1])


def mlstm_scan(q, k, v, ig, lf, C0, n0, m0):
    causal = jnp.tril(jnp.ones((CHUNK, CHUNK), dtype=bool))

    def step(carry, inp):
        C, n, m = carry
        qc, kc, vc, ic, fc = inp
        bt = jnp.moveaxis(jnp.cumsum(fc, axis=1), 1, 2)
        it = jnp.moveaxis(ic, 1, 2)
        log_d = bt[..., :, None] - bt[..., None, :] + it[..., None, :]
        log_d = jnp.where(causal, log_d, -jnp.inf)
        inter = bt + m[..., None]
        m_t = jnp.maximum(inter, jnp.max(log_d, axis=-1))
        inter_w = jnp.exp(inter - m_t)
        s = jnp.einsum('blhd,bshd->bhls', qc, kc) * jnp.exp(log_d - m_t[..., None])
        num = jnp.einsum('bhls,bshe->bhle', s, vc) + inter_w[..., None] * jnp.einsum('blhd,bhde->bhle', qc, C)
        den = jnp.sum(s, axis=-1) + inter_w * jnp.einsum('blhd,bhd->bhl', qc, n)
        h = num / jnp.maximum(jnp.abs(den), jnp.exp(-m_t))[..., None]
        b_end = bt[..., -1]
        log_w = b_end[..., None] - bt + it
        m_new = jnp.maximum(b_end + m, jnp.max(log_w, axis=-1))
        wk = jnp.exp(log_w - m_new[..., None])
        decay = jnp.exp(b_end + m - m_new)
        C_new = decay[..., None, None] * C + jnp.einsum('bhs,bshd,bshe->bhde', wk, kc, vc)
        n_new = decay[..., None] * n + jnp.einsum('bhs,bshd->bhd', wk, kc)
        return (C_new, n_new, m_new), jnp.moveaxis(h, 1, 2)

    xs = (to_chunks(q), to_chunks(k), to_chunks(v), to_chunks(ig), to_chunks(lf))
    (C, n, m), h = lax.scan(step, (C0.astype(F32), n0.astype(F32), m0.astype(F32)), xs)
    return from_chunks(h), C, n, m


def retention_scan(q, k, v, log_gamma, S0):
    pos = jnp.arange(CHUNK, dtype=F32)
    rel = pos[:, None] - pos[None, :]
    intra = jnp.where(rel >= 0, jnp.exp(log_gamma[:, None, None] * jnp.maximum(rel, 0.0)), 0.0)
    inter = jnp.exp(log_gamma[:, None] * (pos + 1.0)).T
    to_end = jnp.exp(log_gamma[:, None] * (CHUNK - 1.0 - pos))
    chunk_decay = jnp.exp(log_gamma * CHUNK)

    def step(S, inp):
        qc, kc, vc = inp
        s = jnp.einsum('blhd,bshd->bhls', qc, kc) * intra
        o = jnp.einsum('bhls,bshe->blhe', s, vc) + jnp.einsum('blhd,bhde->blhe', qc, S) * inter[None, :, :, None]
        S_new = chunk_decay[None, :, None, None] * S + jnp.einsum('bshd,hs,bshe->bhde', kc, to_end, vc)
        return S_new, o

    S, o = lax.scan(step, S0.astype(F32), (to_chunks(q), to_chunks(k), to_chunks(v)))
    return from_chunks(o), S


def even_mixer(h, C0, n0, m0, S0, w_in, conv_w, gate_bias, mlstm_gain, ret_logit, ret_gain, w_out):
    b, t, _ = h.shape
    qk_a, v_a, o_a, g_a, q_b, k_b, v_b, z_b = jnp.split(h @ w_in, EVEN_SPLITS, axis=-1)
    qk = jax.nn.silu(centred_conv(qk_a, conv_w)).astype(F32).reshape(b, t, 2, H_A, DK_A)
    q_a = qk[:, :, 0]
    k_a = qk[:, :, 1] * (DK_A ** -0.5)
    v_a = v_a.astype(F32).reshape(b, t, H_A, DV_A)
    gates = g_a.astype(F32).reshape(b, t, 2, N_DIR, H_A) + gate_bias.astype(F32)
    ig = gates[:, :, 0]
    lf = jax.nn.log_sigmoid(gates[:, :, 1])
    h_f, C_f, n_f, m_f = mlstm_scan(q_a, k_a, v_a, ig[:, :, 0], lf[:, :, 0], C0[:, 0], n0[:, 0], m0[:, 0])
    h_b, C_b, n_b, m_b = mlstm_scan(rev(q_a), rev(k_a), rev(v_a), rev(ig[:, :, 1]), rev(lf[:, :, 1]),
                                    C0[:, 1], n0[:, 1], m0[:, 1])
    y_a = rms_norm(h_f + rev(h_b), mlstm_gain).reshape(b, t, W_A) * jax.nn.sigmoid(o_a.astype(F32))
    q_b = q_b.astype(F32).reshape(b, t, H_B, DK_B)
    k_b = k_b.astype(F32).reshape(b, t, H_B, DK_B) * (DK_B ** -0.5)
    v_b = v_b.astype(F32).reshape(b, t, H_B, DV_B)
    log_gamma = jax.nn.log_sigmoid(ret_logit.astype(F32))
    o_f, S_f = retention_scan(q_b, k_b, v_b, log_gamma[0], S0[:, 0])
    o_b, S_b = retention_scan(rev(q_b), rev(k_b), rev(v_b), log_gamma[1], S0[:, 1])
    y_b = rms_norm(o_f + rev(o_b), ret_gain).reshape(b, t, W_B) * jax.nn.silu(z_b.astype(F32))
    y = jnp.concatenate([y_a, y_b], axis=-1).astype(w_out.dtype) @ w_out
    states = (jnp.stack([C_f, C_b], axis=1), jnp.stack([n_f, n_b], axis=1),
              jnp.stack([m_f, m_b], axis=1), jnp.stack([S_f, S_b], axis=1))
    return y, states


def axial_rope(x):
    t = x.shape[1]
    rows = t // GRID_W
    row = jnp.repeat(jnp.arange(rows, dtype=F32), GRID_W)
    col = jnp.tile(jnp.arange(GRID_W, dtype=F32), rows)
    nf = ROPE_AXIS_DIM // 2
    inv = ROPE_BASE ** (-jnp.arange(nf, dtype=F32) / nf)

    def rotate(xa, p):
        ang = p[:, None] * inv
        cos = jnp.cos(ang)[None, :, None, None, :]
        sin = jnp.sin(ang)[None, :, None, None, :]
        x1, x2 = xa[..., :nf], xa[..., nf:]
        return jnp.concatenate([x1 * cos - x2 * sin, x1 * sin + x2 * cos], axis=-1)

    xf = x.astype(F32)
    return jnp.concatenate([rotate(xf[..., :ROPE_AXIS_DIM], row), rotate(xf[..., ROPE_AXIS_DIM:], col)],
                           axis=-1).astype(x.dtype)


def diff_qkv(h, w_qkv, q_gain, k_gain):
    b, t, _ = h.shape
    q, k, v = jnp.split(h @ w_qkv, 3, axis=-1)
    q = rms_norm(q.reshape(b, t, H_C, 2, DH_C), q_gain)
    k = rms_norm(k.reshape(b, t, H_C, 2, DH_C), k_gain)
    return q, k, v.reshape(b, t, H_C, DV_C)


def diff_lambda(lam_qk, lam_init):
    l = lam_qk.astype(F32)
    return jnp.exp(jnp.sum(l[0] * l[1])) - jnp.exp(jnp.sum(l[2] * l[3])) + lam_init


def diff_scores(q, k, v, lam):
    s = jnp.einsum('bqhmd,bkhmd->bhmqk', q.astype(F32), k.astype(F32)) * (DH_C ** -0.5)
    p = jax.nn.softmax(s, axis=-1)
    a = p[:, :, 0] - lam * p[:, :, 1]
    return jnp.einsum('bhqk,bkhe->bqhe', a, v.astype(F32))


def diff_attention(q, k, v, lam, lam_init, out_gain, w_out):
    b, t = q.shape[0], q.shape[1]
    qb = jnp.moveaxis(q.reshape((b, t // Q_BLOCK, Q_BLOCK) + q.shape[2:]), 1, 0)
    o = lax.map(lambda qq: diff_scores(qq, k, v, lam), qb)
    o = jnp.moveaxis(o, 0, 1).reshape(b, t, H_C, DV_C)
    o = rms_norm(o, out_gain) * (1.0 - lam_init)
    return o.reshape(b, t, H_C * DV_C).astype(w_out.dtype) @ w_out


def route(h, w_router, router_bias):
    n = h.shape[0]
    s = jax.nn.sigmoid((h @ w_router).astype(F32))
    sel = (s + router_bias.astype(F32)).reshape(n, N_GROUPS, EXPERTS_PER_GROUP)
    group_score = jnp.sum(lax.top_k(sel, TOP_K)[0], axis=-1)
    gidx = jnp.argmax(group_score, axis=-1)
    in_group = sel[jnp.arange(n), gidx]
    _, local = lax.top_k(in_group, TOP_K)
    eidx = gidx[:, None] * EXPERTS_PER_GROUP + local
    w = jnp.take_along_axis(s, eidx, axis=1)
    return eidx, w / jnp.sum(w, axis=-1, keepdims=True)


def moe_experts(h, eidx, gate_w, w_gate, w_up, w_down):
    n, d = h.shape
    a = n * TOP_K
    flat_e = eidx.reshape(a)
    order = jnp.argsort(flat_e)
    sorted_e = flat_e[order]
    counts = jnp.bincount(flat_e, length=N_EXPERTS)
    starts = jnp.cumsum(counts) - counts
    padded = (counts + MOE_BLOCK - 1) // MOE_BLOCK * MOE_BLOCK
    pad_ends = jnp.cumsum(padded)
    pad_starts = pad_ends - padded
    dest = (pad_starts[sorted_e] + jnp.arange(a) - starts[sorted_e]).astype(jnp.int32)
    slot_of_pair = jnp.zeros((a,), jnp.int32).at[order].set(dest)
    n_blocks = (a + N_EXPERTS * (MOE_BLOCK - 1) + MOE_BLOCK - 1) // MOE_BLOCK
    p = n_blocks * MOE_BLOCK
    slot_token = jnp.full((p,), n, jnp.int32).at[slot_of_pair].set(jnp.arange(a, dtype=jnp.int32) // TOP_K)
    xs = jnp.concatenate([h, jnp.zeros((1, d), h.dtype)], axis=0)[slot_token]
    block_e = jnp.minimum(jnp.searchsorted(pad_ends, jnp.arange(n_blocks) * MOE_BLOCK, side='right'),
                          N_EXPERTS - 1)

    def expert_block(args):
        xb, e = args
        return (jax.nn.silu(xb @ w_gate[e]) * (xb @ w_up[e])) @ w_down[e]

    ys = lax.map(expert_block, (xs.reshape(n_blocks, MOE_BLOCK, d), block_e)).reshape(p, d)
    y_pairs = ys[slot_of_pair].reshape(n, TOP_K, d)
    return jnp.einsum('nkd,nk->nd', y_pairs, gate_w.astype(ys.dtype))


def channel_mixer(h, w_router, router_bias, w_gate, w_up, w_down):
    b, t, d = h.shape
    hf = h.reshape(b * t, d)
    eidx, gw = route(hf, w_router, router_bias)
    return moe_experts(hf, eidx, gw, w_gate, w_up, w_down).reshape(b, t, d)


def setup_inputs(seed: int = 0) -> dict:
    key = jax.random.key(seed)
    keys = jax.random.split(key, 40)

    def nrm(i, shape, scale):
        return scale * jax.random.normal(keys[i], shape, F32)

    def gain(i, shape):
        return 1.0 + nrm(i, shape, 0.02)

    f_bias = 3.0 + 3.0 * jnp.arange(H_A, dtype=F32) / (H_A - 1)
    gate_bias = jnp.concatenate([nrm(16, (N_EVEN, 1, N_DIR, H_A), 0.1),
                                 f_bias + nrm(32, (N_EVEN, 1, N_DIR, H_A), 0.1)], axis=1)
    decay_logit = jnp.asarray(np.log(2.0 ** (5 + np.arange(H_B)) - 1.0).astype(np.float32))
    return {
        'x_prompt': nrm(0, (BATCH, SEQ, D_MODEL), 1.0),
        'x_sample': nrm(1, (DEC_BATCH, DEC_SEQ, D_MODEL), 1.0),
        'c': nrm(2, (DEC_BATCH, D_MODEL), 1.0),
        'state_mlstm_C': nrm(3, (DEC_BATCH, N_EVEN, N_DIR, H_A, DK_A, DV_A), 0.1),
        'state_mlstm_n': nrm(4, (DEC_BATCH, N_EVEN, N_DIR, H_A, DK_A), 0.1),
        'state_mlstm_m': nrm(5, (DEC_BATCH, N_EVEN, N_DIR, H_A), 1.0),
        'state_ret_S': nrm(6, (DEC_BATCH, N_EVEN, N_DIR, H_B, DK_B, DV_B), 0.1),
        'cache_k': nrm(7, (DEC_BATCH, N_ODD, PAST_LEN, H_C, 2, DH_C), 1.0),
        'cache_v': nrm(8, (DEC_BATCH, N_ODD, PAST_LEN, H_C, DV_C), 1.0),
        'c_ctx': nrm(9, (D_MODEL,), 1.0),
        'w_ada': nrm(10, (DEPTH, D_MODEL, 6 * D_MODEL), 0.5 * D_MODEL ** -0.5),
        'b_ada': nrm(11, (DEPTH, 6 * D_MODEL), 0.02),
        'norm_mix_gain': gain(12, (DEPTH, D_MODEL)),
        'norm_ffn_gain': gain(13, (DEPTH, D_MODEL)),
        'w_in_even': nrm(14, (N_EVEN, D_MODEL, EVEN_IN), D_MODEL ** -0.5),
        'mlstm_conv': nrm(15, (N_EVEN, CONV_W, 2 * W_A), 0.5),
        'mlstm_gate_bias': gate_bias,
        'mlstm_out_gain': gain(17, (N_EVEN, H_A, DV_A)),
        'ret_decay_logit': decay_logit + nrm(18, (N_EVEN, N_DIR, H_B), 0.05),
        'ret_out_gain': gain(19, (N_EVEN, H_B, DV_B)),
        'w_out_even': nrm(20, (N_EVEN, W_A + W_B, D_MODEL), (W_A + W_B) ** -0.5),
        'w_qkv_odd': nrm(21, (N_ODD, D_MODEL, 3 * H_C * DV_C), D_MODEL ** -0.5),
        'q_norm_gain': gain(22, (N_ODD, DH_C)),
        'k_norm_gain': gain(23, (N_ODD, DH_C)),
        'lambda_qk': nrm(24, (N_ODD, 4, DH_C), 0.1),
        'attn_out_gain': gain(25, (N_ODD, DV_C)),
        'w_out_odd': nrm(26, (N_ODD, H_C * DV_C, D_MODEL), (H_C * DV_C) ** -0.5),
        'w_router': nrm(27, (D_MODEL, N_EXPERTS), D_MODEL ** -0.5),
        'router_bias': nrm(28, (N_EXPERTS,), 0.01),
        'moe_w_gate': nrm(29, (DEPTH, N_EXPERTS, D_MODEL, D_EXPERT), D_MODEL ** -0.5),
        'moe_w_up': nrm(30, (DEPTH, N_EXPERTS, D_MODEL, D_EXPERT), D_MODEL ** -0.5),
        'moe_w_down': nrm(31, (DEPTH, N_EXPERTS, D_EXPERT, D_MODEL), D_EXPERT ** -0.5),
    }


def reference(x_prompt, x_sample, c, state_mlstm_C, state_mlstm_n, state_mlstm_m, state_ret_S,
              cache_k, cache_v, c_ctx, w_ada, b_ada, norm_mix_gain, norm_ffn_gain, w_in_even,
              mlstm_conv, mlstm_gate_bias, mlstm_out_gain, ret_decay_logit, ret_out_gain, w_out_even,
              w_qkv_odd, q_norm_gain, k_norm_gain, lambda_qk, attn_out_gain, w_out_odd,
              w_router, router_bias, moe_w_gate, moe_w_up, moe_w_down):
    bp = x_prompt.shape[0]
    xp, xs = x_prompt, x_sample
    ctx_cond = c_ctx[None, :]
    zero_C = jnp.zeros((bp, N_DIR, H_A, DK_A, DV_A), F32)
    zero_n = jnp.zeros((bp, N_DIR, H_A, DK_A), F32)
    zero_m = jnp.zeros((bp, N_DIR, H_A), F32)
    zero_S = jnp.zeros((bp, N_DIR, H_B, DK_B, DV_B), F32)
    st_C, st_n, st_m, st_S, st_k, st_v = [], [], [], [], [], []
    for l in range(DEPTH):
        j = l // 2
        sh1p, sc1p, g1p, sh2p, sc2p, g2p = modulation(ctx_cond, w_ada[l], b_ada[l])
        sh1s, sc1s, g1s, sh2s, sc2s, g2s = modulation(c, w_ada[l], b_ada[l])
        hp = modulate(xp, norm_mix_gain[l], sh1p, sc1p)
        hs = modulate(xs, norm_mix_gain[l], sh1s, sc1s)
        if l % 2 == 0:
            ew = (w_in_even[j], mlstm_conv[j], mlstm_gate_bias[j], mlstm_out_gain[j],
                  ret_decay_logit[j], ret_out_gain[j], w_out_even[j])
            y_p, (C_p, n_p, m_p, S_p) = even_mixer(hp, zero_C, zero_n, zero_m, zero_S, *ew)
            y_s, _ = even_mixer(hs, state_mlstm_C[:, j], state_mlstm_n[:, j], state_mlstm_m[:, j],
                                state_ret_S[:, j], *ew)
            st_C.append(C_p)
            st_n.append(n_p)
            st_m.append(m_p)
            st_S.append(S_p)
        else:
            lam_init = 0.8 - 0.6 * math.exp(-0.3 * l)
            lam = diff_lambda(lambda_qk[j], lam_init)
            q_p, k_p, v_p = diff_qkv(hp, w_qkv_odd[j], q_norm_gain[j], k_norm_gain[j])
            y_p = diff_attention(q_p, k_p, v_p, lam, lam_init, attn_out_gain[j], w_out_odd[j])
            q_s, k_s, v_s = diff_qkv(hs, w_qkv_odd[j], q_norm_gain[j], k_norm_gain[j])
            k_all = jnp.concatenate([axial_rope(k_s), cache_k[:, j].astype(k_s.dtype)], axis=1)
            v_all = jnp.concatenate([v_s, cache_v[:, j].astype(v_s.dtype)], axis=1)
            y_s = diff_attention(axial_rope(q_s), k_all, v_all, lam, lam_init, attn_out_gain[j], w_out_odd[j])
            st_k.append(k_p)
            st_v.append(v_p)
        xp = xp + g1p * y_p
        xs = xs + g1s * y_s
        xp = xp + g2p * channel_mixer(modulate(xp, norm_ffn_gain[l], sh2p, sc2p), w_router, router_bias,
                                      moe_w_gate[l], moe_w_up[l], moe_w_down[l])
        xs = xs + g2s * channel_mixer(modulate(xs, norm_ffn_gain[l], sh2s, sc2s), w_router, router_bias,
                                      moe_w_gate[l], moe_w_up[l], moe_w_down[l])
    dt = x_prompt.dtype
    new_mlstm_C = jnp.stack(st_C, axis=1).astype(dt)
    new_mlstm_n = jnp.stack(st_n, axis=1).astype(dt)
    new_mlstm_m = jnp.stack(st_m, axis=1).astype(dt)
    new_ret_S = jnp.stack(st_S, axis=1).astype(dt)
    new_k = jnp.stack(st_k, axis=1).astype(dt)
    new_v = jnp.stack(st_v, axis=1).astype(dt)
    return (xp, xs, new_mlstm_C, new_mlstm_n, new_mlstm_m, new_ret_S, new_k, new_v)
```

```python
import functools
import math

import numpy as np
import jax
import jax.numpy as jnp
from jax import lax
from jax.experimental import pallas as pl
from jax.experimental.pallas import tpu as pltpu

F32 = jnp.float32
BF16 = jnp.bfloat16

EPS = 1e-6
GRID_W = 64
ROPE_BASE = 10000.0
H_A = 4
H_B = 4
H_C = 8
N_DIR = 2
N_EXPERTS = 16
N_GROUPS = 4
EXPERTS_PER_GROUP = N_EXPERTS // N_GROUPS
HEAD_W = 128
DH_C = 64
N_COND_PAD = 16

ROW_TILE = 256
SCAN_CHUNK = 128
ATTN_TQ = 256
ATTN_TK = 512
MOE_BLOCK = 512
VMEM_LIMIT = 56 * 1024 * 1024

_HI = lax.Precision.HIGHEST


def _dot(a, b, precision=None):
    return jnp.dot(a, b, preferred_element_type=F32, precision=precision)


def _dot_nt(a, b):
    return lax.dot_general(a, b, (((1,), (1,)), ((), ())), preferred_element_type=F32)


def _dot_tn(a, b):
    return lax.dot_general(a, b, (((0,), (0,)), ((), ())), preferred_element_type=F32)


def _rms(x):
    return x * lax.rsqrt(jnp.mean(x * x, axis=-1, keepdims=True) + EPS)


def _sigmoid(x):
    return 1.0 / (1.0 + jnp.exp(-x))


def _silu(x):
    return x * _sigmoid(x)


def _log_sigmoid(x):
    return jnp.minimum(x, 0.0) - jnp.log1p(jnp.exp(-jnp.abs(x)))


def _params(n_axes):
    return pltpu.CompilerParams(dimension_semantics=("arbitrary",) * n_axes,
                                vmem_limit_bytes=VMEM_LIMIT)


def _mod_kernel(cond_ref, w_ref, b_ref, o_ref):
    s = _silu(cond_ref[...]).astype(BF16)
    o_ref[0] = _dot(s, w_ref[0].astype(BF16)) + b_ref[0]


def _modulation_all(cond, w_ada, b_ada):
    depth, d, n = w_ada.shape
    tn = n // 4
    return pl.pallas_call(
        _mod_kernel,
        grid=(depth, n // tn),
        in_specs=[pl.BlockSpec((N_COND_PAD, d), lambda l, j: (0, 0)),
                  pl.BlockSpec((1, d, tn), lambda l, j: (l, 0, j)),
                  pl.BlockSpec((1, 1, tn), lambda l, j: (l, 0, j))],
        out_specs=pl.BlockSpec((1, N_COND_PAD, tn), lambda l, j: (l, 0, j)),
        out_shape=jax.ShapeDtypeStruct((depth, N_COND_PAD, n), F32),
        compiler_params=_params(2),
        name="adaln_modulation",
    )(cond, w_ada, b_ada.reshape(depth, 1, n))


class _Rows:
    def __init__(self, n_prompt_seq, prompt_len, n_sample_seq, sample_len):
        self.prompt_len = prompt_len
        self.sample_len = sample_len
        self.n_prompt_seq = n_prompt_seq
        self.n_sample_seq = n_sample_seq
        self.n_prompt = n_prompt_seq * prompt_len
        self.n_sample = n_sample_seq * sample_len
        self.total = self.n_prompt + self.n_sample
        assert prompt_len % ROW_TILE == 0 and sample_len % ROW_TILE == 0
        assert self.n_prompt % sample_len == 0 or self.n_sample == 0

    def cond_of_tile(self, i, tile):
        n_p = self.n_prompt // tile
        per_seq = self.sample_len // tile
        return jnp.where(i < n_p, 0, 1 + (i - n_p) // per_seq)


def _proj_even_kernel(x_ref, xp_ref, xn_ref, mod_ref, gain_ref, w_ref, wg_ref, cw_ref, gb_ref,
                      qkva_ref, qkvb_ref, oz_ref, g_ref, *, tm, n_prompt, prompt_len, sample_len):
    i = pl.program_id(0)
    shift = mod_ref[0, 0:1, :]
    scale = mod_ref[0, 1:2, :]
    gain = gain_ref[...]

    def modulated(x):
        return _rms(x) * gain * (1.0 + scale) + shift

    hb = modulated(x_ref[...]).astype(BF16)
    halo = jnp.concatenate([xp_ref[0], xn_ref[0]], axis=0)
    hh = modulated(halo).astype(BF16)

    w_qk = w_ref[:, 0:1024]
    qk = _dot(hb, w_qk)
    qk_halo = _dot(hh, w_qk)
    prev_row = qk_halo[7:8, :]
    next_row = qk_halo[8:9, :]
    local = lax.broadcasted_iota(jnp.int32, (tm, 1), 0)
    seq_len = jnp.where(i * tm < n_prompt, prompt_len, sample_len)
    pos = (i * tm + local) & (seq_len - 1)
    prev = pltpu.roll(qk, 1, axis=0)
    prev = jnp.where(local == 0, prev_row, prev)
    prev = jnp.where(pos == 0, 0.0, prev)
    nxt = pltpu.roll(qk, tm - 1, axis=0)
    nxt = jnp.where(local == tm - 1, next_row, nxt)
    nxt = jnp.where(pos == seq_len - 1, 0.0, nxt)
    cw = cw_ref[...]
    act = _silu(cw[0:1, :] * prev + cw[1:2, :] * qk + cw[2:3, :] * nxt)
    k_scale = HEAD_W ** -0.5
    qkva_ref[:, 0:512] = act[:, 0:512].astype(BF16)
    qkva_ref[:, 512:1024] = (act[:, 512:1024] * k_scale).astype(BF16)
    qkva_ref[:, 1024:1536] = _dot(hb, w_ref[:, 1024:1536]).astype(BF16)

    qkvb_ref[:, 0:512] = _dot(hb, w_ref[:, 1536:2048]).astype(BF16)
    qkvb_ref[:, 512:1024] = (_dot(hb, w_ref[:, 2048:2560]) * k_scale).astype(BF16)
    qkvb_ref[:, 1024:1536] = _dot(hb, w_ref[:, 2560:3072]).astype(BF16)
    oz_ref[:, 0:512] = _dot(hb, w_ref[:, 3072:3584]).astype(BF16)
    oz_ref[:, 512:1024] = _dot(hb, w_ref[:, 3584:4096]).astype(BF16)

    gates = _dot(hb, wg_ref[...]) + gb_ref[...]
    g_ref[:, 0:128] = gates[:, 0:128]
    g_ref[:, 128:256] = _log_sigmoid(gates[:, 128:256])


def _proj_even(x, mods, gain, w_main, w_gates, conv_w, gate_bias, rows):
    n, d = x.shape
    tm = ROW_TILE
    x8 = x.reshape(n // 8, 8, d)
    nb8 = n // 8
    kern = functools.partial(_proj_even_kernel, tm=tm, n_prompt=rows.n_prompt,
                             prompt_len=rows.prompt_len, sample_len=rows.sample_len)
    return pl.pallas_call(
        kern,
        grid=(n // tm,),
        in_specs=[pl.BlockSpec((tm, d), lambda i: (i, 0)),
                  pl.BlockSpec((1, 8, d), lambda i: (jnp.maximum(i * (tm // 8) - 1, 0), 0, 0)),
                  pl.BlockSpec((1, 8, d), lambda i: (jnp.minimum((i + 1) * (tm // 8), nb8 - 1), 0, 0)),
                  pl.BlockSpec((1, 6, d), lambda i: (rows.cond_of_tile(i, tm), 0, 0)),
                  pl.BlockSpec((1, d), lambda i: (0, 0)),
                  pl.BlockSpec(w_main.shape, lambda i: (0, 0)),
                  pl.BlockSpec(w_gates.shape, lambda i: (0, 0)),
                  pl.BlockSpec(conv_w.shape, lambda i: (0, 0)),
                  pl.BlockSpec(gate_bias.shape, lambda i: (0, 0))],
        out_specs=[pl.BlockSpec((tm, 1536), lambda i: (i, 0)),
                   pl.BlockSpec((tm, 1536), lambda i: (i, 0)),
                   pl.BlockSpec((tm, 1024), lambda i: (i, 0)),
                   pl.BlockSpec((tm, 256), lambda i: (i, 0))],
        out_shape=[jax.ShapeDtypeStruct((n, 1536), BF16),
                   jax.ShapeDtypeStruct((n, 1536), BF16),
                   jax.ShapeDtypeStruct((n, 1024), BF16),
                   jax.ShapeDtypeStruct((n, 256), F32)],
        compiler_params=_params(1),
        name="proj_even",
    )(x, x8, x8, mods, gain, w_main, w_gates, conv_w, gate_bias)


def _scan_kernel(fb_ref, bb_ref, first_ref, last_ref, sin_ref, sout_ref, isp_ref,
                 qaf_ref, qab_ref, qbf_ref, qbb_ref, gf_ref, gb_ref,
                 c0_ref, n0_ref, m0_ref, s0_ref, rl_ref,
                 hf_ref, hb_ref, cn_ref, nn_ref, mn_ref, sn_ref,
                 cext_sc, s_sc, m_sc, intra_sc, *, L):
    del fb_ref, bb_ref, sin_ref, sout_ref
    step = pl.program_id(0)
    n_units = N_DIR * H_A
    lane = lax.broadcasted_iota(jnp.int32, (HEAD_W, HEAD_W), 1)
    t_idx = lax.broadcasted_iota(jnp.int32, (L, L), 0)
    s_idx = lax.broadcasted_iota(jnp.int32, (L, L), 1)
    masks = (s_idx <= t_idx, s_idx >= t_idx)
    log_gamma = _log_sigmoid(rl_ref[...])

    @pl.when(step == 0)
    def _():
        rel = jnp.abs(t_idx - s_idx).astype(F32)
        for u in range(n_units):
            lg = log_gamma[u:u + 1, 0:1]
            intra_sc[u] = jnp.where(masks[u // H_B], jnp.exp(lg * rel), 0.0)

    @pl.when(jnp.logical_and(first_ref[step] == 1, isp_ref[step] == 1))
    def _():
        cext_sc[...] = jnp.zeros_like(cext_sc)
        s_sc[...] = jnp.zeros_like(s_sc)
        m_sc[...] = jnp.zeros_like(m_sc)

    @pl.when(jnp.logical_and(first_ref[step] == 1, isp_ref[step] == 0))
    def _():
        n0 = n0_ref[0]
        n0_t = jnp.concatenate([n0, jnp.zeros((HEAD_W - n_units, HEAD_W), F32)], axis=0).T
        for u in range(n_units):
            cext_sc[u, :, 0:HEAD_W] = c0_ref[0, u]
            cext_sc[u, :, HEAD_W:2 * HEAD_W] = jnp.where(lane == 0, n0_t[:, u:u + 1], 0.0)
            s_sc[u] = s0_ref[0, u]
        m_sc[...] = m0_ref[0]

    ones_col = jnp.where(lax.broadcasted_iota(jnp.int32, (L, HEAD_W), 1) == 0, 1.0, 0.0).astype(BF16)
    pos_f = lax.broadcasted_iota(jnp.int32, (L, 1), 0).astype(F32)
    tri = tuple(jnp.where(m, 1.0, 0.0) for m in masks)
    qa = (qaf_ref, qab_ref)
    qb = (qbf_ref, qbb_ref)
    g = (gf_ref, gb_ref)
    out = (hf_ref, hb_ref)

    for d in range(N_DIR):
        ig_all = g[d][:, 0:HEAD_W]
        lf_all = g[d][:, HEAD_W:2 * HEAD_W]
        bt_all = _dot(tri[d], lf_all, precision=_HI)
        ig_t = ig_all.T
        bt_t = bt_all.T
        end = L - 1 if d == 0 else 0
        pos = pos_f if d == 0 else (L - 1.0) - pos_f
        for h in range(H_A):
            u = d * H_A + h
            cs = slice(h * HEAD_W, (h + 1) * HEAD_W)
            bt_col = bt_all[:, u:u + 1]
            ig_col = ig_all[:, u:u + 1]
            bt_row = bt_t[u:u + 1, :]
            ig_row = ig_t[u:u + 1, :]
            m_prev = m_sc[u:u + 1, 0:1]
            b_end = bt_col[end:end + 1, :]
            log_d = jnp.where(masks[d], bt_col - bt_row + ig_row, -jnp.inf)
            inter = bt_col + m_prev
            m_t = jnp.maximum(inter, jnp.max(log_d, axis=1, keepdims=True))
            inter_w = jnp.exp(inter - m_t)
            dmat = jnp.exp(log_d - m_t)
            q = qa[d][:, cs]
            k = qa[d][:, 512 + h * HEAD_W:512 + (h + 1) * HEAD_W]
            v = qa[d][:, 1024 + h * HEAD_W:1024 + (h + 1) * HEAD_W]
            v_ext = jnp.concatenate([v, ones_col], axis=1)
            s = (_dot_nt(q, k) * dmat).astype(BF16)
            num = _dot(s, v_ext) + inter_w * _dot(q, cext_sc[u].astype(BF16))
            den = num[:, HEAD_W:HEAD_W + 1]
            inv = 1.0 / jnp.maximum(jnp.abs(den), jnp.exp(-m_t))
            out[d][:, cs] = (num[:, 0:HEAD_W] * inv).astype(out[d].dtype)
            log_w = b_end - bt_col + ig_col
            m_new = jnp.maximum(b_end + m_prev, jnp.max(log_w, axis=0, keepdims=True))
            wk = jnp.exp(log_w - m_new)
            decay = jnp.exp(b_end + m_prev - m_new)
            upd = _dot_tn(k, (wk * v_ext.astype(F32)).astype(BF16))
            cext_sc[u] = decay * cext_sc[u] + upd
            m_sc[u:u + 1, :] = jnp.broadcast_to(m_new, (1, HEAD_W))
            lg = log_gamma[u:u + 1, 0:1]
            qr = qb[d][:, cs]
            kr = qb[d][:, 512 + h * HEAD_W:512 + (h + 1) * HEAD_W]
            vr = qb[d][:, 1024 + h * HEAD_W:1024 + (h + 1) * HEAD_W]
            sr = (_dot_nt(qr, kr) * intra_sc[u]).astype(BF16)
            o = _dot(sr, vr) + _dot(qr, s_sc[u].astype(BF16)) * jnp.exp(lg * (pos + 1.0))
            out[d][:, 512 + h * HEAD_W:512 + (h + 1) * HEAD_W] = o.astype(out[d].dtype)
            to_end = jnp.exp(lg * ((L - 1.0) - pos))
            s_sc[u] = jnp.exp(lg * float(L)) * s_sc[u] + _dot_tn(kr, (to_end * vr.astype(F32)).astype(BF16))

    @pl.when(last_ref[step] == 1)
    def _():
        n_cols = jnp.zeros((HEAD_W, HEAD_W), F32)
        for u in range(n_units):
            cn_ref[0, u] = cext_sc[u, :, 0:HEAD_W]
            sn_ref[0, u] = s_sc[u]
            n_cols = jnp.where(lane == u, cext_sc[u, :, HEAD_W:HEAD_W + 1], n_cols)
        nn_ref[0] = n_cols.T[0:n_units, :]
        mn_ref[0] = m_sc[...]


def _scan_tables(rows, L):
    fb, bb, first, last, sin, sout, isp = [], [], [], [], [], [], []
    base = 0
    for kind, n_seq, seq_len in (("p", rows.n_prompt_seq, rows.prompt_len),
                                 ("s", rows.n_sample_seq, rows.sample_len)):
        nc = seq_len // L
        for b in range(n_seq):
            for c in range(nc):
                fb.append(base + b * nc + c)
                bb.append(base + b * nc + nc - 1 - c)
                first.append(int(c == 0))
                last.append(int(c == nc - 1 and kind == "p"))
                sin.append(b if kind == "s" else 0)
                sout.append(b if kind == "p" else rows.n_prompt_seq - 1)
                isp.append(int(kind == "p"))
        base += n_seq * nc
    return [jnp.asarray(np.asarray(t, np.int32)) for t in (fb, bb, first, last, sin, sout, isp)]


def _scan_even(qkva, qkvb, gates, c0, n0, m0, s0, ret_logit, rows):
    L = SCAN_CHUNK
    n = qkva.shape[0]
    tables = _scan_tables(rows, L)
    n_steps = int(tables[0].shape[0])
    nu = N_DIR * H_A
    fwd = lambda w: pl.BlockSpec((L, w), lambda s, fb, bb, fi, la, si, so, ip: (fb[s], 0))
    bwd = lambda w: pl.BlockSpec((L, w), lambda s, fb, bb, fi, la, si, so, ip: (bb[s], 0))
    st_in4 = pl.BlockSpec((1, nu, HEAD_W, HEAD_W), lambda s, fb, bb, fi, la, si, so, ip: (si[s], 0, 0, 0))
    st_in3 = pl.BlockSpec((1, nu, HEAD_W), lambda s, fb, bb, fi, la, si, so, ip: (si[s], 0, 0))
    st_out4 = pl.BlockSpec((1, nu, HEAD_W, HEAD_W), lambda s, fb, bb, fi, la, si, so, ip: (so[s], 0, 0, 0))
    st_out3 = pl.BlockSpec((1, nu, HEAD_W), lambda s, fb, bb, fi, la, si, so, ip: (so[s], 0, 0))
    nps = rows.n_prompt_seq
    return pl.pallas_call(
        functools.partial(_scan_kernel, L=L),
        grid_spec=pltpu.PrefetchScalarGridSpec(
            num_scalar_prefetch=7,
            grid=(n_steps,),
            in_specs=[fwd(1536), bwd(1536), fwd(1536), bwd(1536), fwd(256), bwd(256),
                      st_in4, st_in3, st_in3, st_in4,
                      pl.BlockSpec((nu, HEAD_W), lambda s, *_: (0, 0))],
            out_specs=[fwd(1024), bwd(1024), st_out4, st_out3, st_out3, st_out4],
            scratch_shapes=[pltpu.VMEM((nu, HEAD_W, 2 * HEAD_W), F32),
                            pltpu.VMEM((nu, HEAD_W, HEAD_W), F32),
                            pltpu.VMEM((nu, HEAD_W), F32),
                            pltpu.VMEM((nu, L, L), F32)]),
        out_shape=[jax.ShapeDtypeStruct((n, 1024), BF16),
                   jax.ShapeDtypeStruct((n, 1024), BF16),
                   jax.ShapeDtypeStruct((nps, nu, HEAD_W, HEAD_W), F32),
                   jax.ShapeDtypeStruct((nps, nu, HEAD_W), F32),
                   jax.ShapeDtypeStruct((nps, nu, HEAD_W), F32),
                   jax.ShapeDtypeStruct((nps, nu, HEAD_W, HEAD_W), F32)],
        compiler_params=_params(1),
        name="scan_even",
    )(*tables, qkva, qkva, qkvb, qkvb, gates, gates, c0, n0, m0, s0, ret_logit)


def _proj_odd_kernel(x_ref, mod_ref, gain_ref, w_ref, e_ref, qg_ref, kg_ref, cos_ref, sin_ref,
                     q_ref, k_ref, v_ref, *, rope):
    shift = mod_ref[0, 0:1, :]
    scale = mod_ref[0, 1:2, :]
    hb = (_rms(x_ref[...]) * gain_ref[...] * (1.0 + scale) + shift).astype(BF16)

    def qk_norm(raw, g):
        ss = _dot((raw * raw).astype(BF16), e_ref[...])
        return raw * lax.rsqrt(ss * (1.0 / DH_C) + EPS) * g

    def rotate(y):
        if not rope:
            return y
        lane = lax.broadcasted_iota(jnp.int32, y.shape, 1)
        first_half = (lane & 31) < 16
        partner = jnp.where(first_half, pltpu.roll(y, y.shape[1] - 16, axis=1), pltpu.roll(y, 16, axis=1))
        cos = jnp.concatenate([cos_ref[...]] * (y.shape[1] // HEAD_W), axis=1)
        sin = jnp.concatenate([sin_ref[...]] * (y.shape[1] // HEAD_W), axis=1)
        return y * cos + partner * sin

    q = rotate(qk_norm(_dot(hb, w_ref[:, 0:1024]), qg_ref[...]))
    q_ref[...] = (q * (DH_C ** -0.5)).astype(q_ref.dtype)
    k = rotate(qk_norm(_dot(hb, w_ref[:, 1024:2048]), kg_ref[...]))
    k_ref[...] = k.astype(k_ref.dtype)
    v_ref[...] = _dot(hb, w_ref[:, 2048:3072]).astype(v_ref.dtype)


def _proj_odd(x, mods, gain, w_qkv, e64, q_gain, k_gain, cos, sin, rows, *, sample):
    d = x.shape[1]
    tm = ROW_TILE
    if sample:
        n, base, kv_dtype = rows.n_sample, rows.n_prompt // tm, BF16
        per_seq = rows.sample_len // tm
        table = lambda i: (i % per_seq, 0)
    else:
        n, base, kv_dtype = rows.n_prompt, 0, F32
        table = lambda i: (0, 0)
    return pl.pallas_call(
        functools.partial(_proj_odd_kernel, rope=sample),
        grid=(n // tm,),
        in_specs=[pl.BlockSpec((tm, d), lambda i: (base + i, 0)),
                  pl.BlockSpec((1, 6, d), lambda i: (rows.cond_of_tile(base + i, tm), 0, 0)),
                  pl.BlockSpec((1, d), lambda i: (0, 0)),
                  pl.BlockSpec(w_qkv.shape, lambda i: (0, 0)),
                  pl.BlockSpec(e64.shape, lambda i: (0, 0)),
                  pl.BlockSpec((1, d), lambda i: (0, 0)),
                  pl.BlockSpec((1, d), lambda i: (0, 0)),
                  pl.BlockSpec((tm, HEAD_W), table),
                  pl.BlockSpec((tm, HEAD_W), table)],
        out_specs=[pl.BlockSpec((tm, d), lambda i: (i, 0))] * 3,
        out_shape=[jax.ShapeDtypeStruct((n, d), BF16),
                   jax.ShapeDtypeStruct((n, d), kv_dtype),
                   jax.ShapeDtypeStruct((n, d), kv_dtype)],
        compiler_params=_params(1),
        name="proj_odd_sample" if sample else "proj_odd_prompt",
    )(x, mods, gain, w_qkv, e64, q_gain, k_gain, cos, sin)


def _rope_tables(sample_len):
    t = np.arange(sample_len)
    nf = DH_C // 4
    inv = ROPE_BASE ** (-np.arange(nf, dtype=np.float32) / nf)
    row = (t // GRID_W).astype(np.float32)[:, None] * inv[None, :]
    col = (t % GRID_W).astype(np.float32)[:, None] * inv[None, :]
    cos64 = np.concatenate([np.cos(row), np.cos(row), np.cos(col), np.cos(col)], axis=1)
    sin64 = np.concatenate([-np.sin(row), np.sin(row), -np.sin(col), np.sin(col)], axis=1)
    cos = np.concatenate([cos64, cos64], axis=1).astype(np.float32)
    sin = np.concatenate([sin64, sin64], axis=1).astype(np.float32)
    return jnp.asarray(cos), jnp.asarray(sin)


def _attn_kernel(*refs, tq, tk, n_main, n_cache, lam_init):
    if n_cache:
        q_ref, k_ref, v_ref, ck_ref, cv_ref, lq_ref, og_ref, o_ref = refs
    else:
        q_ref, k_ref, v_ref, lq_ref, og_ref, o_ref = refs
    q = q_ref[0]
    lane = lax.broadcasted_iota(jnp.int32, q.shape, 1)
    zero = jnp.zeros_like(q)
    qs = jnp.concatenate([jnp.where(lane < DH_C, q, zero), jnp.where(lane >= DH_C, q, zero)], axis=0)

    def tile(carry, kt, vt):
        m, l, acc = carry
        s = _dot_nt(qs, kt)
        m_new = jnp.maximum(m, jnp.max(s, axis=1, keepdims=True))
        alpha = jnp.exp(m - m_new)
        p = jnp.exp(s - m_new)
        l = alpha * l + jnp.sum(p, axis=1, keepdims=True)
        acc = alpha * acc + _dot(p.astype(BF16), vt)
        return m_new, l, acc

    carry = (jnp.full((2 * tq, 1), -jnp.inf, F32), jnp.zeros((2 * tq, 1), F32),
             jnp.zeros((2 * tq, HEAD_W), F32))

    def body(j, c):
        start = pl.multiple_of(j * tk, tk)
        return tile(c, k_ref[0, pl.ds(start, tk), :].astype(BF16), v_ref[0, pl.ds(start, tk), :].astype(BF16))

    carry = lax.fori_loop(0, n_main // tk, body, carry)
    if n_cache:
        carry = tile(carry, ck_ref[0].astype(BF16), cv_ref[0].astype(BF16))
    _, l, acc = carry
    o = acc / l
    lq = lq_ref[...]
    lam = (jnp.exp(jnp.sum(lq[0:1, :] * lq[1:2, :], axis=1, keepdims=True))
           - jnp.exp(jnp.sum(lq[2:3, :] * lq[3:4, :], axis=1, keepdims=True)) + lam_init)
    o = o[0:tq, :] - lam * o[tq:2 * tq, :]
    o_ref[0] = (_rms(o) * og_ref[...] * (1.0 - lam_init)).astype(o_ref.dtype)


def _attention(q, k, v, cache_k, cache_v, lam_qk, out_gain, lam_init):
    b, t, d = q.shape
    tq = min(ATTN_TQ, t)
    tk = min(ATTN_TK, t)
    n_cache = 0 if cache_k is None else cache_k.shape[1]
    seq = lambda n: pl.BlockSpec((1, n, HEAD_W), lambda bi, h, qi: (bi, 0, h))
    in_specs = [pl.BlockSpec((1, tq, HEAD_W), lambda bi, h, qi: (bi, qi, h)), seq(t), seq(t)]
    args = [q, k, v]
    if n_cache:
        in_specs += [seq(n_cache), seq(n_cache)]
        args += [cache_k, cache_v]
    in_specs += [pl.BlockSpec(lam_qk.shape, lambda bi, h, qi: (0, 0)),
                 pl.BlockSpec(out_gain.shape, lambda bi, h, qi: (0, 0))]
    args += [lam_qk, out_gain]
    return pl.pallas_call(
        functools.partial(_attn_kernel, tq=tq, tk=tk, n_main=t, n_cache=n_cache, lam_init=lam_init),
        grid=(b, H_C, t // tq),
        in_specs=in_specs,
        out_specs=pl.BlockSpec((1, tq, HEAD_W), lambda bi, h, qi: (bi, qi, h)),
        out_shape=jax.ShapeDtypeStruct((b, t, d), BF16),
        compiler_params=_params(3),
        name="diff_attention",
    )(*args)


def _out_kernel(*refs, tm, even, n_prompt_tiles):
    if even:
        (hf_ref, hb_ref, oz_ref, ga_ref, gb_ref, x_ref, mod_ref, w_ref, fg_ref, wr_ref, rb_ref,
         x1_ref, h2_ref, e_ref, gw_ref, rk_ref, c1_ref, c2_ref, c1_sc, c2_sc) = refs
    else:
        (op_ref, os_ref, x_ref, mod_ref, w_ref, fg_ref, wr_ref, rb_ref,
         x1_ref, h2_ref, e_ref, gw_ref, rk_ref, c1_ref, c2_ref, c1_sc, c2_sc) = refs
    i = pl.program_id(0)

    @pl.when(i == 0)
    def _():
        c1_sc[...] = jnp.zeros_like(c1_sc)
        c2_sc[...] = jnp.zeros_like(c2_sc)

    if even:
        hs = hf_ref[...].astype(F32) + hb_ref[...].astype(F32)
        oz = oz_ref[...].astype(F32)
        parts = []
        for h in range(H_A + H_B):
            cs = slice(h * HEAD_W, (h + 1) * HEAD_W)
            gain = ga_ref[:, cs] if h < H_A else gb_ref[:, (h - H_A) * HEAD_W:(h - H_A + 1) * HEAD_W]
            act = _sigmoid(oz[:, cs]) if h < H_A else _silu(oz[:, cs])
            parts.append((_rms(hs[:, cs]) * gain * act).astype(BF16))
        y_in = jnp.concatenate(parts, axis=1)
    else:
        y_in = jnp.where(i < n_prompt_tiles, op_ref[...], os_ref[...])

    x1 = x_ref[...] + mod_ref[0, 2:3, :] * _dot(y_in, w_ref[...])
    x1_ref[...] = x1
    h2 = _rms(x1) * fg_ref[...] * (1.0 + mod_ref[0, 4:5, :]) + mod_ref[0, 3:4, :]
    h2_ref[...] = h2.astype(BF16)

    score = _sigmoid(_dot(h2, wr_ref[...], precision=_HI))
    st = score.T[0:N_EXPERTS, :]
    sel = st + rb_ref[:, 0:1]
    srow = [st[e:e + 1, :] for e in range(N_EXPERTS)]
    brow = [sel[e:e + 1, :] for e in range(N_EXPERTS)]
    epg = EXPERTS_PER_GROUP
    gscore = []
    for g in range(N_GROUPS):
        a = brow[g * epg:(g + 1) * epg]
        best = None
        for p in range(epg):
            for r in range(p + 1, epg):
                pair = a[p] + a[r]
                best = pair if best is None else jnp.maximum(best, pair)
        gscore.append(best)
    gbest, gidx = gscore[0], jnp.zeros((1, tm), jnp.int32)
    for g in range(1, N_GROUPS):
        better = gscore[g] > gbest
        gbest = jnp.where(better, gscore[g], gbest)
        gidx = jnp.where(better, g, gidx)
    vals, sig = [], []
    for p in range(epg):
        vp, sp = brow[p], srow[p]
        for g in range(1, N_GROUPS):
            vp = jnp.where(gidx == g, brow[g * epg + p], vp)
            sp = jnp.where(gidx == g, srow[g * epg + p], sp)
        vals.append(vp)
        sig.append(sp)
    v1, i1, w1 = vals[0], jnp.zeros((1, tm), jnp.int32), sig[0]
    for p in range(1, epg):
        better = vals[p] > v1
        v1 = jnp.where(better, vals[p], v1)
        i1 = jnp.where(better, p, i1)
        w1 = jnp.where(better, sig[p], w1)
    v2 = jnp.full((1, tm), -jnp.inf, F32)
    i2 = jnp.zeros((1, tm), jnp.int32)
    w2 = jnp.zeros((1, tm), F32)
    for p in range(epg):
        better = jnp.logical_and(i1 != p, vals[p] > v2)
        v2 = jnp.where(better, vals[p], v2)
        i2 = jnp.where(better, p, i2)
        w2 = jnp.where(better, sig[p], w2)
    e1 = gidx * epg + i1
    e2 = gidx * epg + i2
    wsum = w1 + w2

    erow = lax.broadcasted_iota(jnp.int32, (N_EXPERTS, tm), 0)
    earlier = jnp.where(lax.broadcasted_iota(jnp.int32, (tm, tm), 0)
                        < lax.broadcasted_iota(jnp.int32, (tm, tm), 1), 1.0, 0.0).astype(BF16)
    ranks = []
    for e_k, c_sc in ((e1, c1_sc), (e2, c2_sc)):
        onehot = jnp.where(erow == e_k, 1.0, 0.0)
        before = _dot(onehot.astype(BF16), earlier) + c_sc[:, 0:1]
        ranks.append(jnp.sum(onehot * before, axis=0, keepdims=True))
        c_sc[...] = c_sc[...] + jnp.sum(onehot, axis=1, keepdims=True)
    zi = jnp.zeros((6, tm), jnp.int32)
    e_ref[...] = jnp.concatenate([e1, e2, zi], axis=0)
    gw_ref[...] = jnp.concatenate([w1 / wsum, w2 / wsum, jnp.zeros((6, tm), F32)], axis=0)
    rk_ref[...] = jnp.concatenate([ranks[0].astype(jnp.int32), ranks[1].astype(jnp.int32), zi], axis=0)
    c1_ref[...] = c1_sc[...]
    c2_ref[...] = c2_sc[...]


def _out_and_route(mix_in, x, mods, w_out, ffn_gain, w_router, router_bias, rows, *, even):
    n, d = x.shape
    tm = ROW_TILE
    row = lambda w: pl.BlockSpec((tm, w), lambda i: (i, 0))
    full = lambda a: pl.BlockSpec(a.shape, lambda i: (0,) * a.ndim)
    n_pt = rows.n_prompt // tm
    if even:
        hf, hb, oz, gain_a, gain_b = mix_in
        in_specs = [row(1024), row(1024), row(1024), full(gain_a), full(gain_b)]
        args = [hf, hb, oz, gain_a, gain_b]
    else:
        o_p, o_s = mix_in
        in_specs = [pl.BlockSpec((tm, d), lambda i: (jnp.minimum(i, n_pt - 1), 0)),
                    pl.BlockSpec((tm, d), lambda i: (jnp.maximum(i - n_pt, 0), 0))]
        args = [o_p, o_s]
    in_specs += [row(d), pl.BlockSpec((1, 6, d), lambda i: (rows.cond_of_tile(i, tm), 0, 0)),
                 full(w_out), full(ffn_gain), full(w_router), full(router_bias)]
    args += [x, mods, w_out, ffn_gain, w_router, router_bias]
    col = lambda: pl.BlockSpec((8, tm), lambda i: (0, i))
    cnt = lambda: pl.BlockSpec((N_EXPERTS, HEAD_W), lambda i: (0, 0))
    return pl.pallas_call(
        functools.partial(_out_kernel, tm=tm, even=even, n_prompt_tiles=n_pt),
        grid=(n // tm,),
        in_specs=in_specs,
        out_specs=[row(d), row(d), col(), col(), col(), cnt(), cnt()],
        out_shape=[jax.ShapeDtypeStruct((n, d), F32),
                   jax.ShapeDtypeStruct((n, d), BF16),
                   jax.ShapeDtypeStruct((8, n), jnp.int32),
                   jax.ShapeDtypeStruct((8, n), F32),
                   jax.ShapeDtypeStruct((8, n), jnp.int32),
                   jax.ShapeDtypeStruct((N_EXPERTS, HEAD_W), F32),
                   jax.ShapeDtypeStruct((N_EXPERTS, HEAD_W), F32)],
        scratch_shapes=[pltpu.VMEM((N_EXPERTS, HEAD_W), F32), pltpu.VMEM((N_EXPERTS, HEAD_W), F32)],
        compiler_params=_params(1),
        name="out_even" if even else "out_odd",
    )(*args)


def _moe_kernel(be_ref, nu_ref, x_ref, wg_ref, wu_ref, wd_ref, y_ref, wg_sc, wu_sc, wd_sc):
    i = pl.program_id(0)
    changed = jnp.logical_or(i == 0, be_ref[i] != be_ref[jnp.maximum(i - 1, 0)])

    @pl.when(jnp.logical_and(changed, i < nu_ref[0]))
    def _():
        wg_sc[...] = wg_ref[0].astype(BF16)
        wu_sc[...] = wu_ref[0].astype(BF16)
        wd_sc[...] = wd_ref[0].astype(BF16)

    @pl.when(i < nu_ref[0])
    def _():
        x = x_ref[...]
        a = (_silu(_dot(x, wg_sc[...])) * _dot(x, wu_sc[...])).astype(BF16)
        y_ref[...] = _dot(a, wd_sc[...]).astype(y_ref.dtype)

    @pl.when(i >= nu_ref[0])
    def _():
        y_ref[...] = jnp.zeros_like(y_ref)


def _moe_experts(xs, block_e, n_used, w_gate, w_up, w_down):
    p, d = xs.shape
    bm = MOE_BLOCK
    wspec = pl.BlockSpec((1, d, d), lambda i, be, nu: (be[i], 0, 0))
    return pl.pallas_call(
        _moe_kernel,
        grid_spec=pltpu.PrefetchScalarGridSpec(
            num_scalar_prefetch=2,
            grid=(p // bm,),
            in_specs=[pl.BlockSpec((bm, d), lambda i, be, nu: (i, 0)), wspec, wspec, wspec],
            out_specs=pl.BlockSpec((bm, d), lambda i, be, nu: (i, 0)),
            scratch_shapes=[pltpu.VMEM((d, d), BF16)] * 3),
        out_shape=jax.ShapeDtypeStruct((p, d), BF16),
        compiler_params=_params(1),
        name="moe_experts",
    )(block_e, n_used, xs, w_gate, w_up, w_down)


def _combine_kernel(x_ref, y_ref, gw_ref, mod_ref, o_ref):
    w = gw_ref[...]
    y = w[:, 0:1] * y_ref[0].astype(F32) + w[:, 1:2] * y_ref[1].astype(F32)
    o_ref[...] = x_ref[...] + mod_ref[0, 5:6, :] * y


def _combine(x1, y_pairs, gw_cols, mods, rows):
    n, d = x1.shape
    tm = ROW_TILE
    return pl.pallas_call(
        _combine_kernel,
        grid=(n // tm,),
        in_specs=[pl.BlockSpec((tm, d), lambda i: (i, 0)),
                  pl.BlockSpec((2, tm, d), lambda i: (0, i, 0)),
                  pl.BlockSpec((tm, 2), lambda i: (i, 0)),
                  pl.BlockSpec((1, 6, d), lambda i: (rows.cond_of_tile(i, tm), 0, 0))],
        out_specs=pl.BlockSpec((tm, d), lambda i: (i, 0)),
        out_shape=jax.ShapeDtypeStruct((n, d), F32),
        compiler_params=_params(1),
        name="moe_combine",
    )(x1, y_pairs, gw_cols, mods)


def _moe_layer(x1, h2, eidx, gw, rank, cnt1, cnt2, mods, w_gate, w_up, w_down, rows):
    n = x1.shape[0]
    bm = MOE_BLOCK
    n_blocks = (2 * n + N_EXPERTS * (bm - 1) + bm - 1) // bm
    c1 = cnt1[:, 0].astype(jnp.int32)
    c2 = cnt2[:, 0].astype(jnp.int32)
    padded = (c1 + c2 + bm - 1) // bm * bm
    pad_ends = jnp.cumsum(padded)
    pad_starts = pad_ends - padded
    e1, e2 = eidx[0], eidx[1]
    slot1 = pad_starts[e1] + rank[0]
    slot2 = pad_starts[e2] + c1[e2] + rank[1]
    slots = jnp.stack([slot1, slot2], axis=0)
    tok = jnp.arange(n, dtype=jnp.int32)
    slot_token = jnp.zeros((n_blocks * bm,), jnp.int32).at[slots.reshape(-1)].set(jnp.concatenate([tok, tok]))
    block_e = jnp.minimum(jnp.searchsorted(pad_ends, jnp.arange(n_blocks, dtype=jnp.int32) * bm, side='right'),
                          N_EXPERTS - 1).astype(jnp.int32)
    n_used = (pad_ends[-1] // bm).astype(jnp.int32).reshape(1)
    xs = jnp.take(h2, slot_token, axis=0)
    ys = _moe_experts(xs, block_e, n_used, w_gate, w_up, w_down)
    y_pairs = jnp.take(ys, slots, axis=0)
    return _combine(x1, y_pairs, gw[0:2].T, mods, rows)


def kernel(x_prompt, x_sample, c, state_mlstm_C, state_mlstm_n, state_mlstm_m, state_ret_S, cache_k, cache_v, c_ctx, w_ada, b_ada, norm_mix_gain, norm_ffn_gain, w_in_even, mlstm_conv, mlstm_gate_bias, mlstm_out_gain, ret_decay_logit, ret_out_gain, w_out_even, w_qkv_odd, q_norm_gain, k_norm_gain, lambda_qk, attn_out_gain, w_out_odd, w_router, router_bias, moe_w_gate, moe_w_up, moe_w_down):
    bp, seq, d = x_prompt.shape
    bs, dec_seq, _ = x_sample.shape
    depth = w_ada.shape[0]
    past = cache_k.shape[2]
    rows = _Rows(bp, seq, bs, dec_seq)
    nu = N_DIR * H_A
    assert 1 + bs <= N_COND_PAD

    x = jnp.concatenate([x_prompt.reshape(bp * seq, d), x_sample.reshape(bs * dec_seq, d)], axis=0)
    cond = jnp.concatenate([c_ctx[None, :], c, jnp.zeros((N_COND_PAD - 1 - bs, d), F32)], axis=0)
    mods_all = _modulation_all(cond, w_ada, b_ada).reshape(depth, N_COND_PAD, 6, d)

    w_router_pad = jnp.pad(w_router, ((0, 0), (0, HEAD_W - N_EXPERTS)))
    router_bias_col = jnp.broadcast_to(router_bias[:, None], (N_EXPERTS, HEAD_W))
    e64 = jnp.asarray(np.kron(np.eye(d // DH_C, dtype=np.float32), np.ones((DH_C, DH_C), np.float32)), BF16)
    cos, sin = _rope_tables(dec_seq)

    st_c, st_n, st_m, st_s, st_k, st_v = [], [], [], [], [], []
    for l in range(depth):
        j = l // 2
        mods = mods_all[l]
        gain_mix = norm_mix_gain[l][None, :]
        if l % 2 == 0:
            w = w_in_even[j]
            wa = 4 * HEAD_W * 4
            gcols = 4 * H_A
            ob = wa + gcols
            w_main = jnp.concatenate([w[:, 0:1536], w[:, ob:ob + 1536], w[:, 1536:2048],
                                      w[:, ob + 1536:ob + 2048]], axis=1).astype(BF16)
            w_g = w[:, wa:wa + gcols]
            zpad = jnp.zeros((d, HEAD_W - nu), F32)
            w_gates = jnp.concatenate([w_g[:, 0:nu], zpad, w_g[:, nu:2 * nu], zpad], axis=1).astype(BF16)
            gb = mlstm_gate_bias[j].reshape(2, nu)
            zb = jnp.zeros((HEAD_W - nu,), F32)
            gate_bias = jnp.concatenate([gb[0], zb, gb[1], zb])[None, :]
            qkva, qkvb, oz, gates = _proj_even(x, mods, gain_mix, w_main, w_gates, mlstm_conv[j], gate_bias, rows)
            c0 = state_mlstm_C[:, j].reshape(bs, nu, HEAD_W, HEAD_W)
            n0 = state_mlstm_n[:, j].reshape(bs, nu, HEAD_W)
            m0 = jnp.broadcast_to(state_mlstm_m[:, j].reshape(bs, nu, 1), (bs, nu, HEAD_W))
            s0 = state_ret_S[:, j].reshape(bs, nu, HEAD_W, HEAD_W)
            ret_logit = jnp.broadcast_to(ret_decay_logit[j].reshape(nu, 1), (nu, HEAD_W))
            hf, hb, cn, nn, mn, sn = _scan_even(qkva, qkvb, gates, c0, n0, m0, s0, ret_logit, rows)
            st_c.append(cn.reshape(bp, N_DIR, H_A, HEAD_W, HEAD_W))
            st_n.append(nn.reshape(bp, N_DIR, H_A, HEAD_W))
            st_m.append(mn[:, :, 0].reshape(bp, N_DIR, H_A))
            st_s.append(sn.reshape(bp, N_DIR, H_B, HEAD_W, HEAD_W))
            mix_in = (hf, hb, oz, mlstm_out_gain[j].reshape(1, H_A * HEAD_W),
                      ret_out_gain[j].reshape(1, H_B * HEAD_W))
            w_out = w_out_even[j].astype(BF16)
        else:
            lam_init = 0.8 - 0.6 * math.exp(-0.3 * l)
            w_qkv = w_qkv_odd[j].astype(BF16)
            qg = jnp.tile(q_norm_gain[j], d // DH_C)[None, :]
            kg = jnp.tile(k_norm_gain[j], d // DH_C)[None, :]
            og = attn_out_gain[j][None, :]
            q_p, k_p, v_p = _proj_odd(x, mods, gain_mix, w_qkv, e64, qg, kg, cos, sin, rows, sample=False)
            q_s, k_s, v_s = _proj_odd(x, mods, gain_mix, w_qkv, e64, qg, kg, cos, sin, rows, sample=True)
            o_p = _attention(q_p.reshape(bp, seq, d), k_p.reshape(bp, seq, d), v_p.reshape(bp, seq, d),
                             None, None, lambda_qk[j], og, lam_init)
            o_s = _attention(q_s.reshape(bs, dec_seq, d), k_s.reshape(bs, dec_seq, d), v_s.reshape(bs, dec_seq, d),
                             cache_k[:, j].reshape(bs, past, d), cache_v[:, j].reshape(bs, past, d),
                             lambda_qk[j], og, lam_init)
            st_k.append(k_p.reshape(bp, seq, H_C, 2, DH_C))
            st_v.append(v_p.reshape(bp, seq, H_C, 2 * DH_C))
            mix_in = (o_p.reshape(bp * seq, d), o_s.reshape(bs * dec_seq, d))
            w_out = w_out_odd[j].astype(BF16)
        x1, h2, eidx, gw, rank, cnt1, cnt2 = _out_and_route(
            mix_in, x, mods, w_out, norm_ffn_gain[l][None, :], w_router_pad, router_bias_col, rows,
            even=(l % 2 == 0))
        x = _moe_layer(x1, h2, eidx, gw, rank, cnt1, cnt2, mods, moe_w_gate[l], moe_w_up[l], moe_w_down[l], rows)

    dt = x_prompt.dtype
    y_prompt = x[:rows.n_prompt].reshape(bp, seq, d)
    y_sample = x[rows.n_prompt:].reshape(bs, dec_seq, d)
    return (y_prompt, y_sample,
            jnp.stack(st_c, axis=1).astype(dt), jnp.stack(st_n, axis=1).astype(dt),
            jnp.stack(st_m, axis=1).astype(dt), jnp.stack(st_s, axis=1).astype(dt),
            jnp.stack(st_k, axis=1).astype(dt), jnp.stack(st_v, axis=1).astype(dt))
```

```python
import functools
import math

import numpy as np
import jax
import jax.numpy as jnp
from jax import lax
from jax.experimental import pallas as pl
from jax.experimental.pallas import tpu as pltpu

F32 = jnp.float32
BF16 = jnp.bfloat16

EPS = 1e-6
GRID_W = 64
ROPE_BASE = 10000.0
H_A = 4
H_B = 4
H_C = 8
N_DIR = 2
N_EXPERTS = 16
N_GROUPS = 4
EXPERTS_PER_GROUP = N_EXPERTS // N_GROUPS
HEAD_W = 128
DH_C = 64
N_COND_PAD = 16

ROW_TILE = 256
SCAN_CHUNK = 128
ATTN_TQ = 1024
ATTN_TK = 512
MOE_BLOCK = 512
VMEM_LIMIT = 56 * 1024 * 1024

_HI = lax.Precision.HIGHEST


def _dot(a, b, precision=None):
    return jnp.dot(a, b, preferred_element_type=F32, precision=precision)


def _dot_nt(a, b):
    return lax.dot_general(a, b, (((1,), (1,)), ((), ())), preferred_element_type=F32)


def _dot_tn(a, b):
    return lax.dot_general(a, b, (((0,), (0,)), ((), ())), preferred_element_type=F32)


def _rms(x):
    return x * lax.rsqrt(jnp.mean(x * x, axis=-1, keepdims=True) + EPS)


def _sigmoid(x):
    return 1.0 / (1.0 + jnp.exp(-x))


def _silu(x):
    return x * _sigmoid(x)


def _log_sigmoid(x):
    return jnp.minimum(x, 0.0) - jnp.log1p(jnp.exp(-jnp.abs(x)))


def _params(n_axes):
    return pltpu.CompilerParams(dimension_semantics=("arbitrary",) * n_axes,
                                vmem_limit_bytes=VMEM_LIMIT)


def _mod_kernel(cond_ref, w_ref, b_ref, o_ref):
    s = _silu(cond_ref[...]).astype(BF16)
    o_ref[0] = _dot(s, w_ref[0].astype(BF16)) + b_ref[0]


def _modulation_all(cond, w_ada, b_ada):
    depth, d, n = w_ada.shape
    tn = n // 4
    return pl.pallas_call(
        _mod_kernel,
        grid=(depth, n // tn),
        in_specs=[pl.BlockSpec((N_COND_PAD, d), lambda l, j: (0, 0)),
                  pl.BlockSpec((1, d, tn), lambda l, j: (l, 0, j)),
                  pl.BlockSpec((1, 1, tn), lambda l, j: (l, 0, j))],
        out_specs=pl.BlockSpec((1, N_COND_PAD, tn), lambda l, j: (l, 0, j)),
        out_shape=jax.ShapeDtypeStruct((depth, N_COND_PAD, n), F32),
        compiler_params=_params(2),
        name="adaln_modulation",
    )(cond, w_ada, b_ada.reshape(depth, 1, n))


class _Rows:
    def __init__(self, n_prompt_seq, prompt_len, n_sample_seq, sample_len):
        self.prompt_len = prompt_len
        self.sample_len = sample_len
        self.n_prompt_seq = n_prompt_seq
        self.n_sample_seq = n_sample_seq
        self.n_prompt = n_prompt_seq * prompt_len
        self.n_sample = n_sample_seq * sample_len
        self.total = self.n_prompt + self.n_sample
        assert prompt_len % ROW_TILE == 0 and sample_len % ROW_TILE == 0
        assert self.n_prompt % sample_len == 0 or self.n_sample == 0

    def cond_of_tile(self, i, tile):
        n_p = self.n_prompt // tile
        per_seq = self.sample_len // tile
        return jnp.where(i < n_p, 0, 1 + (i - n_p) // per_seq)


def _proj_even_kernel(x_ref, xp_ref, xn_ref, mod_ref, gain_ref, w_ref, wg_ref, cw_ref, gb_ref,
                      qkva_ref, qkvb_ref, oz_ref, g_ref, *, tm, n_prompt, prompt_len, sample_len):
    i = pl.program_id(0)
    shift = mod_ref[0, 0:1, :]
    scale = mod_ref[0, 1:2, :]
    gain = gain_ref[...]

    def modulated(x):
        return _rms(x) * gain * (1.0 + scale) + shift

    h = modulated(x_ref[...])
    hb = h.astype(BF16)
    halo = jnp.concatenate([xp_ref[0], xn_ref[0]], axis=0)
    hh = modulated(halo).astype(BF16)

    w_qk = w_ref[:, 0:1024]
    qk = _dot(hb, w_qk)
    qk_halo = _dot(hh, w_qk)
    prev_row = qk_halo[7:8, :]
    next_row = qk_halo[8:9, :]
    local = lax.broadcasted_iota(jnp.int32, (tm, 1), 0)
    seq_len = jnp.where(i * tm < n_prompt, prompt_len, sample_len)
    pos = (i * tm + local) & (seq_len - 1)
    prev = pltpu.roll(qk, 1, axis=0)
    prev = jnp.where(local == 0, prev_row, prev)
    prev = jnp.where(pos == 0, 0.0, prev)
    nxt = pltpu.roll(qk, tm - 1, axis=0)
    nxt = jnp.where(local == tm - 1, next_row, nxt)
    nxt = jnp.where(pos == seq_len - 1, 0.0, nxt)
    cw = cw_ref[...]
    act = _silu(cw[0:1, :] * prev + cw[1:2, :] * qk + cw[2:3, :] * nxt)
    k_scale = HEAD_W ** -0.5
    qkva_ref[:, 0:512] = act[:, 0:512].astype(BF16)
    qkva_ref[:, 512:1024] = (act[:, 512:1024] * k_scale).astype(BF16)
    qkva_ref[:, 1024:1536] = _dot(hb, w_ref[:, 1024:1536]).astype(BF16)

    qkvb_ref[:, 0:512] = _dot(hb, w_ref[:, 1536:2048]).astype(BF16)
    qkvb_ref[:, 512:1024] = (_dot(hb, w_ref[:, 2048:2560]) * k_scale).astype(BF16)
    qkvb_ref[:, 1024:1536] = _dot(hb, w_ref[:, 2560:3072]).astype(BF16)
    oz_ref[:, 0:512] = _dot(hb, w_ref[:, 3072:3584]).astype(BF16)
    oz_ref[:, 512:1024] = _dot(hb, w_ref[:, 3584:4096]).astype(BF16)

    h_lo = (h - hb.astype(F32)).astype(BF16)
    g_hl = _dot(hb, wg_ref[...])
    gates = (g_hl[:, 0:256] + g_hl[:, 256:512] + _dot(h_lo, wg_ref[:, 0:256])
             + gb_ref[...])
    g_ref[:, 0:128] = gates[:, 0:128]
    g_ref[:, 128:256] = _log_sigmoid(gates[:, 128:256])


def _proj_even(x, mods, gain, w_main, w_gates, conv_w, gate_bias, rows):
    n, d = x.shape
    tm = ROW_TILE
    x8 = x.reshape(n // 8, 8, d)
    nb8 = n // 8
    kern = functools.partial(_proj_even_kernel, tm=tm, n_prompt=rows.n_prompt,
                             prompt_len=rows.prompt_len, sample_len=rows.sample_len)
    return pl.pallas_call(
        kern,
        grid=(n // tm,),
        in_specs=[pl.BlockSpec((tm, d), lambda i: (i, 0)),
                  pl.BlockSpec((1, 8, d), lambda i: (jnp.maximum(i * (tm // 8) - 1, 0), 0, 0)),
                  pl.BlockSpec((1, 8, d), lambda i: (jnp.minimum((i + 1) * (tm // 8), nb8 - 1), 0, 0)),
                  pl.BlockSpec((1, 6, d), lambda i: (rows.cond_of_tile(i, tm), 0, 0)),
                  pl.BlockSpec((1, d), lambda i: (0, 0)),
                  pl.BlockSpec(w_main.shape, lambda i: (0, 0)),
                  pl.BlockSpec(w_gates.shape, lambda i: (0, 0)),
                  pl.BlockSpec(conv_w.shape, lambda i: (0, 0)),
                  pl.BlockSpec(gate_bias.shape, lambda i: (0, 0))],
        out_specs=[pl.BlockSpec((tm, 1536), lambda i: (i, 0)),
                   pl.BlockSpec((tm, 1536), lambda i: (i, 0)),
                   pl.BlockSpec((tm, 1024), lambda i: (i, 0)),
                   pl.BlockSpec((tm, 256), lambda i: (i, 0))],
        out_shape=[jax.ShapeDtypeStruct((n, 1536), BF16),
                   jax.ShapeDtypeStruct((n, 1536), BF16),
                   jax.ShapeDtypeStruct((n, 1024), BF16),
                   jax.ShapeDtypeStruct((n, 256), F32)],
        compiler_params=_params(1),
        name="proj_even",
    )(x, x8, x8, mods, gain, w_main, w_gates, conv_w, gate_bias)


def _scan_kernel(fb_ref, bb_ref, first_ref, last_ref, sin_ref, sout_ref, isp_ref,
                 qaf_ref, qab_ref, qbf_ref, qbb_ref, gf_ref, gb_ref,
                 c0_ref, n0_ref, m0_ref, s0_ref, rl_ref,
                 hf_ref, hb_ref, cn_ref, nn_ref, mn_ref, sn_ref,
                 cext_sc, s_sc, m_sc, intra_sc, *, L):
    del fb_ref, bb_ref, sin_ref, sout_ref
    step = pl.program_id(0)
    n_units = N_DIR * H_A
    lane = lax.broadcasted_iota(jnp.int32, (HEAD_W, HEAD_W), 1)
    t_idx = lax.broadcasted_iota(jnp.int32, (L, L), 0)
    s_idx = lax.broadcasted_iota(jnp.int32, (L, L), 1)
    masks = (s_idx <= t_idx, s_idx >= t_idx)
    log_gamma = _log_sigmoid(rl_ref[...])

    @pl.when(step == 0)
    def _():
        rel = jnp.abs(t_idx - s_idx).astype(F32)
        for u in range(n_units):
            lg = log_gamma[u:u + 1, 0:1]
            intra_sc[u] = jnp.where(masks[u // H_B], jnp.exp(lg * rel), 0.0)

    @pl.when(jnp.logical_and(first_ref[step] == 1, isp_ref[step] == 1))
    def _():
        cext_sc[...] = jnp.zeros_like(cext_sc)
        s_sc[...] = jnp.zeros_like(s_sc)
        m_sc[...] = jnp.zeros_like(m_sc)

    @pl.when(jnp.logical_and(first_ref[step] == 1, isp_ref[step] == 0))
    def _():
        n0 = n0_ref[0]
        n0_t = jnp.concatenate([n0, jnp.zeros((HEAD_W - n_units, HEAD_W), F32)], axis=0).T
        for u in range(n_units):
            cext_sc[u, :, 0:HEAD_W] = c0_ref[0, u]
            cext_sc[u, :, HEAD_W:2 * HEAD_W] = jnp.where(lane == 0, n0_t[:, u:u + 1], 0.0)
            s_sc[u] = s0_ref[0, u]
        m_sc[...] = m0_ref[0]

    ones_col = jnp.where(lax.broadcasted_iota(jnp.int32, (L, HEAD_W), 1) == 0, 1.0, 0.0).astype(BF16)
    pos_f = lax.broadcasted_iota(jnp.int32, (L, 1), 0).astype(F32)
    tri = tuple(jnp.where(m, 1.0, 0.0) for m in masks)
    qa = (qaf_ref, qab_ref)
    qb = (qbf_ref, qbb_ref)
    g = (gf_ref, gb_ref)
    out = (hf_ref, hb_ref)

    for d in range(N_DIR):
        ig_all = g[d][:, 0:HEAD_W]
        lf_all = g[d][:, HEAD_W:2 * HEAD_W]
        bt_all = _dot(tri[d], lf_all, precision=_HI)
        ig_t = ig_all.T
        bt_t = bt_all.T
        end = L - 1 if d == 0 else 0
        pos = pos_f if d == 0 else (L - 1.0) - pos_f
        for h in range(H_A):
            u = d * H_A + h
            cs = slice(h * HEAD_W, (h + 1) * HEAD_W)
            bt_col = bt_all[:, u:u + 1]
            ig_col = ig_all[:, u:u + 1]
            bt_row = bt_t[u:u + 1, :]
            ig_row = ig_t[u:u + 1, :]
            m_prev = m_sc[u:u + 1, 0:1]
            b_end = bt_col[end:end + 1, :]
            log_d = jnp.where(masks[d], bt_col - bt_row + ig_row, -jnp.inf)
            inter = bt_col + m_prev
            m_t = jnp.maximum(inter, jnp.max(log_d, axis=1, keepdims=True))
            inter_w = jnp.exp(inter - m_t)
            dmat = jnp.exp(log_d - m_t)
            q = qa[d][:, cs]
            k = qa[d][:, 512 + h * HEAD_W:512 + (h + 1) * HEAD_W]
            v = qa[d][:, 1024 + h * HEAD_W:1024 + (h + 1) * HEAD_W]
            v_ext = jnp.concatenate([v, ones_col], axis=1)
            s = (_dot_nt(q, k) * dmat).astype(BF16)
            num = _dot(s, v_ext) + inter_w * _dot(q, cext_sc[u].astype(BF16))
            den = num[:, HEAD_W:HEAD_W + 1]
            inv = 1.0 / jnp.maximum(jnp.abs(den), jnp.exp(-m_t))
            out[d][:, cs] = (num[:, 0:HEAD_W] * inv).astype(out[d].dtype)
            log_w = b_end - bt_col + ig_col
            m_new = jnp.maximum(b_end + m_prev, jnp.max(log_w, axis=0, keepdims=True))
            wk = jnp.exp(log_w - m_new)
            decay = jnp.exp(b_end + m_prev - m_new)
            upd = _dot_tn(k, (wk * v_ext.astype(F32)).astype(BF16))
            cext_sc[u] = decay * cext_sc[u] + upd
            m_sc[u:u + 1, :] = jnp.broadcast_to(m_new, (1, HEAD_W))
            lg = log_gamma[u:u + 1, 0:1]
            qr = qb[d][:, cs]
            kr = qb[d][:, 512 + h * HEAD_W:512 + (h + 1) * HEAD_W]
            vr = qb[d][:, 1024 + h * HEAD_W:1024 + (h + 1) * HEAD_W]
            sr = (_dot_nt(qr, kr) * intra_sc[u]).astype(BF16)
            o = _dot(sr, vr) + _dot(qr, s_sc[u].astype(BF16)) * jnp.exp(lg * (pos + 1.0))
            out[d][:, 512 + h * HEAD_W:512 + (h + 1) * HEAD_W] = o.astype(out[d].dtype)
            to_end = jnp.exp(lg * ((L - 1.0) - pos))
            s_sc[u] = jnp.exp(lg * float(L)) * s_sc[u] + _dot_tn(kr, (to_end * vr.astype(F32)).astype(BF16))

    @pl.when(last_ref[step] == 1)
    def _():
        n_cols = jnp.zeros((HEAD_W, HEAD_W), F32)
        for u in range(n_units):
            cn_ref[0, u] = cext_sc[u, :, 0:HEAD_W]
            sn_ref[0, u] = s_sc[u]
            n_cols = jnp.where(lane == u, cext_sc[u, :, HEAD_W:HEAD_W + 1], n_cols)
        nn_ref[0] = n_cols.T[0:n_units, :]
        mn_ref[0] = m_sc[...]


def _scan_tables(rows, L):
    fb, bb, first, last, sin, sout, isp = [], [], [], [], [], [], []
    base = 0
    for kind, n_seq, seq_len in (("p", rows.n_prompt_seq, rows.prompt_len),
                                 ("s", rows.n_sample_seq, rows.sample_len)):
        nc = seq_len // L
        for b in range(n_seq):
            for c in range(nc):
                fb.append(base + b * nc + c)
                bb.append(base + b * nc + nc - 1 - c)
                first.append(int(c == 0))
                last.append(int(c == nc - 1 and kind == "p"))
                sin.append(b if kind == "s" else 0)
                sout.append(b if kind == "p" else rows.n_prompt_seq - 1)
                isp.append(int(kind == "p"))
        base += n_seq * nc
    return [jnp.asarray(np.asarray(t, np.int32)) for t in (fb, bb, first, last, sin, sout, isp)]


def _scan_even(qkva, qkvb, gates, c0, n0, m0, s0, ret_logit, rows):
    L = SCAN_CHUNK
    n = qkva.shape[0]
    tables = _scan_tables(rows, L)
    n_steps = int(tables[0].shape[0])
    nu = N_DIR * H_A
    fwd = lambda w: pl.BlockSpec((L, w), lambda s, fb, bb, fi, la, si, so, ip: (fb[s], 0))
    bwd = lambda w: pl.BlockSpec((L, w), lambda s, fb, bb, fi, la, si, so, ip: (bb[s], 0))
    st_in4 = pl.BlockSpec((1, nu, HEAD_W, HEAD_W), lambda s, fb, bb, fi, la, si, so, ip: (si[s], 0, 0, 0))
    st_in3 = pl.BlockSpec((1, nu, HEAD_W), lambda s, fb, bb, fi, la, si, so, ip: (si[s], 0, 0))
    st_out4 = pl.BlockSpec((1, nu, HEAD_W, HEAD_W), lambda s, fb, bb, fi, la, si, so, ip: (so[s], 0, 0, 0))
    st_out3 = pl.BlockSpec((1, nu, HEAD_W), lambda s, fb, bb, fi, la, si, so, ip: (so[s], 0, 0))
    nps = rows.n_prompt_seq
    return pl.pallas_call(
        functools.partial(_scan_kernel, L=L),
        grid_spec=pltpu.PrefetchScalarGridSpec(
            num_scalar_prefetch=7,
            grid=(n_steps,),
            in_specs=[fwd(1536), bwd(1536), fwd(1536), bwd(1536), fwd(256), bwd(256),
                      st_in4, st_in3, st_in3, st_in4,
                      pl.BlockSpec((nu, HEAD_W), lambda s, *_: (0, 0))],
            out_specs=[fwd(1024), bwd(1024), st_out4, st_out3, st_out3, st_out4],
            scratch_shapes=[pltpu.VMEM((nu, HEAD_W, 2 * HEAD_W), F32),
                            pltpu.VMEM((nu, HEAD_W, HEAD_W), F32),
                            pltpu.VMEM((nu, HEAD_W), F32),
                            pltpu.VMEM((nu, L, L), F32)]),
        out_shape=[jax.ShapeDtypeStruct((n, 1024), BF16),
                   jax.ShapeDtypeStruct((n, 1024), BF16),
                   jax.ShapeDtypeStruct((nps, nu, HEAD_W, HEAD_W), F32),
                   jax.ShapeDtypeStruct((nps, nu, HEAD_W), F32),
                   jax.ShapeDtypeStruct((nps, nu, HEAD_W), F32),
                   jax.ShapeDtypeStruct((nps, nu, HEAD_W, HEAD_W), F32)],
        compiler_params=_params(1),
        name="scan_even",
    )(*tables, qkva, qkva, qkvb, qkvb, gates, gates, c0, n0, m0, s0, ret_logit)


def _proj_odd_kernel(x_ref, mod_ref, gain_ref, w_ref, e_ref, qg_ref, kg_ref, cos_ref, sin_ref,
                     q_ref, k_ref, v_ref, *, rope):
    shift = mod_ref[0, 0:1, :]
    scale = mod_ref[0, 1:2, :]
    hb = (_rms(x_ref[...]) * gain_ref[...] * (1.0 + scale) + shift).astype(BF16)

    def qk_norm(raw, g):
        ss = _dot((raw * raw).astype(BF16), e_ref[...])
        return raw * lax.rsqrt(ss * (1.0 / DH_C) + EPS) * g

    def rotate(y):
        if not rope:
            return y
        lane = lax.broadcasted_iota(jnp.int32, y.shape, 1)
        first_half = (lane & 31) < 16
        partner = jnp.where(first_half, pltpu.roll(y, y.shape[1] - 16, axis=1), pltpu.roll(y, 16, axis=1))
        cos = jnp.concatenate([cos_ref[...]] * (y.shape[1] // HEAD_W), axis=1)
        sin = jnp.concatenate([sin_ref[...]] * (y.shape[1] // HEAD_W), axis=1)
        return y * cos + partner * sin

    q = rotate(qk_norm(_dot(hb, w_ref[:, 0:1024]), qg_ref[...]))
    q_ref[...] = (q * (DH_C ** -0.5 * math.log2(math.e))).astype(q_ref.dtype)
    k = rotate(qk_norm(_dot(hb, w_ref[:, 1024:2048]), kg_ref[...]))
    k_ref[...] = k.astype(k_ref.dtype)
    v_ref[...] = _dot(hb, w_ref[:, 2048:3072]).astype(v_ref.dtype)


def _proj_odd(x, mods, gain, w_qkv, e64, q_gain, k_gain, cos, sin, rows, *, sample):
    d = x.shape[1]
    tm = ROW_TILE
    if sample:
        n, base, kv_dtype = rows.n_sample, rows.n_prompt // tm, BF16
        per_seq = rows.sample_len // tm
        table = lambda i: (i % per_seq, 0)
    else:
        n, base, kv_dtype = rows.n_prompt, 0, F32
        table = lambda i: (0, 0)
    return pl.pallas_call(
        functools.partial(_proj_odd_kernel, rope=sample),
        grid=(n // tm,),
        in_specs=[pl.BlockSpec((tm, d), lambda i: (base + i, 0)),
                  pl.BlockSpec((1, 6, d), lambda i: (rows.cond_of_tile(base + i, tm), 0, 0)),
                  pl.BlockSpec((1, d), lambda i: (0, 0)),
                  pl.BlockSpec(w_qkv.shape, lambda i: (0, 0)),
                  pl.BlockSpec(e64.shape, lambda i: (0, 0)),
                  pl.BlockSpec((1, d), lambda i: (0, 0)),
                  pl.BlockSpec((1, d), lambda i: (0, 0)),
                  pl.BlockSpec((tm, HEAD_W), table),
                  pl.BlockSpec((tm, HEAD_W), table)],
        out_specs=[pl.BlockSpec((tm, d), lambda i: (i, 0))] * 3,
        out_shape=[jax.ShapeDtypeStruct((n, d), BF16),
                   jax.ShapeDtypeStruct((n, d), kv_dtype),
                   jax.ShapeDtypeStruct((n, d), kv_dtype)],
        compiler_params=_params(1),
        name="proj_odd_sample" if sample else "proj_odd_prompt",
    )(x, mods, gain, w_qkv, e64, q_gain, k_gain, cos, sin)


def _rope_tables(sample_len):
    t = np.arange(sample_len)
    nf = DH_C // 4
    inv = ROPE_BASE ** (-np.arange(nf, dtype=np.float32) / nf)
    row = (t // GRID_W).astype(np.float32)[:, None] * inv[None, :]
    col = (t % GRID_W).astype(np.float32)[:, None] * inv[None, :]
    cos64 = np.concatenate([np.cos(row), np.cos(row), np.cos(col), np.cos(col)], axis=1)
    sin64 = np.concatenate([-np.sin(row), np.sin(row), -np.sin(col), np.sin(col)], axis=1)
    cos = np.concatenate([cos64, cos64], axis=1).astype(np.float32)
    sin = np.concatenate([sin64, sin64], axis=1).astype(np.float32)
    return jnp.asarray(cos), jnp.asarray(sin)


def _attn_kernel(*refs, tq, tk, n_main, n_cache, lam_init):
    if n_cache:
        q_ref, k_ref, v_ref, ck_ref, cv_ref, lq_ref, og_ref, o_ref, vt_sc = refs
    else:
        q_ref, k_ref, v_ref, lq_ref, og_ref, o_ref, vt_sc = refs
    fill_w = 256

    @pl.when(pl.program_id(2) == 0)
    def _():
        ones_rows = jnp.where(lax.broadcasted_iota(jnp.int32, (8, fill_w), 0) == 0, 1.0, 0.0).astype(BF16)
        for j in range(n_main // fill_w):
            vt = v_ref[0, j * fill_w:(j + 1) * fill_w, :].astype(F32).T.astype(BF16)
            vt_sc[:, j * fill_w:(j + 1) * fill_w] = jnp.concatenate([vt, ones_rows], axis=0)
        if n_cache:
            vt = cv_ref[0].astype(F32).T.astype(BF16)
            vt_sc[:, n_main:n_main + n_cache] = jnp.concatenate([vt, ones_rows], axis=0)

    qt = q_ref[0].astype(F32).T
    row = lax.broadcasted_iota(jnp.int32, qt.shape, 0)
    qst = jnp.concatenate([jnp.where(row < DH_C, qt, 0.0), jnp.where(row >= DH_C, qt, 0.0)],
                          axis=1).astype(BF16)

    def scores(kt):
        return _dot(kt.astype(BF16), qst)

    def accumulate(carry, s, vt):
        m, acc = carry
        m_new = jnp.maximum(m, jnp.max(s, axis=0, keepdims=True))
        p = jnp.exp2(s - m_new).astype(BF16)
        acc = jnp.exp2(m - m_new) * acc + _dot(vt, p)
        return m_new, acc

    tiles = [(k_ref, j * tk, tk, j * tk) for j in range(n_main // tk)]
    if n_cache:
        tiles = [(ck_ref, 0, n_cache, n_main)] + tiles
    carry = (jnp.full((1, 2 * tq), -jnp.inf, F32), jnp.zeros((HEAD_W + 8, 2 * tq), F32))
    ref0, r0, n0, _ = tiles[0]
    s_next = scores(ref0[0, r0:r0 + n0, :])
    for j, (_, _, n_keys, c0) in enumerate(tiles):
        s_cur = s_next
        if j + 1 < len(tiles):
            ref1, r1, n1, _ = tiles[j + 1]
            s_next = scores(ref1[0, r1:r1 + n1, :])
        carry = accumulate(carry, s_cur, vt_sc[:, c0:c0 + n_keys])
    _, acc = carry
    o_t = acc[0:HEAD_W, :] / acc[HEAD_W:HEAD_W + 1, :]
    lq = lq_ref[...]
    lam = (jnp.exp(jnp.sum(lq[0:1, :] * lq[1:2, :], axis=1, keepdims=True))
           - jnp.exp(jnp.sum(lq[2:3, :] * lq[3:4, :], axis=1, keepdims=True)) + lam_init)
    o = (o_t[:, 0:tq] - lam * o_t[:, tq:2 * tq]).T
    o_ref[0] = (_rms(o) * og_ref[...] * (1.0 - lam_init)).astype(o_ref.dtype)


def _attention(q, k, v, cache_k, cache_v, lam_qk, out_gain, lam_init):
    b, t, d = q.shape
    tq = min(ATTN_TQ, t)
    tk = min(ATTN_TK, t)
    n_cache = 0 if cache_k is None else cache_k.shape[1]
    assert t % tk == 0 and t % tq == 0 and t % 256 == 0 and n_cache in (0, 256)
    seq = lambda n: pl.BlockSpec((1, n, HEAD_W), lambda bi, h, qi: (bi, 0, h))
    in_specs = [pl.BlockSpec((1, tq, HEAD_W), lambda bi, h, qi: (bi, qi, h)), seq(t), seq(t)]
    args = [q, k, v]
    if n_cache:
        in_specs += [seq(n_cache), seq(n_cache)]
        args += [cache_k, cache_v]
    in_specs += [pl.BlockSpec(lam_qk.shape, lambda bi, h, qi: (0, 0)),
                 pl.BlockSpec(out_gain.shape, lambda bi, h, qi: (0, 0))]
    args += [lam_qk, out_gain]
    return pl.pallas_call(
        functools.partial(_attn_kernel, tq=tq, tk=tk, n_main=t, n_cache=n_cache, lam_init=lam_init),
        grid=(b, H_C, t // tq),
        in_specs=in_specs,
        out_specs=pl.BlockSpec((1, tq, HEAD_W), lambda bi, h, qi: (bi, qi, h)),
        out_shape=jax.ShapeDtypeStruct((b, t, d), BF16),
        scratch_shapes=[pltpu.VMEM((HEAD_W + 8, t + n_cache), BF16)],
        compiler_params=_params(3),
        name="diff_attention",
    )(*args)


def _out_kernel(*refs, tm, even, n_prompt_tiles):
    if even:
        (hf_ref, hb_ref, oz_ref, ga_ref, gb_ref, x_ref, mod_ref, w_ref, fg_ref, wr_ref, rb_ref,
         x1_ref, h2_ref, e_ref, gw_ref, rk_ref, c1_ref, c2_ref, c1_sc, c2_sc) = refs
    else:
        (op_ref, os_ref, x_ref, mod_ref, w_ref, fg_ref, wr_ref, rb_ref,
         x1_ref, h2_ref, e_ref, gw_ref, rk_ref, c1_ref, c2_ref, c1_sc, c2_sc) = refs
    i = pl.program_id(0)

    @pl.when(i == 0)
    def _():
        c1_sc[...] = jnp.zeros_like(c1_sc)
        c2_sc[...] = jnp.zeros_like(c2_sc)

    if even:
        hs = hf_ref[...].astype(F32) + hb_ref[...].astype(F32)
        oz = oz_ref[...].astype(F32)
        parts = []
        for h in range(H_A + H_B):
            cs = slice(h * HEAD_W, (h + 1) * HEAD_W)
            gain = ga_ref[:, cs] if h < H_A else gb_ref[:, (h - H_A) * HEAD_W:(h - H_A + 1) * HEAD_W]
            act = _sigmoid(oz[:, cs]) if h < H_A else _silu(oz[:, cs])
            parts.append((_rms(hs[:, cs]) * gain * act).astype(BF16))
        y_in = jnp.concatenate(parts, axis=1)
    else:
        y_in = jnp.where(i < n_prompt_tiles, op_ref[...], os_ref[...])

    x1 = x_ref[...] + mod_ref[0, 2:3, :] * _dot(y_in, w_ref[...])
    x1_ref[...] = x1
    h2 = _rms(x1) * fg_ref[...] * (1.0 + mod_ref[0, 4:5, :]) + mod_ref[0, 3:4, :]
    h2_ref[...] = h2.astype(BF16)

    score = _sigmoid(_dot(h2, wr_ref[...], precision=_HI))
    st = score.T[0:N_EXPERTS, :]
    sel = st + rb_ref[:, 0:1]
    srow = [st[e:e + 1, :] for e in range(N_EXPERTS)]
    brow = [sel[e:e + 1, :] for e in range(N_EXPERTS)]
    epg = EXPERTS_PER_GROUP
    gscore = []
    for g in range(N_GROUPS):
        a = brow[g * epg:(g + 1) * epg]
        best = None
        for p in range(epg):
            for r in range(p + 1, epg):
                pair = a[p] + a[r]
                best = pair if best is None else jnp.maximum(best, pair)
        gscore.append(best)
    gbest, gidx = gscore[0], jnp.zeros((1, tm), jnp.int32)
    for g in range(1, N_GROUPS):
        better = gscore[g] > gbest
        gbest = jnp.where(better, gscore[g], gbest)
        gidx = jnp.where(better, g, gidx)
    vals, sig = [], []
    for p in range(epg):
        vp, sp = brow[p], srow[p]
        for g in range(1, N_GROUPS):
            vp = jnp.where(gidx == g, brow[g * epg + p], vp)
            sp = jnp.where(gidx == g, srow[g * epg + p], sp)
        vals.append(vp)
        sig.append(sp)
    v1, i1, w1 = vals[0], jnp.zeros((1, tm), jnp.int32), sig[0]
    for p in range(1, epg):
        better = vals[p] > v1
        v1 = jnp.where(better, vals[p], v1)
        i1 = jnp.where(better, p, i1)
        w1 = jnp.where(better, sig[p], w1)
    v2 = jnp.full((1, tm), -jnp.inf, F32)
    i2 = jnp.zeros((1, tm), jnp.int32)
    w2 = jnp.zeros((1, tm), F32)
    for p in range(epg):
        better = jnp.logical_and(i1 != p, vals[p] > v2)
        v2 = jnp.where(better, vals[p], v2)
        i2 = jnp.where(better, p, i2)
        w2 = jnp.where(better, sig[p], w2)
    e1 = gidx * epg + i1
    e2 = gidx * epg + i2
    wsum = w1 + w2

    erow = lax.broadcasted_iota(jnp.int32, (N_EXPERTS, tm), 0)
    earlier = jnp.where(lax.broadcasted_iota(jnp.int32, (tm, tm), 0)
                        < lax.broadcasted_iota(jnp.int32, (tm, tm), 1), 1.0, 0.0).astype(BF16)
    ranks = []
    for e_k, c_sc in ((e1, c1_sc), (e2, c2_sc)):
        onehot = jnp.where(erow == e_k, 1.0, 0.0)
        before = _dot(onehot.astype(BF16), earlier) + c_sc[:, 0:1]
        ranks.append(jnp.sum(onehot * before, axis=0, keepdims=True))
        c_sc[...] = c_sc[...] + jnp.sum(onehot, axis=1, keepdims=True)
    zi = jnp.zeros((6, tm), jnp.int32)
    e_ref[...] = jnp.concatenate([e1, e2, zi], axis=0)
    gw_ref[...] = jnp.concatenate([w1 / wsum, w2 / wsum, jnp.zeros((6, tm), F32)], axis=0)
    rk_ref[...] = jnp.concatenate([ranks[0].astype(jnp.int32), ranks[1].astype(jnp.int32), zi], axis=0)
    c1_ref[...] = c1_sc[...]
    c2_ref[...] = c2_sc[...]


def _out_and_route(mix_in, x, mods, w_out, ffn_gain, w_router, router_bias, rows, *, even):
    n, d = x.shape
    tm = ROW_TILE
    row = lambda w: pl.BlockSpec((tm, w), lambda i: (i, 0))
    full = lambda a: pl.BlockSpec(a.shape, lambda i: (0,) * a.ndim)
    n_pt = rows.n_prompt // tm
    if even:
        hf, hb, oz, gain_a, gain_b = mix_in
        in_specs = [row(1024), row(1024), row(1024), full(gain_a), full(gain_b)]
        args = [hf, hb, oz, gain_a, gain_b]
    else:
        o_p, o_s = mix_in
        in_specs = [pl.BlockSpec((tm, d), lambda i: (jnp.minimum(i, n_pt - 1), 0)),
                    pl.BlockSpec((tm, d), lambda i: (jnp.maximum(i - n_pt, 0), 0))]
        args = [o_p, o_s]
    in_specs += [row(d), pl.BlockSpec((1, 6, d), lambda i: (rows.cond_of_tile(i, tm), 0, 0)),
                 full(w_out), full(ffn_gain), full(w_router), full(router_bias)]
    args += [x, mods, w_out, ffn_gain, w_router, router_bias]
    col = lambda: pl.BlockSpec((8, tm), lambda i: (0, i))
    cnt = lambda: pl.BlockSpec((N_EXPERTS, HEAD_W), lambda i: (0, 0))
    return pl.pallas_call(
        functools.partial(_out_kernel, tm=tm, even=even, n_prompt_tiles=n_pt),
        grid=(n // tm,),
        in_specs=in_specs,
        out_specs=[row(d), row(d), col(), col(), col(), cnt(), cnt()],
        out_shape=[jax.ShapeDtypeStruct((n, d), F32),
                   jax.ShapeDtypeStruct((n, d), BF16),
                   jax.ShapeDtypeStruct((8, n), jnp.int32),
                   jax.ShapeDtypeStruct((8, n), F32),
                   jax.ShapeDtypeStruct((8, n), jnp.int32),
                   jax.ShapeDtypeStruct((N_EXPERTS, HEAD_W), F32),
                   jax.ShapeDtypeStruct((N_EXPERTS, HEAD_W), F32)],
        scratch_shapes=[pltpu.VMEM((N_EXPERTS, HEAD_W), F32), pltpu.VMEM((N_EXPERTS, HEAD_W), F32)],
        compiler_params=_params(1),
        name="out_even" if even else "out_odd",
    )(*args)


def _moe_kernel(be_ref, nu_ref, x_ref, wg_ref, wu_ref, wd_ref, y_ref, wg_sc, wu_sc, wd_sc):
    i = pl.program_id(0)
    changed = jnp.logical_or(i == 0, be_ref[i] != be_ref[jnp.maximum(i - 1, 0)])

    @pl.when(jnp.logical_and(changed, i < nu_ref[0]))
    def _():
        wg_sc[...] = wg_ref[0, 0].astype(BF16)
        wu_sc[...] = wu_ref[0, 0].astype(BF16)
        wd_sc[...] = wd_ref[0, 0].astype(BF16)

    @pl.when(i < nu_ref[0])
    def _():
        x = x_ref[...]
        a = (_silu(_dot(x, wg_sc[...])) * _dot(x, wu_sc[...])).astype(BF16)
        y_ref[...] = _dot(a, wd_sc[...]).astype(y_ref.dtype)

    @pl.when(i >= nu_ref[0])
    def _():
        y_ref[...] = jnp.zeros_like(y_ref)


def _moe_experts(xs, block_e, n_used, w_gate, w_up, w_down, layer):
    p, d = xs.shape
    bm = MOE_BLOCK
    wspec = pl.BlockSpec((1, 1, d, d), lambda i, be, nu: (layer, be[i], 0, 0))
    return pl.pallas_call(
        _moe_kernel,
        grid_spec=pltpu.PrefetchScalarGridSpec(
            num_scalar_prefetch=2,
            grid=(p // bm,),
            in_specs=[pl.BlockSpec((bm, d), lambda i, be, nu: (i, 0)), wspec, wspec, wspec],
            out_specs=pl.BlockSpec((bm, d), lambda i, be, nu: (i, 0)),
            scratch_shapes=[pltpu.VMEM((d, d), BF16)] * 3),
        out_shape=jax.ShapeDtypeStruct((p, d), BF16),
        compiler_params=_params(1),
        name="moe_experts",
    )(block_e, n_used, xs, w_gate, w_up, w_down)


def _combine_kernel(x_ref, y_ref, gw_ref, mod_ref, o_ref):
    w = gw_ref[...]
    y = w[:, 0:1] * y_ref[0].astype(F32) + w[:, 1:2] * y_ref[1].astype(F32)
    o_ref[...] = x_ref[...] + mod_ref[0, 5:6, :] * y


def _combine(x1, y_pairs, gw_cols, mods, rows):
    n, d = x1.shape
    tm = ROW_TILE
    return pl.pallas_call(
        _combine_kernel,
        grid=(n // tm,),
        in_specs=[pl.BlockSpec((tm, d), lambda i: (i, 0)),
                  pl.BlockSpec((2, tm, d), lambda i: (0, i, 0)),
                  pl.BlockSpec((tm, 2), lambda i: (i, 0)),
                  pl.BlockSpec((1, 6, d), lambda i: (rows.cond_of_tile(i, tm), 0, 0))],
        out_specs=pl.BlockSpec((tm, d), lambda i: (i, 0)),
        out_shape=jax.ShapeDtypeStruct((n, d), F32),
        compiler_params=_params(1),
        name="moe_combine",
    )(x1, y_pairs, gw_cols, mods)


def _moe_layer(x1, h2, eidx, gw, rank, cnt1, cnt2, mods, w_gate, w_up, w_down, layer, rows):
    n = x1.shape[0]
    bm = MOE_BLOCK
    n_blocks = (2 * n + N_EXPERTS * (bm - 1) + bm - 1) // bm
    c1 = cnt1[:, 0].astype(jnp.int32)
    c2 = cnt2[:, 0].astype(jnp.int32)
    padded = (c1 + c2 + bm - 1) // bm * bm
    pad_ends = jnp.cumsum(padded)
    pad_starts = pad_ends - padded
    e1, e2 = eidx[0], eidx[1]
    slot1 = pad_starts[e1] + rank[0]
    slot2 = pad_starts[e2] + c1[e2] + rank[1]
    slots = jnp.stack([slot1, slot2], axis=0)
    tok = jnp.arange(n, dtype=jnp.int32)
    slot_token = jnp.zeros((n_blocks * bm,), jnp.int32).at[slots.reshape(-1)].set(jnp.concatenate([tok, tok]))
    block_start = jnp.arange(n_blocks, dtype=jnp.int32) * bm
    block_e = jnp.minimum(jnp.sum((pad_ends[None, :] <= block_start[:, None]).astype(jnp.int32), axis=1),
                          N_EXPERTS - 1)
    n_used = (pad_ends[-1] // bm).astype(jnp.int32).reshape(1)
    xs = h2.at[slot_token].get(mode="promise_in_bounds")
    ys = _moe_experts(xs, block_e, n_used, w_gate, w_up, w_down, layer)
    y_pairs = ys.at[slots].get(mode="promise_in_bounds")
    return _combine(x1, y_pairs, gw[0:2].T, mods, rows)


def kernel(x_prompt, x_sample, c, state_mlstm_C, state_mlstm_n, state_mlstm_m, state_ret_S, cache_k, cache_v, c_ctx, w_ada, b_ada, norm_mix_gain, norm_ffn_gain, w_in_even, mlstm_conv, mlstm_gate_bias, mlstm_out_gain, ret_decay_logit, ret_out_gain, w_out_even, w_qkv_odd, q_norm_gain, k_norm_gain, lambda_qk, attn_out_gain, w_out_odd, w_router, router_bias, moe_w_gate, moe_w_up, moe_w_down):
    bp, seq, d = x_prompt.shape
    bs, dec_seq, _ = x_sample.shape
    depth = w_ada.shape[0]
    past = cache_k.shape[2]
    rows = _Rows(bp, seq, bs, dec_seq)
    nu = N_DIR * H_A
    assert 1 + bs <= N_COND_PAD

    x = jnp.concatenate([x_prompt.reshape(bp * seq, d), x_sample.reshape(bs * dec_seq, d)], axis=0)
    cond = jnp.concatenate([c_ctx[None, :], c, jnp.zeros((N_COND_PAD - 1 - bs, d), F32)], axis=0)
    mods_all = _modulation_all(cond, w_ada, b_ada).reshape(depth, N_COND_PAD, 6, d)

    w_router_pad = jnp.pad(w_router, ((0, 0), (0, HEAD_W - N_EXPERTS)))
    router_bias_col = jnp.broadcast_to(router_bias[:, None], (N_EXPERTS, HEAD_W))
    e64 = jnp.asarray(np.kron(np.eye(d // DH_C, dtype=np.float32), np.ones((DH_C, DH_C), np.float32)), BF16)
    cos, sin = _rope_tables(dec_seq)

    st_c, st_n, st_m, st_s, st_k, st_v = [], [], [], [], [], []
    for l in range(depth):
        j = l // 2
        mods = mods_all[l]
        gain_mix = norm_mix_gain[l][None, :]
        if l % 2 == 0:
            w = w_in_even[j]
            wa = 4 * HEAD_W * 4
            gcols = 4 * H_A
            ob = wa + gcols
            w_main = jnp.concatenate([w[:, 0:1536], w[:, ob:ob + 1536], w[:, 1536:2048],
                                      w[:, ob + 1536:ob + 2048]], axis=1).astype(BF16)
            w_g = w[:, wa:wa + gcols]
            zpad = jnp.zeros((d, HEAD_W - nu), F32)
            w_g32 = jnp.concatenate([w_g[:, 0:nu], zpad, w_g[:, nu:2 * nu], zpad], axis=1)
            w_g_hi = w_g32.astype(BF16)
            w_gates = jnp.concatenate([w_g_hi, (w_g32 - w_g_hi.astype(F32)).astype(BF16)], axis=1)
            gb = mlstm_gate_bias[j].reshape(2, nu)
            zb = jnp.zeros((HEAD_W - nu,), F32)
            gate_bias = jnp.concatenate([gb[0], zb, gb[1], zb])[None, :]
            qkva, qkvb, oz, gates = _proj_even(x, mods, gain_mix, w_main, w_gates, mlstm_conv[j], gate_bias, rows)
            c0 = state_mlstm_C[:, j].reshape(bs, nu, HEAD_W, HEAD_W)
            n0 = state_mlstm_n[:, j].reshape(bs, nu, HEAD_W)
            m0 = jnp.broadcast_to(state_mlstm_m[:, j].reshape(bs, nu, 1), (bs, nu, HEAD_W))
            s0 = state_ret_S[:, j].reshape(bs, nu, HEAD_W, HEAD_W)
            ret_logit = jnp.broadcast_to(ret_decay_logit[j].reshape(nu, 1), (nu, HEAD_W))
            hf, hb, cn, nn, mn, sn = _scan_even(qkva, qkvb, gates, c0, n0, m0, s0, ret_logit, rows)
            st_c.append(cn.reshape(bp, N_DIR, H_A, HEAD_W, HEAD_W))
            st_n.append(nn.reshape(bp, N_DIR, H_A, HEAD_W))
            st_m.append(mn[:, :, 0].reshape(bp, N_DIR, H_A))
            st_s.append(sn.reshape(bp, N_DIR, H_B, HEAD_W, HEAD_W))
            mix_in = (hf, hb, oz, mlstm_out_gain[j].reshape(1, H_A * HEAD_W),
                      ret_out_gain[j].reshape(1, H_B * HEAD_W))
            w_out = w_out_even[j].astype(BF16)
        else:
            lam_init = 0.8 - 0.6 * math.exp(-0.3 * l)
            w_qkv = w_qkv_odd[j].astype(BF16)
            qg = jnp.tile(q_norm_gain[j], d // DH_C)[None, :]
            kg = jnp.tile(k_norm_gain[j], d // DH_C)[None, :]
            og = attn_out_gain[j][None, :]
            q_p, k_p, v_p = _proj_odd(x, mods, gain_mix, w_qkv, e64, qg, kg, cos, sin, rows, sample=False)
            q_s, k_s, v_s = _proj_odd(x, mods, gain_mix, w_qkv, e64, qg, kg, cos, sin, rows, sample=True)
            o_p = _attention(q_p.reshape(bp, seq, d), k_p.reshape(bp, seq, d), v_p.reshape(bp, seq, d),
                             None, None, lambda_qk[j], og, lam_init)
            o_s = _attention(q_s.reshape(bs, dec_seq, d), k_s.reshape(bs, dec_seq, d), v_s.reshape(bs, dec_seq, d),
                             cache_k[:, j].reshape(bs, past, d), cache_v[:, j].reshape(bs, past, d),
                             lambda_qk[j], og, lam_init)
            st_k.append(k_p.reshape(bp, seq, H_C, 2, DH_C))
            st_v.append(v_p.reshape(bp, seq, H_C, 2 * DH_C))
            mix_in = (o_p.reshape(bp * seq, d), o_s.reshape(bs * dec_seq, d))
            w_out = w_out_odd[j].astype(BF16)
        x1, h2, eidx, gw, rank, cnt1, cnt2 = _out_and_route(
            mix_in, x, mods, w_out, norm_ffn_gain[l][None, :], w_router_pad, router_bias_col, rows,
            even=(l % 2 == 0))
        x = _moe_layer(x1, h2, eidx, gw, rank, cnt1, cnt2, mods, moe_w_gate, moe_w_up, moe_w_down, l, rows)

    dt = x_prompt.dtype
    y_prompt = x[:rows.n_prompt].reshape(bp, seq, d)
    y_sample = x[rows.n_prompt:].reshape(bs, dec_seq, d)
    return (y_prompt, y_sample,
            jnp.stack(st_c, axis=1).astype(dt), jnp.stack(st_n, axis=1).astype(dt),
            jnp.stack(st_m, axis=1).astype(dt), jnp.stack(st_s, axis=1).astype(dt),
            jnp.stack(st_k, axis=1).astype(dt), jnp.stack(st_v, axis=1).astype(dt))
```

```python
import functools
import math

import numpy as np
import jax
import jax.numpy as jnp
from jax import lax
from jax.experimental import pallas as pl
from jax.experimental.pallas import tpu as pltpu

F32 = jnp.float32
BF16 = jnp.bfloat16

EPS = 1e-6
GRID_W = 64
ROPE_BASE = 10000.0
H_A = 4
H_B = 4
H_C = 8
N_DIR = 2
N_EXPERTS = 16
N_GROUPS = 4
EXPERTS_PER_GROUP = N_EXPERTS // N_GROUPS
HEAD_W = 128
DH_C = 64
N_COND_PAD = 16

ROW_TILE = 256
SCAN_CHUNK = 256
ATTN_TQ = 1024
ATTN_TK = 512
MOE_BLOCK = 512
VMEM_LIMIT = 56 * 1024 * 1024

_HI = lax.Precision.HIGHEST


def _dot(a, b, precision=None):
    return jnp.dot(a, b, preferred_element_type=F32, precision=precision)


def _dot_nt(a, b):
    return lax.dot_general(a, b, (((1,), (1,)), ((), ())), preferred_element_type=F32)


def _dot_tn(a, b):
    return lax.dot_general(a, b, (((0,), (0,)), ((), ())), preferred_element_type=F32)


def _rms(x):
    return x * lax.rsqrt(jnp.mean(x * x, axis=-1, keepdims=True) + EPS)


def _sigmoid(x):
    return 1.0 / (1.0 + jnp.exp(-x))


def _silu(x):
    return x * _sigmoid(x)


def _log_sigmoid(x):
    return jnp.minimum(x, 0.0) - jnp.log1p(jnp.exp(-jnp.abs(x)))


def _params(n_axes):
    return pltpu.CompilerParams(dimension_semantics=("arbitrary",) * n_axes,
                                vmem_limit_bytes=VMEM_LIMIT)


def _mod_kernel(cond_ref, w_ref, b_ref, o_ref):
    s = _silu(cond_ref[...]).astype(BF16)
    o_ref[0] = _dot(s, w_ref[0].astype(BF16)) + b_ref[0]


def _modulation_all(cond, w_ada, b_ada):
    depth, d, n = w_ada.shape
    tn = n // 4
    return pl.pallas_call(
        _mod_kernel,
        grid=(depth, n // tn),
        in_specs=[pl.BlockSpec((N_COND_PAD, d), lambda l, j: (0, 0)),
                  pl.BlockSpec((1, d, tn), lambda l, j: (l, 0, j)),
                  pl.BlockSpec((1, 1, tn), lambda l, j: (l, 0, j))],
        out_specs=pl.BlockSpec((1, N_COND_PAD, tn), lambda l, j: (l, 0, j)),
        out_shape=jax.ShapeDtypeStruct((depth, N_COND_PAD, n), F32),
        compiler_params=_params(2),
        name="adaln_modulation",
    )(cond, w_ada, b_ada.reshape(depth, 1, n))


class _Rows:
    def __init__(self, n_prompt_seq, prompt_len, n_sample_seq, sample_len):
        self.prompt_len = prompt_len
        self.sample_len = sample_len
        self.n_prompt_seq = n_prompt_seq
        self.n_sample_seq = n_sample_seq
        self.n_prompt = n_prompt_seq * prompt_len
        self.n_sample = n_sample_seq * sample_len
        self.total = self.n_prompt + self.n_sample
        assert prompt_len % ROW_TILE == 0 and sample_len % ROW_TILE == 0
        assert self.n_prompt % sample_len == 0 or self.n_sample == 0

    def cond_of_tile(self, i, tile):
        n_p = self.n_prompt // tile
        per_seq = self.sample_len // tile
        return jnp.where(i < n_p, 0, 1 + (i - n_p) // per_seq)


def _proj_even_kernel(x_ref, xp_ref, xn_ref, mod_ref, gain_ref, w_ref, wg_ref, cw_ref, gb_ref,
                      qkva_ref, qkvb_ref, oz_ref, g_ref, *, tm, n_prompt, prompt_len, sample_len):
    i = pl.program_id(0)
    shift = mod_ref[0, 0:1, :]
    scale = mod_ref[0, 1:2, :]
    gain = gain_ref[...]

    def modulated(x):
        return _rms(x) * gain * (1.0 + scale) + shift

    h = modulated(x_ref[...])
    hb = h.astype(BF16)
    halo = jnp.concatenate([xp_ref[0], xn_ref[0]], axis=0)
    hh = modulated(halo).astype(BF16)

    w_qk = w_ref[:, 0:1024]
    qk = _dot(hb, w_qk)
    qk_halo = _dot(hh, w_qk)
    prev_row = qk_halo[7:8, :]
    next_row = qk_halo[8:9, :]
    local = lax.broadcasted_iota(jnp.int32, (tm, 1), 0)
    seq_len = jnp.where(i * tm < n_prompt, prompt_len, sample_len)
    pos = (i * tm + local) & (seq_len - 1)
    prev = pltpu.roll(qk, 1, axis=0)
    prev = jnp.where(local == 0, prev_row, prev)
    prev = jnp.where(pos == 0, 0.0, prev)
    nxt = pltpu.roll(qk, tm - 1, axis=0)
    nxt = jnp.where(local == tm - 1, next_row, nxt)
    nxt = jnp.where(pos == seq_len - 1, 0.0, nxt)
    cw = cw_ref[...]
    act = _silu(cw[0:1, :] * prev + cw[1:2, :] * qk + cw[2:3, :] * nxt)
    k_scale = HEAD_W ** -0.5
    qkva_ref[:, 0:512] = act[:, 0:512].astype(BF16)
    qkva_ref[:, 512:1024] = (act[:, 512:1024] * k_scale).astype(BF16)
    qkva_ref[:, 1024:1536] = _dot(hb, w_ref[:, 1024:1536]).astype(BF16)

    qkvb_ref[:, 0:512] = _dot(hb, w_ref[:, 1536:2048]).astype(BF16)
    qkvb_ref[:, 512:1024] = (_dot(hb, w_ref[:, 2048:2560]) * k_scale).astype(BF16)
    qkvb_ref[:, 1024:1536] = _dot(hb, w_ref[:, 2560:3072]).astype(BF16)
    oz_ref[:, 0:512] = _dot(hb, w_ref[:, 3072:3584]).astype(BF16)
    oz_ref[:, 512:1024] = _dot(hb, w_ref[:, 3584:4096]).astype(BF16)

    h_lo = (h - hb.astype(F32)).astype(BF16)
    g_hl = _dot(hb, wg_ref[...])
    gates = (g_hl[:, 0:256] + g_hl[:, 256:512] + _dot(h_lo, wg_ref[:, 0:256])
             + gb_ref[...])
    g_ref[:, 0:128] = gates[:, 0:128]
    g_ref[:, 128:256] = _log_sigmoid(gates[:, 128:256])


def _proj_even(x, mods, gain, w_main, w_gates, conv_w, gate_bias, rows):
    n, d = x.shape
    tm = ROW_TILE
    x8 = x.reshape(n // 8, 8, d)
    nb8 = n // 8
    kern = functools.partial(_proj_even_kernel, tm=tm, n_prompt=rows.n_prompt,
                             prompt_len=rows.prompt_len, sample_len=rows.sample_len)
    return pl.pallas_call(
        kern,
        grid=(n // tm,),
        in_specs=[pl.BlockSpec((tm, d), lambda i: (i, 0)),
                  pl.BlockSpec((1, 8, d), lambda i: (jnp.maximum(i * (tm // 8) - 1, 0), 0, 0)),
                  pl.BlockSpec((1, 8, d), lambda i: (jnp.minimum((i + 1) * (tm // 8), nb8 - 1), 0, 0)),
                  pl.BlockSpec((1, 6, d), lambda i: (rows.cond_of_tile(i, tm), 0, 0)),
                  pl.BlockSpec((1, d), lambda i: (0, 0)),
                  pl.BlockSpec(w_main.shape, lambda i: (0, 0)),
                  pl.BlockSpec(w_gates.shape, lambda i: (0, 0)),
                  pl.BlockSpec(conv_w.shape, lambda i: (0, 0)),
                  pl.BlockSpec(gate_bias.shape, lambda i: (0, 0))],
        out_specs=[pl.BlockSpec((tm, 1536), lambda i: (i, 0)),
                   pl.BlockSpec((tm, 1536), lambda i: (i, 0)),
                   pl.BlockSpec((tm, 1024), lambda i: (i, 0)),
                   pl.BlockSpec((tm, 256), lambda i: (i, 0))],
        out_shape=[jax.ShapeDtypeStruct((n, 1536), BF16),
                   jax.ShapeDtypeStruct((n, 1536), BF16),
                   jax.ShapeDtypeStruct((n, 1024), BF16),
                   jax.ShapeDtypeStruct((n, 256), F32)],
        compiler_params=_params(1),
        name="proj_even",
    )(x, x8, x8, mods, gain, w_main, w_gates, conv_w, gate_bias)


def _scan_kernel(fb_ref, bb_ref, first_ref, last_ref, sin_ref, sout_ref, isp_ref,
                 qaf_ref, qab_ref, qbf_ref, qbb_ref, gf_ref, gb_ref,
                 c0_ref, n0_ref, m0_ref, s0_ref, rl_ref,
                 hf_ref, hb_ref, cn_ref, nn_ref, mn_ref, sn_ref,
                 cext_sc, s_sc, m_sc, intra_sc, inter_sc, toend_sc, cdec_sc, *, L):
    del fb_ref, bb_ref, sin_ref, sout_ref
    step = pl.program_id(0)
    n_units = N_DIR * H_A
    lane = lax.broadcasted_iota(jnp.int32, (HEAD_W, HEAD_W), 1)
    t_idx = lax.broadcasted_iota(jnp.int32, (L, L), 0)
    s_idx = lax.broadcasted_iota(jnp.int32, (L, L), 1)
    masks = (s_idx <= t_idx, s_idx >= t_idx)

    @pl.when(step == 0)
    def _():
        log_gamma = _log_sigmoid(rl_ref[...])
        rel = jnp.abs(t_idx - s_idx).astype(F32)
        pos_col = lax.broadcasted_iota(jnp.int32, (L, HEAD_W), 0).astype(F32)
        for u in range(n_units):
            lg = log_gamma[u:u + 1, 0:1]
            intra_sc[u] = jnp.where(masks[u // H_B], jnp.exp(lg * rel), 0.0)
            pos = pos_col if u < H_B else (L - 1.0) - pos_col
            inter_sc[u] = jnp.exp(lg * (pos + 1.0))
        unit = lax.broadcasted_iota(jnp.int32, (n_units, L), 0)
        pos_row = lax.broadcasted_iota(jnp.int32, (n_units, L), 1).astype(F32)
        pos_row = jnp.where(unit < H_B, pos_row, (L - 1.0) - pos_row)
        toend_sc[...] = jnp.exp(log_gamma[:, 0:1] * ((L - 1.0) - pos_row))
        cdec_sc[...] = jnp.exp(log_gamma * float(L))

    @pl.when(jnp.logical_and(first_ref[step] == 1, isp_ref[step] == 1))
    def _():
        cext_sc[...] = jnp.zeros_like(cext_sc)
        s_sc[...] = jnp.zeros_like(s_sc)
        m_sc[...] = jnp.zeros_like(m_sc)

    @pl.when(jnp.logical_and(first_ref[step] == 1, isp_ref[step] == 0))
    def _():
        n0 = n0_ref[0]
        n0_t = jnp.concatenate([n0, jnp.zeros((HEAD_W - n_units, HEAD_W), F32)], axis=0).T
        for u in range(n_units):
            cext_sc[u, :, 0:HEAD_W] = c0_ref[0, u]
            cext_sc[u, :, HEAD_W:2 * HEAD_W] = jnp.broadcast_to(n0_t[:, u:u + 1], (HEAD_W, HEAD_W))
            s_sc[u] = s0_ref[0, u]
        m_sc[...] = m0_ref[0]

    ones_ext = jnp.ones((L, HEAD_W), BF16)
    row_l = lax.broadcasted_iota(jnp.int32, (L, HEAD_W), 0)
    tri = tuple(jnp.where(m, 1.0, 0.0) for m in masks)
    qa = (qaf_ref, qab_ref)
    qb = (qbf_ref, qbb_ref)
    g = (gf_ref, gb_ref)
    out = (hf_ref, hb_ref)
    m_all = m_sc[...]
    toend = toend_sc[...]
    cdec = cdec_sc[...]
    m_rows = []

    def wide(x):
        return x if L == HEAD_W else jnp.concatenate([x] * (L // HEAD_W), axis=1)

    for d in range(N_DIR):
        ig_all = g[d][:, 0:HEAD_W]
        lf_all = g[d][:, HEAD_W:2 * HEAD_W]
        bt_all = _dot(tri[d], lf_all, precision=_HI)
        a_all = ig_all - bt_all
        a_rows = a_all.T[0:n_units, :]
        cm_all = a_all
        shift = 1
        while shift < L:
            if d == 0:
                moved = jnp.where(row_l >= shift, pltpu.roll(cm_all, shift, axis=0), -jnp.inf)
            else:
                moved = jnp.where(row_l < L - shift, pltpu.roll(cm_all, L - shift, axis=0), -jnp.inf)
            cm_all = jnp.maximum(cm_all, moved)
            shift *= 2
        end = L - 1 if d == 0 else 0
        for h in range(H_A):
            u = d * H_A + h
            cs = slice(h * HEAD_W, (h + 1) * HEAD_W)
            m_prev = m_all[u:u + 1, :]
            g_rep = jnp.maximum(m_prev, jnp.broadcast_to(cm_all[:, u:u + 1], (L, HEAD_W)))
            bt_rep = jnp.broadcast_to(bt_all[:, u:u + 1], (L, HEAD_W))
            a_row = a_rows[u:u + 1, :]
            dmat = jnp.exp(jnp.where(masks[d], a_row - wide(g_rep), -jnp.inf))
            inter_w = jnp.exp(m_prev - g_rep)
            q = qa[d][:, cs]
            k = qa[d][:, 512 + h * HEAD_W:512 + (h + 1) * HEAD_W]
            v = qa[d][:, 1024 + h * HEAD_W:1024 + (h + 1) * HEAD_W]
            v_ext = jnp.concatenate([v, ones_ext], axis=1)
            s = (_dot_nt(q, k) * dmat).astype(BF16)
            lhs = jnp.concatenate([s, (q.astype(F32) * inter_w).astype(BF16)], axis=1)
            rhs = jnp.concatenate([v_ext, cext_sc[u].astype(BF16)], axis=0)
            num = _dot(lhs, rhs)
            inv = 1.0 / jnp.maximum(jnp.abs(num[:, HEAD_W:2 * HEAD_W]), jnp.exp(-(bt_rep + g_rep)))
            out[d][:, cs] = (num[:, 0:HEAD_W] * inv).astype(out[d].dtype)
            m_prev1 = m_prev[:, 0:1]
            g_end = jnp.maximum(m_prev1, cm_all[end:end + 1, u:u + 1])
            wk_row = jnp.exp(a_row - g_end)
            kw = (k.astype(F32).T * wk_row).astype(BF16)
            cext_sc[u] = jnp.exp(m_prev1 - g_end) * cext_sc[u] + _dot(kw, v_ext)
            m_rows.append(jnp.broadcast_to(bt_all[end:end + 1, u:u + 1] + g_end, (1, HEAD_W)))
            qr = qb[d][:, cs]
            kr = qb[d][:, 512 + h * HEAD_W:512 + (h + 1) * HEAD_W]
            vr = qb[d][:, 1024 + h * HEAD_W:1024 + (h + 1) * HEAD_W]
            sr = (_dot_nt(qr, kr) * intra_sc[u]).astype(BF16)
            lhs = jnp.concatenate([sr, (qr.astype(F32) * inter_sc[u]).astype(BF16)], axis=1)
            rhs = jnp.concatenate([vr, s_sc[u].astype(BF16)], axis=0)
            out[d][:, 512 + h * HEAD_W:512 + (h + 1) * HEAD_W] = _dot(lhs, rhs).astype(out[d].dtype)
            krw = (kr.astype(F32).T * toend[u:u + 1, :]).astype(BF16)
            s_sc[u] = cdec[u:u + 1, :] * s_sc[u] + _dot(krw, vr)
    m_sc[...] = jnp.concatenate(m_rows, axis=0)

    @pl.when(last_ref[step] == 1)
    def _():
        n_cols = jnp.zeros((HEAD_W, HEAD_W), F32)
        for u in range(n_units):
            cn_ref[0, u] = cext_sc[u, :, 0:HEAD_W]
            sn_ref[0, u] = s_sc[u]
            n_cols = jnp.where(lane == u, cext_sc[u, :, HEAD_W:HEAD_W + 1], n_cols)
        nn_ref[0] = n_cols.T[0:n_units, :]
        mn_ref[0] = m_sc[...]


def _scan_tables(rows, L):
    fb, bb, first, last, sin, sout, isp = [], [], [], [], [], [], []
    base = 0
    for kind, n_seq, seq_len in (("p", rows.n_prompt_seq, rows.prompt_len),
                                 ("s", rows.n_sample_seq, rows.sample_len)):
        nc = seq_len // L
        for b in range(n_seq):
            for c in range(nc):
                fb.append(base + b * nc + c)
                bb.append(base + b * nc + nc - 1 - c)
                first.append(int(c == 0))
                last.append(int(c == nc - 1 and kind == "p"))
                sin.append(b if kind == "s" else 0)
                sout.append(b if kind == "p" else rows.n_prompt_seq - 1)
                isp.append(int(kind == "p"))
        base += n_seq * nc
    return [jnp.asarray(np.asarray(t, np.int32)) for t in (fb, bb, first, last, sin, sout, isp)]


def _scan_even(qkva, qkvb, gates, c0, n0, m0, s0, ret_logit, rows):
    L = SCAN_CHUNK
    n = qkva.shape[0]
    tables = _scan_tables(rows, L)
    n_steps = int(tables[0].shape[0])
    nu = N_DIR * H_A
    fwd = lambda w: pl.BlockSpec((L, w), lambda s, fb, bb, fi, la, si, so, ip: (fb[s], 0))
    bwd = lambda w: pl.BlockSpec((L, w), lambda s, fb, bb, fi, la, si, so, ip: (bb[s], 0))
    st_in4 = pl.BlockSpec((1, nu, HEAD_W, HEAD_W), lambda s, fb, bb, fi, la, si, so, ip: (si[s], 0, 0, 0))
    st_in3 = pl.BlockSpec((1, nu, HEAD_W), lambda s, fb, bb, fi, la, si, so, ip: (si[s], 0, 0))
    st_out4 = pl.BlockSpec((1, nu, HEAD_W, HEAD_W), lambda s, fb, bb, fi, la, si, so, ip: (so[s], 0, 0, 0))
    st_out3 = pl.BlockSpec((1, nu, HEAD_W), lambda s, fb, bb, fi, la, si, so, ip: (so[s], 0, 0))
    nps = rows.n_prompt_seq
    return pl.pallas_call(
        functools.partial(_scan_kernel, L=L),
        grid_spec=pltpu.PrefetchScalarGridSpec(
            num_scalar_prefetch=7,
            grid=(n_steps,),
            in_specs=[fwd(1536), bwd(1536), fwd(1536), bwd(1536), fwd(256), bwd(256),
                      st_in4, st_in3, st_in3, st_in4,
                      pl.BlockSpec((nu, HEAD_W), lambda s, *_: (0, 0))],
            out_specs=[fwd(1024), bwd(1024), st_out4, st_out3, st_out3, st_out4],
            scratch_shapes=[pltpu.VMEM((nu, HEAD_W, 2 * HEAD_W), F32),
                            pltpu.VMEM((nu, HEAD_W, HEAD_W), F32),
                            pltpu.VMEM((nu, HEAD_W), F32),
                            pltpu.VMEM((nu, L, L), F32),
                            pltpu.VMEM((nu, L, HEAD_W), F32),
                            pltpu.VMEM((nu, L), F32),
                            pltpu.VMEM((nu, HEAD_W), F32)]),
        out_shape=[jax.ShapeDtypeStruct((n, 1024), BF16),
                   jax.ShapeDtypeStruct((n, 1024), BF16),
                   jax.ShapeDtypeStruct((nps, nu, HEAD_W, HEAD_W), F32),
                   jax.ShapeDtypeStruct((nps, nu, HEAD_W), F32),
                   jax.ShapeDtypeStruct((nps, nu, HEAD_W), F32),
                   jax.ShapeDtypeStruct((nps, nu, HEAD_W, HEAD_W), F32)],
        compiler_params=_params(1),
        name="scan_even",
    )(*tables, qkva, qkva, qkvb, qkvb, gates, gates, c0, n0, m0, s0, ret_logit)


def _proj_odd_kernel(x_ref, mod_ref, gain_ref, w_ref, e_ref, qg_ref, kg_ref, cos_ref, sin_ref,
                     q_ref, k_ref, v_ref, *, rope):
    shift = mod_ref[0, 0:1, :]
    scale = mod_ref[0, 1:2, :]
    hb = (_rms(x_ref[...]) * gain_ref[...] * (1.0 + scale) + shift).astype(BF16)

    def qk_norm(raw, g):
        ss = _dot((raw * raw).astype(BF16), e_ref[...])
        return raw * lax.rsqrt(ss * (1.0 / DH_C) + EPS) * g

    def rotate(y):
        if not rope:
            return y
        lane = lax.broadcasted_iota(jnp.int32, y.shape, 1)
        first_half = (lane & 31) < 16
        partner = jnp.where(first_half, pltpu.roll(y, y.shape[1] - 16, axis=1), pltpu.roll(y, 16, axis=1))
        cos = jnp.concatenate([cos_ref[...]] * (y.shape[1] // HEAD_W), axis=1)
        sin = jnp.concatenate([sin_ref[...]] * (y.shape[1] // HEAD_W), axis=1)
        return y * cos + partner * sin

    q = rotate(qk_norm(_dot(hb, w_ref[:, 0:1024]), qg_ref[...]))
    q_ref[...] = (q * (DH_C ** -0.5 * math.log2(math.e))).astype(q_ref.dtype)
    k = rotate(qk_norm(_dot(hb, w_ref[:, 1024:2048]), kg_ref[...]))
    k_ref[...] = k.astype(k_ref.dtype)
    v_ref[...] = _dot(hb, w_ref[:, 2048:3072]).astype(v_ref.dtype)


def _proj_odd(x, mods, gain, w_qkv, e64, q_gain, k_gain, cos, sin, rows, *, sample):
    d = x.shape[1]
    tm = ROW_TILE
    if sample:
        n, base, kv_dtype = rows.n_sample, rows.n_prompt // tm, BF16
        per_seq = rows.sample_len // tm
        table = lambda i: (i % per_seq, 0)
    else:
        n, base, kv_dtype = rows.n_prompt, 0, F32
        table = lambda i: (0, 0)
    return pl.pallas_call(
        functools.partial(_proj_odd_kernel, rope=sample),
        grid=(n // tm,),
        in_specs=[pl.BlockSpec((tm, d), lambda i: (base + i, 0)),
                  pl.BlockSpec((1, 6, d), lambda i: (rows.cond_of_tile(base + i, tm), 0, 0)),
                  pl.BlockSpec((1, d), lambda i: (0, 0)),
                  pl.BlockSpec(w_qkv.shape, lambda i: (0, 0)),
                  pl.BlockSpec(e64.shape, lambda i: (0, 0)),
                  pl.BlockSpec((1, d), lambda i: (0, 0)),
                  pl.BlockSpec((1, d), lambda i: (0, 0)),
                  pl.BlockSpec((tm, HEAD_W), table),
                  pl.BlockSpec((tm, HEAD_W), table)],
        out_specs=[pl.BlockSpec((tm, d), lambda i: (i, 0))] * 3,
        out_shape=[jax.ShapeDtypeStruct((n, d), BF16),
                   jax.ShapeDtypeStruct((n, d), kv_dtype),
                   jax.ShapeDtypeStruct((n, d), kv_dtype)],
        compiler_params=_params(1),
        name="proj_odd_sample" if sample else "proj_odd_prompt",
    )(x, mods, gain, w_qkv, e64, q_gain, k_gain, cos, sin)


def _rope_tables(sample_len):
    t = np.arange(sample_len)
    nf = DH_C // 4
    inv = ROPE_BASE ** (-np.arange(nf, dtype=np.float32) / nf)
    row = (t // GRID_W).astype(np.float32)[:, None] * inv[None, :]
    col = (t % GRID_W).astype(np.float32)[:, None] * inv[None, :]
    cos64 = np.concatenate([np.cos(row), np.cos(row), np.cos(col), np.cos(col)], axis=1)
    sin64 = np.concatenate([-np.sin(row), np.sin(row), -np.sin(col), np.sin(col)], axis=1)
    cos = np.concatenate([cos64, cos64], axis=1).astype(np.float32)
    sin = np.concatenate([sin64, sin64], axis=1).astype(np.float32)
    return jnp.asarray(cos), jnp.asarray(sin)


def _attn_kernel(*refs, tq, tk, n_main, n_cache, lam_init):
    if n_cache:
        q_ref, k_ref, v_ref, ck_ref, cv_ref, lq_ref, og_ref, o_ref, vt_sc = refs
    else:
        q_ref, k_ref, v_ref, lq_ref, og_ref, o_ref, vt_sc = refs
    fill_w = 256

    @pl.when(pl.program_id(2) == 0)
    def _():
        ones_rows = jnp.where(lax.broadcasted_iota(jnp.int32, (8, fill_w), 0) == 0, 1.0, 0.0).astype(BF16)
        for j in range(n_main // fill_w):
            vt = v_ref[0, j * fill_w:(j + 1) * fill_w, :].astype(F32).T.astype(BF16)
            vt_sc[:, j * fill_w:(j + 1) * fill_w] = jnp.concatenate([vt, ones_rows], axis=0)
        if n_cache:
            vt = cv_ref[0].astype(F32).T.astype(BF16)
            vt_sc[:, n_main:n_main + n_cache] = jnp.concatenate([vt, ones_rows], axis=0)

    qt = q_ref[0].astype(F32).T
    row = lax.broadcasted_iota(jnp.int32, qt.shape, 0)
    qst = jnp.concatenate([jnp.where(row < DH_C, qt, 0.0), jnp.where(row >= DH_C, qt, 0.0)],
                          axis=1).astype(BF16)

    def scores(kt):
        return _dot(kt.astype(BF16), qst)

    def accumulate(carry, s, vt):
        m, acc = carry
        m_new = jnp.maximum(m, jnp.max(s, axis=0, keepdims=True))
        p = jnp.exp2(s - m_new).astype(BF16)
        acc = jnp.exp2(m - m_new) * acc + _dot(vt, p)
        return m_new, acc

    tiles = [(k_ref, j * tk, tk, j * tk) for j in range(n_main // tk)]
    if n_cache:
        tiles = [(ck_ref, 0, n_cache, n_main)] + tiles
    carry = (jnp.full((1, 2 * tq), -jnp.inf, F32), jnp.zeros((HEAD_W + 8, 2 * tq), F32))
    ref0, r0, n0, _ = tiles[0]
    s_next = scores(ref0[0, r0:r0 + n0, :])
    for j, (_, _, n_keys, c0) in enumerate(tiles):
        s_cur = s_next
        if j + 1 < len(tiles):
            ref1, r1, n1, _ = tiles[j + 1]
            s_next = scores(ref1[0, r1:r1 + n1, :])
        carry = accumulate(carry, s_cur, vt_sc[:, c0:c0 + n_keys])
    _, acc = carry
    o_t = acc[0:HEAD_W, :] / acc[HEAD_W:HEAD_W + 1, :]
    lq = lq_ref[...]
    lam = (jnp.exp(jnp.sum(lq[0:1, :] * lq[1:2, :], axis=1, keepdims=True))
           - jnp.exp(jnp.sum(lq[2:3, :] * lq[3:4, :], axis=1, keepdims=True)) + lam_init)
    o = (o_t[:, 0:tq] - lam * o_t[:, tq:2 * tq]).T
    o_ref[0] = (_rms(o) * og_ref[...] * (1.0 - lam_init)).astype(o_ref.dtype)


def _attention(q, k, v, cache_k, cache_v, lam_qk, out_gain, lam_init):
    b, t, d = q.shape
    tq = min(ATTN_TQ, t)
    tk = min(ATTN_TK, t)
    n_cache = 0 if cache_k is None else cache_k.shape[1]
    assert t % tk == 0 and t % tq == 0 and t % 256 == 0 and n_cache in (0, 256)
    seq = lambda n: pl.BlockSpec((1, n, HEAD_W), lambda bi, h, qi: (bi, 0, h))
    in_specs = [pl.BlockSpec((1, tq, HEAD_W), lambda bi, h, qi: (bi, qi, h)), seq(t), seq(t)]
    args = [q, k, v]
    if n_cache:
        in_specs += [seq(n_cache), seq(n_cache)]
        args += [cache_k, cache_v]
    in_specs += [pl.BlockSpec(lam_qk.shape, lambda bi, h, qi: (0, 0)),
                 pl.BlockSpec(out_gain.shape, lambda bi, h, qi: (0, 0))]
    args += [lam_qk, out_gain]
    return pl.pallas_call(
        functools.partial(_attn_kernel, tq=tq, tk=tk, n_main=t, n_cache=n_cache, lam_init=lam_init),
        grid=(b, H_C, t // tq),
        in_specs=in_specs,
        out_specs=pl.BlockSpec((1, tq, HEAD_W), lambda bi, h, qi: (bi, qi, h)),
        out_shape=jax.ShapeDtypeStruct((b, t, d), BF16),
        scratch_shapes=[pltpu.VMEM((HEAD_W + 8, t + n_cache), BF16)],
        compiler_params=_params(3),
        name="diff_attention",
    )(*args)


def _out_kernel(*refs, tm, even, n_prompt_tiles):
    if even:
        (hf_ref, hb_ref, oz_ref, ga_ref, gb_ref, x_ref, mod_ref, w_ref, fg_ref, wr_ref, rb_ref,
         x1_ref, h2_ref, e_ref, gw_ref, rk_ref, c1_ref, c2_ref, c1_sc, c2_sc) = refs
    else:
        (op_ref, os_ref, x_ref, mod_ref, w_ref, fg_ref, wr_ref, rb_ref,
         x1_ref, h2_ref, e_ref, gw_ref, rk_ref, c1_ref, c2_ref, c1_sc, c2_sc) = refs
    i = pl.program_id(0)

    @pl.when(i == 0)
    def _():
        c1_sc[...] = jnp.zeros_like(c1_sc)
        c2_sc[...] = jnp.zeros_like(c2_sc)

    if even:
        hs = hf_ref[...].astype(F32) + hb_ref[...].astype(F32)
        oz = oz_ref[...].astype(F32)
        parts = []
        for h in range(H_A + H_B):
            cs = slice(h * HEAD_W, (h + 1) * HEAD_W)
            gain = ga_ref[:, cs] if h < H_A else gb_ref[:, (h - H_A) * HEAD_W:(h - H_A + 1) * HEAD_W]
            act = _sigmoid(oz[:, cs]) if h < H_A else _silu(oz[:, cs])
            parts.append((_rms(hs[:, cs]) * gain * act).astype(BF16))
        y_in = jnp.concatenate(parts, axis=1)
    else:
        y_in = jnp.where(i < n_prompt_tiles, op_ref[...], os_ref[...])

    x1 = x_ref[...] + mod_ref[0, 2:3, :] * _dot(y_in, w_ref[...])
    x1_ref[...] = x1
    h2 = _rms(x1) * fg_ref[...] * (1.0 + mod_ref[0, 4:5, :]) + mod_ref[0, 3:4, :]
    h2b = h2.astype(BF16)
    h2_ref[...] = h2b

    score = _sigmoid(_dot(h2b, wr_ref[...]))
    st = score.T[0:N_EXPERTS, :]
    sel = st + rb_ref[:, 0:1]
    srow = [st[e:e + 1, :] for e in range(N_EXPERTS)]
    brow = [sel[e:e + 1, :] for e in range(N_EXPERTS)]
    epg = EXPERTS_PER_GROUP
    gscore = []
    for g in range(N_GROUPS):
        a = brow[g * epg:(g + 1) * epg]
        best = None
        for p in range(epg):
            for r in range(p + 1, epg):
                pair = a[p] + a[r]
                best = pair if best is None else jnp.maximum(best, pair)
        gscore.append(best)
    gbest, gidx = gscore[0], jnp.zeros((1, tm), jnp.int32)
    for g in range(1, N_GROUPS):
        better = gscore[g] > gbest
        gbest = jnp.where(better, gscore[g], gbest)
        gidx = jnp.where(better, g, gidx)
    vals, sig = [], []
    for p in range(epg):
        vp, sp = brow[p], srow[p]
        for g in range(1, N_GROUPS):
            vp = jnp.where(gidx == g, brow[g * epg + p], vp)
            sp = jnp.where(gidx == g, srow[g * epg + p], sp)
        vals.append(vp)
        sig.append(sp)
    v1, i1, w1 = vals[0], jnp.zeros((1, tm), jnp.int32), sig[0]
    for p in range(1, epg):
        better = vals[p] > v1
        v1 = jnp.where(better, vals[p], v1)
        i1 = jnp.where(better, p, i1)
        w1 = jnp.where(better, sig[p], w1)
    v2 = jnp.full((1, tm), -jnp.inf, F32)
    i2 = jnp.zeros((1, tm), jnp.int32)
    w2 = jnp.zeros((1, tm), F32)
    for p in range(epg):
        better = jnp.logical_and(i1 != p, vals[p] > v2)
        v2 = jnp.where(better, vals[p], v2)
        i2 = jnp.where(better, p, i2)
        w2 = jnp.where(better, sig[p], w2)
    e1 = gidx * epg + i1
    e2 = gidx * epg + i2
    wsum = w1 + w2

    erow = lax.broadcasted_iota(jnp.int32, (N_EXPERTS, tm), 0)
    earlier = jnp.where(lax.broadcasted_iota(jnp.int32, (tm, tm), 0)
                        < lax.broadcasted_iota(jnp.int32, (tm, tm), 1), 1.0, 0.0).astype(BF16)
    ranks = []
    for e_k, c_sc in ((e1, c1_sc), (e2, c2_sc)):
        onehot = jnp.where(erow == e_k, 1.0, 0.0)
        before = _dot(onehot.astype(BF16), earlier) + c_sc[:, 0:1]
        ranks.append(jnp.sum(onehot * before, axis=0, keepdims=True))
        c_sc[...] = c_sc[...] + jnp.sum(onehot, axis=1, keepdims=True)
    zi = jnp.zeros((6, tm), jnp.int32)
    e_ref[...] = jnp.concatenate([e1, e2, zi], axis=0)
    gw_ref[...] = jnp.concatenate([w1 / wsum, w2 / wsum, jnp.zeros((6, tm), F32)], axis=0)
    rk_ref[...] = jnp.concatenate([ranks[0].astype(jnp.int32), ranks[1].astype(jnp.int32), zi], axis=0)
    c1_ref[...] = c1_sc[...]
    c2_ref[...] = c2_sc[...]


def _out_and_route(mix_in, x, mods, w_out, ffn_gain, w_router, router_bias, rows, *, even):
    n, d = x.shape
    tm = ROW_TILE
    row = lambda w: pl.BlockSpec((tm, w), lambda i: (i, 0))
    full = lambda a: pl.BlockSpec(a.shape, lambda i: (0,) * a.ndim)
    n_pt = rows.n_prompt // tm
    if even:
        hf, hb, oz, gain_a, gain_b = mix_in
        in_specs = [row(1024), row(1024), row(1024), full(gain_a), full(gain_b)]
        args = [hf, hb, oz, gain_a, gain_b]
    else:
        o_p, o_s = mix_in
        in_specs = [pl.BlockSpec((tm, d), lambda i: (jnp.minimum(i, n_pt - 1), 0)),
                    pl.BlockSpec((tm, d), lambda i: (jnp.maximum(i - n_pt, 0), 0))]
        args = [o_p, o_s]
    in_specs += [row(d), pl.BlockSpec((1, 6, d), lambda i: (rows.cond_of_tile(i, tm), 0, 0)),
                 full(w_out), full(ffn_gain), full(w_router), full(router_bias)]
    args += [x, mods, w_out, ffn_gain, w_router, router_bias]
    col = lambda: pl.BlockSpec((8, tm), lambda i: (0, i))
    cnt = lambda: pl.BlockSpec((N_EXPERTS, HEAD_W), lambda i: (0, 0))
    return pl.pallas_call(
        functools.partial(_out_kernel, tm=tm, even=even, n_prompt_tiles=n_pt),
        grid=(n // tm,),
        in_specs=in_specs,
        out_specs=[row(d), row(d), col(), col(), col(), cnt(), cnt()],
        out_shape=[jax.ShapeDtypeStruct((n, d), F32),
                   jax.ShapeDtypeStruct((n, d), BF16),
                   jax.ShapeDtypeStruct((8, n), jnp.int32),
                   jax.ShapeDtypeStruct((8, n), F32),
                   jax.ShapeDtypeStruct((8, n), jnp.int32),
                   jax.ShapeDtypeStruct((N_EXPERTS, HEAD_W), F32),
                   jax.ShapeDtypeStruct((N_EXPERTS, HEAD_W), F32)],
        scratch_shapes=[pltpu.VMEM((N_EXPERTS, HEAD_W), F32), pltpu.VMEM((N_EXPERTS, HEAD_W), F32)],
        compiler_params=_params(1),
        name="out_even" if even else "out_odd",
    )(*args)


def _moe_kernel(be_ref, nu_ref, x_ref, wg_ref, wu_ref, wd_ref, y_ref, wg_sc, wu_sc, wd_sc):
    i = pl.program_id(0)
    changed = jnp.logical_or(i == 0, be_ref[i] != be_ref[jnp.maximum(i - 1, 0)])

    @pl.when(jnp.logical_and(changed, i < nu_ref[0]))
    def _():
        wg_sc[...] = wg_ref[0, 0].astype(BF16)
        wu_sc[...] = wu_ref[0, 0].astype(BF16)
        wd_sc[...] = wd_ref[0, 0].astype(BF16)

    @pl.when(i < nu_ref[0])
    def _():
        x = x_ref[...]
        a = (_silu(_dot(x, wg_sc[...])) * _dot(x, wu_sc[...])).astype(BF16)
        y_ref[...] = _dot(a, wd_sc[...]).astype(y_ref.dtype)

    @pl.when(i >= nu_ref[0])
    def _():
        y_ref[...] = jnp.zeros_like(y_ref)


def _moe_experts(xs, block_e, n_used, w_gate, w_up, w_down, layer):
    p, d = xs.shape
    bm = MOE_BLOCK
    wspec = pl.BlockSpec((1, 1, d, d), lambda i, be, nu: (layer, be[i], 0, 0))
    return pl.pallas_call(
        _moe_kernel,
        grid_spec=pltpu.PrefetchScalarGridSpec(
            num_scalar_prefetch=2,
            grid=(p // bm,),
            in_specs=[pl.BlockSpec((bm, d), lambda i, be, nu: (i, 0)), wspec, wspec, wspec],
            out_specs=pl.BlockSpec((bm, d), lambda i, be, nu: (i, 0)),
            scratch_shapes=[pltpu.VMEM((d, d), BF16)] * 3),
        out_shape=jax.ShapeDtypeStruct((p, d), BF16),
        compiler_params=_params(1),
        name="moe_experts",
    )(block_e, n_used, xs, w_gate, w_up, w_down)


def _combine_kernel(x_ref, y_ref, gw_ref, mod_ref, o_ref):
    w = gw_ref[...]
    y = w[:, 0:1] * y_ref[0].astype(F32) + w[:, 1:2] * y_ref[1].astype(F32)
    o_ref[...] = x_ref[...] + mod_ref[0, 5:6, :] * y


def _combine(x1, y_pairs, gw_cols, mods, rows):
    n, d = x1.shape
    tm = ROW_TILE
    return pl.pallas_call(
        _combine_kernel,
        grid=(n // tm,),
        in_specs=[pl.BlockSpec((tm, d), lambda i: (i, 0)),
                  pl.BlockSpec((2, tm, d), lambda i: (0, i, 0)),
                  pl.BlockSpec((tm, 2), lambda i: (i, 0)),
                  pl.BlockSpec((1, 6, d), lambda i: (rows.cond_of_tile(i, tm), 0, 0))],
        out_specs=pl.BlockSpec((tm, d), lambda i: (i, 0)),
        out_shape=jax.ShapeDtypeStruct((n, d), F32),
        compiler_params=_params(1),
        name="moe_combine",
    )(x1, y_pairs, gw_cols, mods)


def _moe_layer(x1, h2, eidx, gw, rank, cnt1, cnt2, mods, w_gate, w_up, w_down, layer, rows):
    n = x1.shape[0]
    bm = MOE_BLOCK
    n_blocks = (2 * n + N_EXPERTS * (bm - 1) + bm - 1) // bm
    c1 = cnt1[:, 0].astype(jnp.int32)
    c2 = cnt2[:, 0].astype(jnp.int32)
    padded = (c1 + c2 + bm - 1) // bm * bm
    pad_ends = jnp.cumsum(padded)
    pad_starts = pad_ends - padded
    e1, e2 = eidx[0], eidx[1]
    slot1 = pad_starts[e1] + rank[0]
    slot2 = pad_starts[e2] + c1[e2] + rank[1]
    slots = jnp.stack([slot1, slot2], axis=0)
    tok = jnp.arange(n, dtype=jnp.int32)
    slot_token = jnp.zeros((n_blocks * bm,), jnp.int32).at[slots.reshape(-1)].set(jnp.concatenate([tok, tok]))
    block_start = jnp.arange(n_blocks, dtype=jnp.int32) * bm
    block_e = jnp.minimum(jnp.sum((pad_ends[None, :] <= block_start[:, None]).astype(jnp.int32), axis=1),
                          N_EXPERTS - 1)
    n_used = (pad_ends[-1] // bm).astype(jnp.int32).reshape(1)
    xs = h2.at[slot_token].get(mode="promise_in_bounds")
    ys = _moe_experts(xs, block_e, n_used, w_gate, w_up, w_down, layer)
    y_pairs = ys.at[slots].get(mode="promise_in_bounds")
    return _combine(x1, y_pairs, gw[0:2].T, mods, rows)


def kernel(x_prompt, x_sample, c, state_mlstm_C, state_mlstm_n, state_mlstm_m, state_ret_S, cache_k, cache_v, c_ctx, w_ada, b_ada, norm_mix_gain, norm_ffn_gain, w_in_even, mlstm_conv, mlstm_gate_bias, mlstm_out_gain, ret_decay_logit, ret_out_gain, w_out_even, w_qkv_odd, q_norm_gain, k_norm_gain, lambda_qk, attn_out_gain, w_out_odd, w_router, router_bias, moe_w_gate, moe_w_up, moe_w_down):
    bp, seq, d = x_prompt.shape
    bs, dec_seq, _ = x_sample.shape
    depth = w_ada.shape[0]
    past = cache_k.shape[2]
    rows = _Rows(bp, seq, bs, dec_seq)
    nu = N_DIR * H_A
    assert 1 + bs <= N_COND_PAD

    x = jnp.concatenate([x_prompt.reshape(bp * seq, d), x_sample.reshape(bs * dec_seq, d)], axis=0)
    cond = jnp.concatenate([c_ctx[None, :], c, jnp.zeros((N_COND_PAD - 1 - bs, d), F32)], axis=0)
    mods_all = _modulation_all(cond, w_ada, b_ada).reshape(depth, N_COND_PAD, 6, d)

    w_router_pad = jnp.pad(w_router, ((0, 0), (0, HEAD_W - N_EXPERTS))).astype(BF16)
    router_bias_col = jnp.broadcast_to(router_bias[:, None], (N_EXPERTS, HEAD_W))
    e64 = jnp.asarray(np.kron(np.eye(d // DH_C, dtype=np.float32), np.ones((DH_C, DH_C), np.float32)), BF16)
    cos, sin = _rope_tables(dec_seq)

    st_c, st_n, st_m, st_s, st_k, st_v = [], [], [], [], [], []
    for l in range(depth):
        j = l // 2
        mods = mods_all[l]
        gain_mix = norm_mix_gain[l][None, :]
        if l % 2 == 0:
            w = w_in_even[j]
            wa = 4 * HEAD_W * 4
            gcols = 4 * H_A
            ob = wa + gcols
            w_main = jnp.concatenate([w[:, 0:1536], w[:, ob:ob + 1536], w[:, 1536:2048],
                                      w[:, ob + 1536:ob + 2048]], axis=1).astype(BF16)
            w_g = w[:, wa:wa + gcols]
            zpad = jnp.zeros((d, HEAD_W - nu), F32)
            w_g32 = jnp.concatenate([w_g[:, 0:nu], zpad, w_g[:, nu:2 * nu], zpad], axis=1)
            w_g_hi = w_g32.astype(BF16)
            w_gates = jnp.concatenate([w_g_hi, (w_g32 - w_g_hi.astype(F32)).astype(BF16)], axis=1)
            gb = mlstm_gate_bias[j].reshape(2, nu)
            zb = jnp.zeros((HEAD_W - nu,), F32)
            gate_bias = jnp.concatenate([gb[0], zb, gb[1], zb])[None, :]
            qkva, qkvb, oz, gates = _proj_even(x, mods, gain_mix, w_main, w_gates, mlstm_conv[j], gate_bias, rows)
            c0 = state_mlstm_C[:, j].reshape(bs, nu, HEAD_W, HEAD_W)
            n0 = state_mlstm_n[:, j].reshape(bs, nu, HEAD_W)
            m0 = jnp.broadcast_to(state_mlstm_m[:, j].reshape(bs, nu, 1), (bs, nu, HEAD_W))
            s0 = state_ret_S[:, j].reshape(bs, nu, HEAD_W, HEAD_W)
            ret_logit = jnp.broadcast_to(ret_decay_logit[j].reshape(nu, 1), (nu, HEAD_W))
            hf, hb, cn, nn, mn, sn = _scan_even(qkva, qkvb, gates, c0, n0, m0, s0, ret_logit, rows)
            st_c.append(cn.reshape(bp, N_DIR, H_A, HEAD_W, HEAD_W))
            st_n.append(nn.reshape(bp, N_DIR, H_A, HEAD_W))
            st_m.append(mn[:, :, 0].reshape(bp, N_DIR, H_A))
            st_s.append(sn.reshape(bp, N_DIR, H_B, HEAD_W, HEAD_W))
            mix_in = (hf, hb, oz, mlstm_out_gain[j].reshape(1, H_A * HEAD_W),
                      ret_out_gain[j].reshape(1, H_B * HEAD_W))
            w_out = w_out_even[j].astype(BF16)
        else:
            lam_init = 0.8 - 0.6 * math.exp(-0.3 * l)
            w_qkv = w_qkv_odd[j].astype(BF16)
            qg = jnp.tile(q_norm_gain[j], d // DH_C)[None, :]
            kg = jnp.tile(k_norm_gain[j], d // DH_C)[None, :]
            og = attn_out_gain[j][None, :]
            q_p, k_p, v_p = _proj_odd(x, mods, gain_mix, w_qkv, e64, qg, kg, cos, sin, rows, sample=False)
            q_s, k_s, v_s = _proj_odd(x, mods, gain_mix, w_qkv, e64, qg, kg, cos, sin, rows, sample=True)
            o_p = _attention(q_p.reshape(bp, seq, d), k_p.reshape(bp, seq, d), v_p.reshape(bp, seq, d),
                             None, None, lambda_qk[j], og, lam_init)
            o_s = _attention(q_s.reshape(bs, dec_seq, d), k_s.reshape(bs, dec_seq, d), v_s.reshape(bs, dec_seq, d),
                             cache_k[:, j].reshape(bs, past, d), cache_v[:, j].reshape(bs, past, d),
                             lambda_qk[j], og, lam_init)
            st_k.append(k_p.reshape(bp, seq, H_C, 2, DH_C))
            st_v.append(v_p.reshape(bp, seq, H_C, 2 * DH_C))
            mix_in = (o_p.reshape(bp * seq, d), o_s.reshape(bs * dec_seq, d))
            w_out = w_out_odd[j].astype(BF16)
        x1, h2, eidx, gw, rank, cnt1, cnt2 = _out_and_route(
            mix_in, x, mods, w_out, norm_ffn_gain[l][None, :], w_router_pad, router_bias_col, rows,
            even=(l % 2 == 0))
        x = _moe_layer(x1, h2, eidx, gw, rank, cnt1, cnt2, mods, moe_w_gate, moe_w_up, moe_w_down, l, rows)

    dt = x_prompt.dtype
    y_prompt = x[:rows.n_prompt].reshape(bp, seq, d)
    y_sample = x[rows.n_prompt:].reshape(bs, dec_seq, d)
    return (y_prompt, y_sample,
            jnp.stack(st_c, axis=1).astype(dt), jnp.stack(st_n, axis=1).astype(dt),
            jnp.stack(st_m, axis=1).astype(dt), jnp.stack(st_s, axis=1).astype(dt),
            jnp.stack(st_k, axis=1).astype(dt), jnp.stack(st_v, axis=1).astype(dt))
```

```python
import functools
import math

import numpy as np
import jax
import jax.numpy as jnp
from jax import lax
from jax.experimental import pallas as pl
from jax.experimental.pallas import tpu as pltpu

F32 = jnp.float32
BF16 = jnp.bfloat16

EPS = 1e-6
GRID_W = 64
ROPE_BASE = 10000.0
H_A = 4
H_B = 4
H_C = 8
N_DIR = 2
N_EXPERTS = 16
N_GROUPS = 4
EXPERTS_PER_GROUP = N_EXPERTS // N_GROUPS
HEAD_W = 128
DH_C = 64
N_COND_PAD = 16

ROW_TILE = 256
SCAN_CHUNK = 256
ATTN_TQ = 1024
ATTN_TK = 512
MOE_BLOCK = 512
MOE_CHUNK = 16
MOE_BLOCK_CHUNKS = MOE_BLOCK // MOE_CHUNK
MOE_TILE_CHUNKS = (2 * ROW_TILE + N_EXPERTS * (MOE_CHUNK - 1)) // MOE_CHUNK + 1
MOE_TILE_ROWS = MOE_TILE_CHUNKS * MOE_CHUNK
VMEM_LIMIT = 56 * 1024 * 1024

_HI = lax.Precision.HIGHEST


def _dot(a, b, precision=None):
    return jnp.dot(a, b, preferred_element_type=F32, precision=precision)


def _dot_nt(a, b):
    return lax.dot_general(a, b, (((1,), (1,)), ((), ())), preferred_element_type=F32)


def _dot_tn(a, b):
    return lax.dot_general(a, b, (((0,), (0,)), ((), ())), preferred_element_type=F32)


def _rms(x):
    return x * lax.rsqrt(jnp.mean(x * x, axis=-1, keepdims=True) + EPS)


def _sigmoid(x):
    return 1.0 / (1.0 + jnp.exp(-x))


def _silu(x):
    return x * _sigmoid(x)


def _log_sigmoid(x):
    return jnp.minimum(x, 0.0) - jnp.log1p(jnp.exp(-jnp.abs(x)))


def _params(n_axes):
    return pltpu.CompilerParams(dimension_semantics=("arbitrary",) * n_axes,
                                vmem_limit_bytes=VMEM_LIMIT)


def _mod_kernel(cond_ref, w_ref, b_ref, o_ref):
    s = _silu(cond_ref[...]).astype(BF16)
    o_ref[0] = _dot(s, w_ref[0].astype(BF16)) + b_ref[0]


def _modulation_all(cond, w_ada, b_ada):
    depth, d, n = w_ada.shape
    tn = n // 4
    return pl.pallas_call(
        _mod_kernel,
        grid=(depth, n // tn),
        in_specs=[pl.BlockSpec((N_COND_PAD, d), lambda l, j: (0, 0)),
                  pl.BlockSpec((1, d, tn), lambda l, j: (l, 0, j)),
                  pl.BlockSpec((1, 1, tn), lambda l, j: (l, 0, j))],
        out_specs=pl.BlockSpec((1, N_COND_PAD, tn), lambda l, j: (l, 0, j)),
        out_shape=jax.ShapeDtypeStruct((depth, N_COND_PAD, n), F32),
        compiler_params=_params(2),
        name="adaln_modulation",
    )(cond, w_ada, b_ada.reshape(depth, 1, n))


class _Rows:
    def __init__(self, n_prompt_seq, prompt_len, n_sample_seq, sample_len):
        self.prompt_len = prompt_len
        self.sample_len = sample_len
        self.n_prompt_seq = n_prompt_seq
        self.n_sample_seq = n_sample_seq
        self.n_prompt = n_prompt_seq * prompt_len
        self.n_sample = n_sample_seq * sample_len
        self.total = self.n_prompt + self.n_sample
        assert prompt_len % ROW_TILE == 0 and sample_len % ROW_TILE == 0
        assert self.n_prompt % sample_len == 0 or self.n_sample == 0

    def cond_of_tile(self, i, tile):
        n_p = self.n_prompt // tile
        per_seq = self.sample_len // tile
        return jnp.where(i < n_p, 0, 1 + (i - n_p) // per_seq)


def _proj_even_kernel(x_ref, xp_ref, xn_ref, mod_ref, gain_ref, w_ref, wg_ref, cw_ref, gb_ref,
                      qkva_ref, qkvb_ref, oz_ref, g_ref, *, tm, n_prompt, prompt_len, sample_len):
    i = pl.program_id(0)
    shift = mod_ref[0, 0:1, :]
    scale = mod_ref[0, 1:2, :]
    gain = gain_ref[...]

    def modulated(x):
        return _rms(x) * gain * (1.0 + scale) + shift

    h = modulated(x_ref[...])
    hb = h.astype(BF16)
    halo = jnp.concatenate([xp_ref[0], xn_ref[0]], axis=0)
    hh = modulated(halo).astype(BF16)

    w_qk = w_ref[:, 0:1024]
    qk = _dot(hb, w_qk)
    qk_halo = _dot(hh, w_qk)
    prev_row = qk_halo[7:8, :]
    next_row = qk_halo[8:9, :]
    local = lax.broadcasted_iota(jnp.int32, (tm, 1), 0)
    seq_len = jnp.where(i * tm < n_prompt, prompt_len, sample_len)
    pos = (i * tm + local) & (seq_len - 1)
    prev = pltpu.roll(qk, 1, axis=0)
    prev = jnp.where(local == 0, prev_row, prev)
    prev = jnp.where(pos == 0, 0.0, prev)
    nxt = pltpu.roll(qk, tm - 1, axis=0)
    nxt = jnp.where(local == tm - 1, next_row, nxt)
    nxt = jnp.where(pos == seq_len - 1, 0.0, nxt)
    cw = cw_ref[...]
    act = _silu(cw[0:1, :] * prev + cw[1:2, :] * qk + cw[2:3, :] * nxt)
    k_scale = HEAD_W ** -0.5
    qkva_ref[:, 0:512] = act[:, 0:512].astype(BF16)
    qkva_ref[:, 512:1024] = (act[:, 512:1024] * k_scale).astype(BF16)
    qkva_ref[:, 1024:1536] = _dot(hb, w_ref[:, 1024:1536]).astype(BF16)

    qkvb_ref[:, 0:512] = _dot(hb, w_ref[:, 1536:2048]).astype(BF16)
    qkvb_ref[:, 512:1024] = (_dot(hb, w_ref[:, 2048:2560]) * k_scale).astype(BF16)
    qkvb_ref[:, 1024:1536] = _dot(hb, w_ref[:, 2560:3072]).astype(BF16)
    oz_ref[:, 0:512] = _dot(hb, w_ref[:, 3072:3584]).astype(BF16)
    oz_ref[:, 512:1024] = _dot(hb, w_ref[:, 3584:4096]).astype(BF16)

    h_lo = (h - hb.astype(F32)).astype(BF16)
    g_hl = _dot(hb, wg_ref[...])
    gates = (g_hl[:, 0:256] + g_hl[:, 256:512] + _dot(h_lo, wg_ref[:, 0:256])
             + gb_ref[...])
    g_ref[:, 0:128] = gates[:, 0:128]
    g_ref[:, 128:256] = _log_sigmoid(gates[:, 128:256])


def _proj_even(x, mods, gain, w_main, w_gates, conv_w, gate_bias, rows):
    n, d = x.shape
    tm = ROW_TILE
    x8 = x.reshape(n // 8, 8, d)
    nb8 = n // 8
    kern = functools.partial(_proj_even_kernel, tm=tm, n_prompt=rows.n_prompt,
                             prompt_len=rows.prompt_len, sample_len=rows.sample_len)
    return pl.pallas_call(
        kern,
        grid=(n // tm,),
        in_specs=[pl.BlockSpec((tm, d), lambda i: (i, 0)),
                  pl.BlockSpec((1, 8, d), lambda i: (jnp.maximum(i * (tm // 8) - 1, 0), 0, 0)),
                  pl.BlockSpec((1, 8, d), lambda i: (jnp.minimum((i + 1) * (tm // 8), nb8 - 1), 0, 0)),
                  pl.BlockSpec((1, 6, d), lambda i: (rows.cond_of_tile(i, tm), 0, 0)),
                  pl.BlockSpec((1, d), lambda i: (0, 0)),
                  pl.BlockSpec(w_main.shape, lambda i: (0, 0)),
                  pl.BlockSpec(w_gates.shape, lambda i: (0, 0)),
                  pl.BlockSpec(conv_w.shape, lambda i: (0, 0)),
                  pl.BlockSpec(gate_bias.shape, lambda i: (0, 0))],
        out_specs=[pl.BlockSpec((tm, 1536), lambda i: (i, 0)),
                   pl.BlockSpec((tm, 1536), lambda i: (i, 0)),
                   pl.BlockSpec((tm, 1024), lambda i: (i, 0)),
                   pl.BlockSpec((tm, 256), lambda i: (i, 0))],
        out_shape=[jax.ShapeDtypeStruct((n, 1536), BF16),
                   jax.ShapeDtypeStruct((n, 1536), BF16),
                   jax.ShapeDtypeStruct((n, 1024), BF16),
                   jax.ShapeDtypeStruct((n, 256), F32)],
        compiler_params=_params(1),
        name="proj_even",
    )(x, x8, x8, mods, gain, w_main, w_gates, conv_w, gate_bias)


def _scan_kernel(fb_ref, bb_ref, first_ref, last_ref, sin_ref, sout_ref, isp_ref,
                 qaf_ref, qab_ref, qbf_ref, qbb_ref, gf_ref, gb_ref,
                 c0_ref, n0_ref, m0_ref, s0_ref, rl_ref,
                 hf_ref, hb_ref, cn_ref, nn_ref, mn_ref, sn_ref,
                 cext_sc, s_sc, m_sc, intra_sc, inter_sc, toend_sc, cdec_sc, *, L):
    del fb_ref, bb_ref, sin_ref, sout_ref
    step = pl.program_id(0)
    n_units = N_DIR * H_A
    lane = lax.broadcasted_iota(jnp.int32, (HEAD_W, HEAD_W), 1)
    t_idx = lax.broadcasted_iota(jnp.int32, (L, L), 0)
    s_idx = lax.broadcasted_iota(jnp.int32, (L, L), 1)
    masks = (s_idx <= t_idx, s_idx >= t_idx)

    @pl.when(step == 0)
    def _():
        log_gamma = _log_sigmoid(rl_ref[...])
        rel = jnp.abs(t_idx - s_idx).astype(F32)
        pos_col = lax.broadcasted_iota(jnp.int32, (L, HEAD_W), 0).astype(F32)
        for u in range(n_units):
            lg = log_gamma[u:u + 1, 0:1]
            intra_sc[u] = jnp.where(masks[u // H_B], jnp.exp(lg * rel), 0.0)
            pos = pos_col if u < H_B else (L - 1.0) - pos_col
            inter_sc[u] = jnp.exp(lg * (pos + 1.0))
        unit = lax.broadcasted_iota(jnp.int32, (n_units, L), 0)
        pos_row = lax.broadcasted_iota(jnp.int32, (n_units, L), 1).astype(F32)
        pos_row = jnp.where(unit < H_B, pos_row, (L - 1.0) - pos_row)
        toend_sc[...] = jnp.exp(log_gamma[:, 0:1] * ((L - 1.0) - pos_row))
        cdec_sc[...] = jnp.exp(log_gamma * float(L))

    @pl.when(jnp.logical_and(first_ref[step] == 1, isp_ref[step] == 1))
    def _():
        cext_sc[...] = jnp.zeros_like(cext_sc)
        s_sc[...] = jnp.zeros_like(s_sc)
        m_sc[...] = jnp.zeros_like(m_sc)

    @pl.when(jnp.logical_and(first_ref[step] == 1, isp_ref[step] == 0))
    def _():
        n0 = n0_ref[0]
        n0_t = jnp.concatenate([n0, jnp.zeros((HEAD_W - n_units, HEAD_W), F32)], axis=0).T
        for u in range(n_units):
            cext_sc[u, :, 0:HEAD_W] = c0_ref[0, u]
            cext_sc[u, :, HEAD_W:2 * HEAD_W] = jnp.broadcast_to(n0_t[:, u:u + 1], (HEAD_W, HEAD_W))
            s_sc[u] = s0_ref[0, u]
        m_sc[...] = m0_ref[0]

    ones_ext = jnp.ones((L, HEAD_W), BF16)
    row_l = lax.broadcasted_iota(jnp.int32, (L, HEAD_W), 0)
    tri = tuple(jnp.where(m, 1.0, 0.0) for m in masks)
    qa = (qaf_ref, qab_ref)
    qb = (qbf_ref, qbb_ref)
    g = (gf_ref, gb_ref)
    out = (hf_ref, hb_ref)
    m_all = m_sc[...]
    toend = toend_sc[...]
    cdec = cdec_sc[...]
    m_rows = []

    def wide(x):
        return x if L == HEAD_W else jnp.concatenate([x] * (L // HEAD_W), axis=1)

    for d in range(N_DIR):
        ig_all = g[d][:, 0:HEAD_W]
        lf_all = g[d][:, HEAD_W:2 * HEAD_W]
        bt_all = _dot(tri[d], lf_all, precision=_HI)
        a_all = ig_all - bt_all
        a_rows = a_all.T[0:n_units, :]
        cm_all = a_all
        shift = 1
        while shift < L:
            if d == 0:
                moved = jnp.where(row_l >= shift, pltpu.roll(cm_all, shift, axis=0), -jnp.inf)
            else:
                moved = jnp.where(row_l < L - shift, pltpu.roll(cm_all, L - shift, axis=0), -jnp.inf)
            cm_all = jnp.maximum(cm_all, moved)
            shift *= 2
        end = L - 1 if d == 0 else 0
        for h in range(H_A):
            u = d * H_A + h
            cs = slice(h * HEAD_W, (h + 1) * HEAD_W)
            m_prev = m_all[u:u + 1, :]
            g_rep = jnp.maximum(m_prev, jnp.broadcast_to(cm_all[:, u:u + 1], (L, HEAD_W)))
            bt_rep = jnp.broadcast_to(bt_all[:, u:u + 1], (L, HEAD_W))
            a_row = a_rows[u:u + 1, :]
            dmat = jnp.exp(jnp.where(masks[d], a_row - wide(g_rep), -jnp.inf))
            inter_w = jnp.exp(m_prev - g_rep)
            q = qa[d][:, cs]
            k = qa[d][:, 512 + h * HEAD_W:512 + (h + 1) * HEAD_W]
            v = qa[d][:, 1024 + h * HEAD_W:1024 + (h + 1) * HEAD_W]
            v_ext = jnp.concatenate([v, ones_ext], axis=1)
            s = (_dot_nt(q, k) * dmat).astype(BF16)
            lhs = jnp.concatenate([s, (q.astype(F32) * inter_w).astype(BF16)], axis=1)
            rhs = jnp.concatenate([v_ext, cext_sc[u].astype(BF16)], axis=0)
            num = _dot(lhs, rhs)
            inv = 1.0 / jnp.maximum(jnp.abs(num[:, HEAD_W:2 * HEAD_W]), jnp.exp(-(bt_rep + g_rep)))
            out[d][:, cs] = (num[:, 0:HEAD_W] * inv).astype(out[d].dtype)
            m_prev1 = m_prev[:, 0:1]
            g_end = jnp.maximum(m_prev1, cm_all[end:end + 1, u:u + 1])
            wk_row = jnp.exp(a_row - g_end)
            kw = (k.astype(F32).T * wk_row).astype(BF16)
            cext_sc[u] = jnp.exp(m_prev1 - g_end) * cext_sc[u] + _dot(kw, v_ext)
            m_rows.append(jnp.broadcast_to(bt_all[end:end + 1, u:u + 1] + g_end, (1, HEAD_W)))
            qr = qb[d][:, cs]
            kr = qb[d][:, 512 + h * HEAD_W:512 + (h + 1) * HEAD_W]
            vr = qb[d][:, 1024 + h * HEAD_W:1024 + (h + 1) * HEAD_W]
            sr = (_dot_nt(qr, kr) * intra_sc[u]).astype(BF16)
            lhs = jnp.concatenate([sr, (qr.astype(F32) * inter_sc[u]).astype(BF16)], axis=1)
            rhs = jnp.concatenate([vr, s_sc[u].astype(BF16)], axis=0)
            out[d][:, 512 + h * HEAD_W:512 + (h + 1) * HEAD_W] = _dot(lhs, rhs).astype(out[d].dtype)
            krw = (kr.astype(F32).T * toend[u:u + 1, :]).astype(BF16)
            s_sc[u] = cdec[u:u + 1, :] * s_sc[u] + _dot(krw, vr)
    m_sc[...] = jnp.concatenate(m_rows, axis=0)

    @pl.when(last_ref[step] == 1)
    def _():
        n_cols = jnp.zeros((HEAD_W, HEAD_W), F32)
        for u in range(n_units):
            cn_ref[0, u] = cext_sc[u, :, 0:HEAD_W]
            sn_ref[0, u] = s_sc[u]
            n_cols = jnp.where(lane == u, cext_sc[u, :, HEAD_W:HEAD_W + 1], n_cols)
        nn_ref[0] = n_cols.T[0:n_units, :]
        mn_ref[0] = m_sc[...]


def _scan_tables(rows, L):
    fb, bb, first, last, sin, sout, isp = [], [], [], [], [], [], []
    base = 0
    for kind, n_seq, seq_len in (("p", rows.n_prompt_seq, rows.prompt_len),
                                 ("s", rows.n_sample_seq, rows.sample_len)):
        nc = seq_len // L
        for b in range(n_seq):
            for c in range(nc):
                fb.append(base + b * nc + c)
                bb.append(base + b * nc + nc - 1 - c)
                first.append(int(c == 0))
                last.append(int(c == nc - 1 and kind == "p"))
                sin.append(b if kind == "s" else 0)
                sout.append(b if kind == "p" else rows.n_prompt_seq - 1)
                isp.append(int(kind == "p"))
        base += n_seq * nc
    return [jnp.asarray(np.asarray(t, np.int32)) for t in (fb, bb, first, last, sin, sout, isp)]


def _scan_even(qkva, qkvb, gates, c0, n0, m0, s0, ret_logit, rows):
    L = SCAN_CHUNK
    n = qkva.shape[0]
    tables = _scan_tables(rows, L)
    n_steps = int(tables[0].shape[0])
    nu = N_DIR * H_A
    fwd = lambda w: pl.BlockSpec((L, w), lambda s, fb, bb, fi, la, si, so, ip: (fb[s], 0))
    bwd = lambda w: pl.BlockSpec((L, w), lambda s, fb, bb, fi, la, si, so, ip: (bb[s], 0))
    st_in4 = pl.BlockSpec((1, nu, HEAD_W, HEAD_W), lambda s, fb, bb, fi, la, si, so, ip: (si[s], 0, 0, 0))
    st_in3 = pl.BlockSpec((1, nu, HEAD_W), lambda s, fb, bb, fi, la, si, so, ip: (si[s], 0, 0))
    st_out4 = pl.BlockSpec((1, nu, HEAD_W, HEAD_W), lambda s, fb, bb, fi, la, si, so, ip: (so[s], 0, 0, 0))
    st_out3 = pl.BlockSpec((1, nu, HEAD_W), lambda s, fb, bb, fi, la, si, so, ip: (so[s], 0, 0))
    nps = rows.n_prompt_seq
    return pl.pallas_call(
        functools.partial(_scan_kernel, L=L),
        grid_spec=pltpu.PrefetchScalarGridSpec(
            num_scalar_prefetch=7,
            grid=(n_steps,),
            in_specs=[fwd(1536), bwd(1536), fwd(1536), bwd(1536), fwd(256), bwd(256),
                      st_in4, st_in3, st_in3, st_in4,
                      pl.BlockSpec((nu, HEAD_W), lambda s, *_: (0, 0))],
            out_specs=[fwd(1024), bwd(1024), st_out4, st_out3, st_out3, st_out4],
            scratch_shapes=[pltpu.VMEM((nu, HEAD_W, 2 * HEAD_W), F32),
                            pltpu.VMEM((nu, HEAD_W, HEAD_W), F32),
                            pltpu.VMEM((nu, HEAD_W), F32),
                            pltpu.VMEM((nu, L, L), F32),
                            pltpu.VMEM((nu, L, HEAD_W), F32),
                            pltpu.VMEM((nu, L), F32),
                            pltpu.VMEM((nu, HEAD_W), F32)]),
        out_shape=[jax.ShapeDtypeStruct((n, 1024), BF16),
                   jax.ShapeDtypeStruct((n, 1024), BF16),
                   jax.ShapeDtypeStruct((nps, nu, HEAD_W, HEAD_W), F32),
                   jax.ShapeDtypeStruct((nps, nu, HEAD_W), F32),
                   jax.ShapeDtypeStruct((nps, nu, HEAD_W), F32),
                   jax.ShapeDtypeStruct((nps, nu, HEAD_W, HEAD_W), F32)],
        compiler_params=_params(1),
        name="scan_even",
    )(*tables, qkva, qkva, qkvb, qkvb, gates, gates, c0, n0, m0, s0, ret_logit)


def _proj_odd_kernel(x_ref, mod_ref, gain_ref, w_ref, e_ref, qg_ref, kg_ref, cos_ref, sin_ref,
                     q_ref, k_ref, v_ref, *, rope):
    shift = mod_ref[0, 0:1, :]
    scale = mod_ref[0, 1:2, :]
    hb = (_rms(x_ref[...]) * gain_ref[...] * (1.0 + scale) + shift).astype(BF16)

    def qk_norm(raw, g):
        ss = _dot((raw * raw).astype(BF16), e_ref[...])
        return raw * lax.rsqrt(ss * (1.0 / DH_C) + EPS) * g

    def rotate(y):
        if not rope:
            return y
        lane = lax.broadcasted_iota(jnp.int32, y.shape, 1)
        first_half = (lane & 31) < 16
        partner = jnp.where(first_half, pltpu.roll(y, y.shape[1] - 16, axis=1), pltpu.roll(y, 16, axis=1))
        cos = jnp.concatenate([cos_ref[...]] * (y.shape[1] // HEAD_W), axis=1)
        sin = jnp.concatenate([sin_ref[...]] * (y.shape[1] // HEAD_W), axis=1)
        return y * cos + partner * sin

    q = rotate(qk_norm(_dot(hb, w_ref[:, 0:1024]), qg_ref[...]))
    q_ref[...] = (q * (DH_C ** -0.5 * math.log2(math.e))).astype(q_ref.dtype)
    k = rotate(qk_norm(_dot(hb, w_ref[:, 1024:2048]), kg_ref[...]))
    k_ref[...] = k.astype(k_ref.dtype)
    v_ref[...] = _dot(hb, w_ref[:, 2048:3072]).astype(v_ref.dtype)


def _proj_odd(x, mods, gain, w_qkv, e64, q_gain, k_gain, cos, sin, rows, *, sample):
    d = x.shape[1]
    tm = ROW_TILE
    if sample:
        n, base, kv_dtype = rows.n_sample, rows.n_prompt // tm, BF16
        per_seq = rows.sample_len // tm
        table = lambda i: (i % per_seq, 0)
    else:
        n, base, kv_dtype = rows.n_prompt, 0, F32
        table = lambda i: (0, 0)
    return pl.pallas_call(
        functools.partial(_proj_odd_kernel, rope=sample),
        grid=(n // tm,),
        in_specs=[pl.BlockSpec((tm, d), lambda i: (base + i, 0)),
                  pl.BlockSpec((1, 6, d), lambda i: (rows.cond_of_tile(base + i, tm), 0, 0)),
                  pl.BlockSpec((1, d), lambda i: (0, 0)),
                  pl.BlockSpec(w_qkv.shape, lambda i: (0, 0)),
                  pl.BlockSpec(e64.shape, lambda i: (0, 0)),
                  pl.BlockSpec((1, d), lambda i: (0, 0)),
                  pl.BlockSpec((1, d), lambda i: (0, 0)),
                  pl.BlockSpec((tm, HEAD_W), table),
                  pl.BlockSpec((tm, HEAD_W), table)],
        out_specs=[pl.BlockSpec((tm, d), lambda i: (i, 0))] * 3,
        out_shape=[jax.ShapeDtypeStruct((n, d), BF16),
                   jax.ShapeDtypeStruct((n, d), kv_dtype),
                   jax.ShapeDtypeStruct((n, d), kv_dtype)],
        compiler_params=_params(1),
        name="proj_odd_sample" if sample else "proj_odd_prompt",
    )(x, mods, gain, w_qkv, e64, q_gain, k_gain, cos, sin)


def _rope_tables(sample_len):
    t = np.arange(sample_len)
    nf = DH_C // 4
    inv = ROPE_BASE ** (-np.arange(nf, dtype=np.float32) / nf)
    row = (t // GRID_W).astype(np.float32)[:, None] * inv[None, :]
    col = (t % GRID_W).astype(np.float32)[:, None] * inv[None, :]
    cos64 = np.concatenate([np.cos(row), np.cos(row), np.cos(col), np.cos(col)], axis=1)
    sin64 = np.concatenate([-np.sin(row), np.sin(row), -np.sin(col), np.sin(col)], axis=1)
    cos = np.concatenate([cos64, cos64], axis=1).astype(np.float32)
    sin = np.concatenate([sin64, sin64], axis=1).astype(np.float32)
    return jnp.asarray(cos), jnp.asarray(sin)


def _attn_kernel(*refs, tq, tk, n_main, n_cache, lam_init):
    if n_cache:
        q_ref, k_ref, v_ref, ck_ref, cv_ref, lq_ref, og_ref, o_ref, vt_sc = refs
    else:
        q_ref, k_ref, v_ref, lq_ref, og_ref, o_ref, vt_sc = refs
    fill_w = 256

    @pl.when(pl.program_id(2) == 0)
    def _():
        ones_rows = jnp.where(lax.broadcasted_iota(jnp.int32, (8, fill_w), 0) == 0, 1.0, 0.0).astype(BF16)
        for j in range(n_main // fill_w):
            vt = v_ref[0, j * fill_w:(j + 1) * fill_w, :].astype(F32).T.astype(BF16)
            vt_sc[:, j * fill_w:(j + 1) * fill_w] = jnp.concatenate([vt, ones_rows], axis=0)
        if n_cache:
            vt = cv_ref[0].astype(F32).T.astype(BF16)
            vt_sc[:, n_main:n_main + n_cache] = jnp.concatenate([vt, ones_rows], axis=0)

    qt = q_ref[0].astype(F32).T
    row = lax.broadcasted_iota(jnp.int32, qt.shape, 0)
    qst = jnp.concatenate([jnp.where(row < DH_C, qt, 0.0), jnp.where(row >= DH_C, qt, 0.0)],
                          axis=1).astype(BF16)

    def scores(kt):
        return _dot(kt.astype(BF16), qst)

    def accumulate(carry, s, vt):
        m, acc = carry
        m_new = jnp.maximum(m, jnp.max(s, axis=0, keepdims=True))
        p = jnp.exp2(s - m_new).astype(BF16)
        acc = jnp.exp2(m - m_new) * acc + _dot(vt, p)
        return m_new, acc

    tiles = [(k_ref, j * tk, tk, j * tk) for j in range(n_main // tk)]
    if n_cache:
        tiles = [(ck_ref, 0, n_cache, n_main)] + tiles
    carry = (jnp.full((1, 2 * tq), -jnp.inf, F32), jnp.zeros((HEAD_W + 8, 2 * tq), F32))
    ref0, r0, n0, _ = tiles[0]
    s_next = scores(ref0[0, r0:r0 + n0, :])
    for j, (_, _, n_keys, c0) in enumerate(tiles):
        s_cur = s_next
        if j + 1 < len(tiles):
            ref1, r1, n1, _ = tiles[j + 1]
            s_next = scores(ref1[0, r1:r1 + n1, :])
        carry = accumulate(carry, s_cur, vt_sc[:, c0:c0 + n_keys])
    _, acc = carry
    o_t = acc[0:HEAD_W, :] / acc[HEAD_W:HEAD_W + 1, :]
    lq = lq_ref[...]
    lam = (jnp.exp(jnp.sum(lq[0:1, :] * lq[1:2, :], axis=1, keepdims=True))
           - jnp.exp(jnp.sum(lq[2:3, :] * lq[3:4, :], axis=1, keepdims=True)) + lam_init)
    o = (o_t[:, 0:tq] - lam * o_t[:, tq:2 * tq]).T
    o_ref[0] = (_rms(o) * og_ref[...] * (1.0 - lam_init)).astype(o_ref.dtype)


def _attention(q, k, v, cache_k, cache_v, lam_qk, out_gain, lam_init):
    b, t, d = q.shape
    tq = min(ATTN_TQ, t)
    tk = min(ATTN_TK, t)
    n_cache = 0 if cache_k is None else cache_k.shape[1]
    assert t % tk == 0 and t % tq == 0 and t % 256 == 0 and n_cache in (0, 256)
    seq = lambda n: pl.BlockSpec((1, n, HEAD_W), lambda bi, h, qi: (bi, 0, h))
    in_specs = [pl.BlockSpec((1, tq, HEAD_W), lambda bi, h, qi: (bi, qi, h)), seq(t), seq(t)]
    args = [q, k, v]
    if n_cache:
        in_specs += [seq(n_cache), seq(n_cache)]
        args += [cache_k, cache_v]
    in_specs += [pl.BlockSpec(lam_qk.shape, lambda bi, h, qi: (0, 0)),
                 pl.BlockSpec(out_gain.shape, lambda bi, h, qi: (0, 0))]
    args += [lam_qk, out_gain]
    return pl.pallas_call(
        functools.partial(_attn_kernel, tq=tq, tk=tk, n_main=t, n_cache=n_cache, lam_init=lam_init),
        grid=(b, H_C, t // tq),
        in_specs=in_specs,
        out_specs=pl.BlockSpec((1, tq, HEAD_W), lambda bi, h, qi: (bi, qi, h)),
        out_shape=jax.ShapeDtypeStruct((b, t, d), BF16),
        scratch_shapes=[pltpu.VMEM((HEAD_W + 8, t + n_cache), BF16)],
        compiler_params=_params(3),
        name="diff_attention",
    )(*args)


def _out_kernel(*refs, tm, even, n_prompt_tiles):
    if even:
        (hf_ref, hb_ref, oz_ref, ga_ref, gb_ref, x_ref, mod_ref, w_ref, fg_ref, wr_ref, rb_ref,
         x1_ref, xs_ref, sl_ref, gw_ref, cnt_ref) = refs
    else:
        (op_ref, os_ref, x_ref, mod_ref, w_ref, fg_ref, wr_ref, rb_ref,
         x1_ref, xs_ref, sl_ref, gw_ref, cnt_ref) = refs
    i = pl.program_id(0)

    if even:
        hs = hf_ref[...].astype(F32) + hb_ref[...].astype(F32)
        oz = oz_ref[...].astype(F32)
        parts = []
        for h in range(H_A + H_B):
            cs = slice(h * HEAD_W, (h + 1) * HEAD_W)
            gain = ga_ref[:, cs] if h < H_A else gb_ref[:, (h - H_A) * HEAD_W:(h - H_A + 1) * HEAD_W]
            act = _sigmoid(oz[:, cs]) if h < H_A else _silu(oz[:, cs])
            parts.append((_rms(hs[:, cs]) * gain * act).astype(BF16))
        y_in = jnp.concatenate(parts, axis=1)
    else:
        y_in = jnp.where(i < n_prompt_tiles, op_ref[...], os_ref[...])

    x1 = x_ref[...] + mod_ref[0, 2:3, :] * _dot(y_in, w_ref[...])
    x1_ref[...] = x1
    h2 = _rms(x1) * fg_ref[...] * (1.0 + mod_ref[0, 4:5, :]) + mod_ref[0, 3:4, :]
    h2b = h2.astype(BF16)

    score = _sigmoid(_dot(h2b, wr_ref[...]))
    st = score.T[0:N_EXPERTS, :]
    sel = st + rb_ref[:, 0:1]
    srow = [st[e:e + 1, :] for e in range(N_EXPERTS)]
    brow = [sel[e:e + 1, :] for e in range(N_EXPERTS)]
    epg = EXPERTS_PER_GROUP
    gscore = []
    for g in range(N_GROUPS):
        a = brow[g * epg:(g + 1) * epg]
        best = None
        for p in range(epg):
            for r in range(p + 1, epg):
                pair = a[p] + a[r]
                best = pair if best is None else jnp.maximum(best, pair)
        gscore.append(best)
    gbest, gidx = gscore[0], jnp.zeros((1, tm), jnp.int32)
    for g in range(1, N_GROUPS):
        better = gscore[g] > gbest
        gbest = jnp.where(better, gscore[g], gbest)
        gidx = jnp.where(better, g, gidx)
    vals, sig = [], []
    for p in range(epg):
        vp, sp = brow[p], srow[p]
        for g in range(1, N_GROUPS):
            vp = jnp.where(gidx == g, brow[g * epg + p], vp)
            sp = jnp.where(gidx == g, srow[g * epg + p], sp)
        vals.append(vp)
        sig.append(sp)
    v1, i1, w1 = vals[0], jnp.zeros((1, tm), jnp.int32), sig[0]
    for p in range(1, epg):
        better = vals[p] > v1
        v1 = jnp.where(better, vals[p], v1)
        i1 = jnp.where(better, p, i1)
        w1 = jnp.where(better, sig[p], w1)
    v2 = jnp.full((1, tm), -jnp.inf, F32)
    i2 = jnp.zeros((1, tm), jnp.int32)
    w2 = jnp.zeros((1, tm), F32)
    for p in range(epg):
        better = jnp.logical_and(i1 != p, vals[p] > v2)
        v2 = jnp.where(better, vals[p], v2)
        i2 = jnp.where(better, p, i2)
        w2 = jnp.where(better, sig[p], w2)
    e1 = gidx * epg + i1
    e2 = gidx * epg + i2
    wsum = w1 + w2

    erow = lax.broadcasted_iota(jnp.int32, (N_EXPERTS, tm), 0)
    earlier = jnp.where(lax.broadcasted_iota(jnp.int32, (tm, tm), 0)
                        < lax.broadcasted_iota(jnp.int32, (tm, tm), 1), 1.0, 0.0).astype(BF16)
    oh1 = jnp.where(erow == e1, 1.0, 0.0)
    oh2 = jnp.where(erow == e2, 1.0, 0.0)
    before1 = _dot(oh1.astype(BF16), earlier)
    before2 = _dot(oh2.astype(BF16), earlier)
    cnt1 = jnp.sum(oh1, axis=1, keepdims=True)
    cnt = cnt1 + jnp.sum(oh2, axis=1, keepdims=True)
    padded = jnp.floor((cnt + (MOE_CHUNK - 1.0)) * (1.0 / MOE_CHUNK)) * MOE_CHUNK
    below = jnp.where(lax.broadcasted_iota(jnp.int32, (N_EXPERTS, N_EXPERTS), 1)
                      < lax.broadcasted_iota(jnp.int32, (N_EXPERTS, N_EXPERTS), 0), 1.0, 0.0)
    start = _dot(below, jnp.broadcast_to(padded, (N_EXPERTS, HEAD_W)), precision=_HI)[:, 0:1]
    slot1 = jnp.sum(oh1 * (start + before1), axis=0, keepdims=True).astype(jnp.int32)
    slot2 = jnp.sum(oh2 * (start + cnt1 + before2), axis=0, keepdims=True).astype(jnp.int32)
    local = lax.broadcasted_iota(jnp.int32, (MOE_TILE_ROWS, tm), 0)
    place = jnp.where(jnp.logical_or(local == slot1, local == slot2), 1.0, 0.0).astype(BF16)
    xs_ref[...] = _dot(place, h2b).astype(BF16)
    zi = jnp.zeros((6, tm), jnp.int32)
    sl_ref[...] = jnp.concatenate([slot1, slot2, zi], axis=0)
    gw_ref[...] = jnp.concatenate([w1 / wsum, w2 / wsum, jnp.zeros((6, tm), F32)], axis=0)
    cnt_ref[...] = jnp.broadcast_to(cnt, (N_EXPERTS, HEAD_W))


def _out_and_route(mix_in, x, mods, w_out, ffn_gain, w_router, router_bias, rows, *, even):
    n, d = x.shape
    tm = ROW_TILE
    row = lambda w: pl.BlockSpec((tm, w), lambda i: (i, 0))
    full = lambda a: pl.BlockSpec(a.shape, lambda i: (0,) * a.ndim)
    n_pt = rows.n_prompt // tm
    if even:
        hf, hb, oz, gain_a, gain_b = mix_in
        in_specs = [row(1024), row(1024), row(1024), full(gain_a), full(gain_b)]
        args = [hf, hb, oz, gain_a, gain_b]
    else:
        o_p, o_s = mix_in
        in_specs = [pl.BlockSpec((tm, d), lambda i: (jnp.minimum(i, n_pt - 1), 0)),
                    pl.BlockSpec((tm, d), lambda i: (jnp.maximum(i - n_pt, 0), 0))]
        args = [o_p, o_s]
    in_specs += [row(d), pl.BlockSpec((1, 6, d), lambda i: (rows.cond_of_tile(i, tm), 0, 0)),
                 full(w_out), full(ffn_gain), full(w_router), full(router_bias)]
    args += [x, mods, w_out, ffn_gain, w_router, router_bias]
    col = lambda: pl.BlockSpec((8, tm), lambda i: (0, i))
    n_tiles = n // tm
    return pl.pallas_call(
        functools.partial(_out_kernel, tm=tm, even=even, n_prompt_tiles=n_pt),
        grid=(n_tiles,),
        in_specs=in_specs,
        out_specs=[row(d), pl.BlockSpec((MOE_TILE_ROWS, d), lambda i: (i, 0)), col(), col(),
                   pl.BlockSpec((N_EXPERTS, HEAD_W), lambda i: (i, 0))],
        out_shape=[jax.ShapeDtypeStruct((n, d), F32),
                   jax.ShapeDtypeStruct((n_tiles * MOE_TILE_ROWS, d), BF16),
                   jax.ShapeDtypeStruct((8, n), jnp.int32),
                   jax.ShapeDtypeStruct((8, n), F32),
                   jax.ShapeDtypeStruct((n_tiles * N_EXPERTS, HEAD_W), F32)],
        compiler_params=_params(1),
        name="out_even" if even else "out_odd",
    )(*args)


def _moe_kernel(cin_ref, cout_ref, be_ref, nu_ref, xs_hbm, ys_init_hbm, wg_ref, wu_ref, wd_ref, ys_hbm,
                xbuf, ybuf, wg_sc, wu_sc, wd_sc, in_sem, out_sem):
    del ys_init_hbm
    b = pl.program_id(0)
    n_used = nu_ref[0]
    cb, ch = MOE_BLOCK_CHUNKS, MOE_CHUNK

    def in_copy(blk, slot, c):
        src = pl.multiple_of(cin_ref[blk * cb + c] * ch, ch)
        return pltpu.make_async_copy(xs_hbm.at[pl.ds(src, ch), :], xbuf.at[slot, pl.ds(c * ch, ch), :],
                                     in_sem.at[slot])

    def out_copy(blk, slot, c):
        dst = pl.multiple_of(cout_ref[blk * cb + c] * ch, ch)
        return pltpu.make_async_copy(ybuf.at[slot, pl.ds(c * ch, ch), :], ys_hbm.at[pl.ds(dst, ch), :],
                                     out_sem.at[slot])

    def start_in(blk, slot):
        for c in range(cb):
            in_copy(blk, slot, c).start()

    def wait_out(blk, slot):
        for c in range(cb):
            out_copy(blk, slot, c).wait()

    slot = b % 2

    @pl.when(b == 0)
    def _():
        start_in(0, 0)

    @pl.when(b + 1 < n_used)
    def _():
        start_in(b + 1, 1 - slot)

    @pl.when(b < n_used)
    def _():
        for c in range(cb):
            in_copy(b, slot, c).wait()

        @pl.when(b >= 2)
        def _():
            wait_out(b - 2, slot)

        @pl.when(jnp.logical_or(b == 0, be_ref[b] != be_ref[jnp.maximum(b - 1, 0)]))
        def _():
            wg_sc[...] = wg_ref[0, 0].astype(BF16)
            wu_sc[...] = wu_ref[0, 0].astype(BF16)
            wd_sc[...] = wd_ref[0, 0].astype(BF16)

        x = xbuf[slot]
        a = (_silu(_dot(x, wg_sc[...])) * _dot(x, wu_sc[...])).astype(BF16)
        ybuf[slot] = _dot(a, wd_sc[...]).astype(BF16)
        for c in range(cb):
            out_copy(b, slot, c).start()

    @pl.when(b == pl.num_programs(0) - 1)
    def _():
        @pl.when(n_used >= 2)
        def _():
            wait_out(n_used - 2, n_used % 2)

        wait_out(n_used - 1, (n_used - 1) % 2)


def _moe_experts(xs_local, ys_init, chunk_in, chunk_out, block_e, n_used, w_gate, w_up, w_down, layer):
    d = xs_local.shape[1]
    n_blocks = chunk_in.shape[0] // MOE_BLOCK_CHUNKS
    wspec = pl.BlockSpec((1, 1, d, d), lambda i, ci, co, be, nu: (layer, be[i], 0, 0))
    hbm = pl.BlockSpec(memory_space=pl.ANY)
    return pl.pallas_call(
        _moe_kernel,
        grid_spec=pltpu.PrefetchScalarGridSpec(
            num_scalar_prefetch=4,
            grid=(n_blocks,),
            in_specs=[hbm, hbm, wspec, wspec, wspec],
            out_specs=hbm,
            scratch_shapes=[pltpu.VMEM((2, MOE_BLOCK, d), BF16), pltpu.VMEM((2, MOE_BLOCK, d), BF16),
                            pltpu.VMEM((d, d), BF16), pltpu.VMEM((d, d), BF16), pltpu.VMEM((d, d), BF16),
                            pltpu.SemaphoreType.DMA((2,)), pltpu.SemaphoreType.DMA((2,))]),
        out_shape=jax.ShapeDtypeStruct(ys_init.shape, BF16),
        input_output_aliases={5: 0},
        compiler_params=_params(1),
        name="moe_experts",
    )(chunk_in, chunk_out, block_e, n_used, xs_local, ys_init, w_gate, w_up, w_down)


def _combine_kernel(x_ref, ys_ref, sl_ref, gw_ref, mod_ref, o_ref):
    sl = sl_ref[...]
    w = gw_ref[...]
    ys = ys_ref[...]
    local = lax.broadcasted_iota(jnp.int32, (sl.shape[0], ys.shape[0]), 1)
    pick1 = jnp.where(local == sl[:, 0:1], 1.0, 0.0).astype(BF16)
    pick2 = jnp.where(local == sl[:, 1:2], 1.0, 0.0).astype(BF16)
    y = w[:, 0:1] * _dot(pick1, ys) + w[:, 1:2] * _dot(pick2, ys)
    o_ref[...] = x_ref[...] + mod_ref[0, 5:6, :] * y


def _combine(x1, ys_local, slot_cols, gw_cols, mods, rows):
    n, d = x1.shape
    tm = ROW_TILE
    return pl.pallas_call(
        _combine_kernel,
        grid=(n // tm,),
        in_specs=[pl.BlockSpec((tm, d), lambda i: (i, 0)),
                  pl.BlockSpec((MOE_TILE_ROWS, d), lambda i: (i, 0)),
                  pl.BlockSpec((tm, 2), lambda i: (i, 0)),
                  pl.BlockSpec((tm, 2), lambda i: (i, 0)),
                  pl.BlockSpec((1, 6, d), lambda i: (rows.cond_of_tile(i, tm), 0, 0))],
        out_specs=pl.BlockSpec((tm, d), lambda i: (i, 0)),
        out_shape=jax.ShapeDtypeStruct((n, d), F32),
        compiler_params=_params(1),
        name="moe_combine",
    )(x1, ys_local, slot_cols, gw_cols, mods)


def _chunk_tables(counts):
    n_tiles = counts.shape[0]
    cb, tc = MOE_BLOCK_CHUNKS, MOE_TILE_CHUNKS
    n_list = (n_tiles * (tc - 1) + N_EXPERTS * (cb - 1) + cb - 1) // cb * cb
    nch = (counts + MOE_CHUNK - 1) // MOE_CHUNK
    first = jnp.cumsum(nch, axis=1) - nch
    total = jnp.sum(nch, axis=0)
    padded = (total + cb - 1) // cb * cb
    e_end = jnp.cumsum(padded)
    e_start = e_end - padded
    before = (jnp.cumsum(nch, axis=0) - nch).T
    j = jnp.arange(n_list, dtype=jnp.int32)
    e_j = jnp.minimum(jnp.sum((e_end[None, :] <= j[:, None]).astype(jnp.int32), axis=1), N_EXPERTS - 1)
    off = j - e_start[e_j]
    before_j = before[e_j]
    t_j = jnp.clip(jnp.sum((before_j <= off[:, None]).astype(jnp.int32), axis=1) - 1, 0, n_tiles - 1)
    seg_off = off - jnp.take_along_axis(before_j, t_j[:, None], axis=1)[:, 0]
    valid = jnp.logical_and(seg_off < nch[t_j, e_j], j < e_end[-1])
    src = t_j * tc + first[t_j, e_j] + seg_off
    scratch = n_tiles * tc + e_j * cb + jnp.clip(off - total[e_j], 0, cb - 1)
    chunk_in = jnp.where(valid, src, tc - 1).astype(jnp.int32)
    chunk_out = jnp.where(valid, src, scratch).astype(jnp.int32)
    block_e = e_j[::cb]
    n_used = (e_end[-1] // cb).astype(jnp.int32).reshape(1)
    return chunk_in, chunk_out, block_e, n_used


def _moe_layer(x1, xs_local, slots, gw, counts, mods, w_gate, w_up, w_down, layer, rows):
    n, d = x1.shape
    n_tiles = n // ROW_TILE
    counts = counts[:, 0].astype(jnp.int32).reshape(n_tiles, N_EXPERTS)
    chunk_in, chunk_out, block_e, n_used = _chunk_tables(counts)
    scratch_tiles = -(-N_EXPERTS * MOE_BLOCK_CHUNKS // MOE_TILE_CHUNKS)
    ys_init = jnp.zeros(((n_tiles + scratch_tiles) * MOE_TILE_ROWS, d), BF16)
    ys_local = _moe_experts(xs_local, ys_init, chunk_in, chunk_out, block_e, n_used, w_gate, w_up, w_down, layer)
    return _combine(x1, ys_local, slots[0:2].T, gw[0:2].T, mods, rows)


def kernel(x_prompt, x_sample, c, state_mlstm_C, state_mlstm_n, state_mlstm_m, state_ret_S, cache_k, cache_v, c_ctx, w_ada, b_ada, norm_mix_gain, norm_ffn_gain, w_in_even, mlstm_conv, mlstm_gate_bias, mlstm_out_gain, ret_decay_logit, ret_out_gain, w_out_even, w_qkv_odd, q_norm_gain, k_norm_gain, lambda_qk, attn_out_gain, w_out_odd, w_router, router_bias, moe_w_gate, moe_w_up, moe_w_down):
    bp, seq, d = x_prompt.shape
    bs, dec_seq, _ = x_sample.shape
    depth = w_ada.shape[0]
    past = cache_k.shape[2]
    rows = _Rows(bp, seq, bs, dec_seq)
    nu = N_DIR * H_A
    assert 1 + bs <= N_COND_PAD

    x = jnp.concatenate([x_prompt.reshape(bp * seq, d), x_sample.reshape(bs * dec_seq, d)], axis=0)
    cond = jnp.concatenate([c_ctx[None, :], c, jnp.zeros((N_COND_PAD - 1 - bs, d), F32)], axis=0)
    mods_all = _modulation_all(cond, w_ada, b_ada).reshape(depth, N_COND_PAD, 6, d)

    w_router_pad = jnp.pad(w_router, ((0, 0), (0, HEAD_W - N_EXPERTS))).astype(BF16)
    router_bias_col = jnp.broadcast_to(router_bias[:, None], (N_EXPERTS, HEAD_W))
    e64 = jnp.asarray(np.kron(np.eye(d // DH_C, dtype=np.float32), np.ones((DH_C, DH_C), np.float32)), BF16)
    cos, sin = _rope_tables(dec_seq)

    st_c, st_n, st_m, st_s, st_k, st_v = [], [], [], [], [], []
    for l in range(depth):
        j = l // 2
        mods = mods_all[l]
        gain_mix = norm_mix_gain[l][None, :]
        if l % 2 == 0:
            w = w_in_even[j]
            wa = 4 * HEAD_W * 4
            gcols = 4 * H_A
            ob = wa + gcols
            w_main = jnp.concatenate([w[:, 0:1536], w[:, ob:ob + 1536], w[:, 1536:2048],
                                      w[:, ob + 1536:ob + 2048]], axis=1).astype(BF16)
            w_g = w[:, wa:wa + gcols]
            zpad = jnp.zeros((d, HEAD_W - nu), F32)
            w_g32 = jnp.concatenate([w_g[:, 0:nu], zpad, w_g[:, nu:2 * nu], zpad], axis=1)
            w_g_hi = w_g32.astype(BF16)
            w_gates = jnp.concatenate([w_g_hi, (w_g32 - w_g_hi.astype(F32)).astype(BF16)], axis=1)
            gb = mlstm_gate_bias[j].reshape(2, nu)
            zb = jnp.zeros((HEAD_W - nu,), F32)
            gate_bias = jnp.concatenate([gb[0], zb, gb[1], zb])[None, :]
            qkva, qkvb, oz, gates = _proj_even(x, mods, gain_mix, w_main, w_gates, mlstm_conv[j], gate_bias, rows)
            c0 = state_mlstm_C[:, j].reshape(bs, nu, HEAD_W, HEAD_W)
            n0 = state_mlstm_n[:, j].reshape(bs, nu, HEAD_W)
            m0 = jnp.broadcast_to(state_mlstm_m[:, j].reshape(bs, nu, 1), (bs, nu, HEAD_W))
            s0 = state_ret_S[:, j].reshape(bs, nu, HEAD_W, HEAD_W)
            ret_logit = jnp.broadcast_to(ret_decay_logit[j].reshape(nu, 1), (nu, HEAD_W))
            hf, hb, cn, nn, mn, sn = _scan_even(qkva, qkvb, gates, c0, n0, m0, s0, ret_logit, rows)
            st_c.append(cn.reshape(bp, N_DIR, H_A, HEAD_W, HEAD_W))
            st_n.append(nn.reshape(bp, N_DIR, H_A, HEAD_W))
            st_m.append(mn[:, :, 0].reshape(bp, N_DIR, H_A))
            st_s.append(sn.reshape(bp, N_DIR, H_B, HEAD_W, HEAD_W))
            mix_in = (hf, hb, oz, mlstm_out_gain[j].reshape(1, H_A * HEAD_W),
                      ret_out_gain[j].reshape(1, H_B * HEAD_W))
            w_out = w_out_even[j].astype(BF16)
        else:
            lam_init = 0.8 - 0.6 * math.exp(-0.3 * l)
            w_qkv = w_qkv_odd[j].astype(BF16)
            qg = jnp.tile(q_norm_gain[j], d // DH_C)[None, :]
            kg = jnp.tile(k_norm_gain[j], d // DH_C)[None, :]
            og = attn_out_gain[j][None, :]
            q_p, k_p, v_p = _proj_odd(x, mods, gain_mix, w_qkv, e64, qg, kg, cos, sin, rows, sample=False)
            q_s, k_s, v_s = _proj_odd(x, mods, gain_mix, w_qkv, e64, qg, kg, cos, sin, rows, sample=True)
            o_p = _attention(q_p.reshape(bp, seq, d), k_p.reshape(bp, seq, d), v_p.reshape(bp, seq, d),
                             None, None, lambda_qk[j], og, lam_init)
            o_s = _attention(q_s.reshape(bs, dec_seq, d), k_s.reshape(bs, dec_seq, d), v_s.reshape(bs, dec_seq, d),
                             cache_k[:, j].reshape(bs, past, d), cache_v[:, j].reshape(bs, past, d),
                             lambda_qk[j], og, lam_init)
            st_k.append(k_p.reshape(bp, seq, H_C, 2, DH_C))
            st_v.append(v_p.reshape(bp, seq, H_C, 2 * DH_C))
            mix_in = (o_p.reshape(bp * seq, d), o_s.reshape(bs * dec_seq, d))
            w_out = w_out_odd[j].astype(BF16)
        x1, xs_local, slots, gw, counts = _out_and_route(
            mix_in, x, mods, w_out, norm_ffn_gain[l][None, :], w_router_pad, router_bias_col, rows,
            even=(l % 2 == 0))
        x = _moe_layer(x1, xs_local, slots, gw, counts, mods, moe_w_gate, moe_w_up, moe_w_down, l, rows)

    dt = x_prompt.dtype
    y_prompt = x[:rows.n_prompt].reshape(bp, seq, d)
    y_sample = x[rows.n_prompt:].reshape(bs, dec_seq, d)
    return (y_prompt, y_sample,
            jnp.stack(st_c, axis=1).astype(dt), jnp.stack(st_n, axis=1).astype(dt),
            jnp.stack(st_m, axis=1).astype(dt), jnp.stack(st_s, axis=1).astype(dt),
            jnp.stack(st_k, axis=1).astype(dt), jnp.stack(st_v, axis=1).astype(dt))
```

```python
import functools
import math

import numpy as np
import jax
import jax.numpy as jnp
from jax import lax
from jax.experimental import pallas as pl
from jax.experimental.pallas import tpu as pltpu

F32 = jnp.float32
BF16 = jnp.bfloat16

EPS = 1e-6
GRID_W = 64
ROPE_BASE = 10000.0
H_A = 4
H_B = 4
H_C = 8
N_DIR = 2
N_EXPERTS = 16
N_GROUPS = 4
EXPERTS_PER_GROUP = N_EXPERTS // N_GROUPS
HEAD_W = 128
DH_C = 64
N_COND_PAD = 16

ROW_TILE = 256
SCAN_CHUNK = 256
ATTN_TQ = 1024
ATTN_TK = 512
MOE_BLOCK = 512
MOE_CHUNK = 16
MOE_BLOCK_CHUNKS = MOE_BLOCK // MOE_CHUNK
MOE_TILE_CHUNKS = (2 * ROW_TILE + N_EXPERTS * (MOE_CHUNK - 1)) // MOE_CHUNK + 1
MOE_TILE_ROWS = MOE_TILE_CHUNKS * MOE_CHUNK
MOE_SCRATCH_TILES = -(-N_EXPERTS * MOE_BLOCK_CHUNKS // MOE_TILE_CHUNKS)
VMEM_LIMIT = 56 * 1024 * 1024

_HI = lax.Precision.HIGHEST


def _dot(a, b, precision=None):
    return jnp.dot(a, b, preferred_element_type=F32, precision=precision)


def _dot_nt(a, b):
    return lax.dot_general(a, b, (((1,), (1,)), ((), ())), preferred_element_type=F32)


def _dot_tn(a, b):
    return lax.dot_general(a, b, (((0,), (0,)), ((), ())), preferred_element_type=F32)


def _rms(x):
    return x * lax.rsqrt(jnp.mean(x * x, axis=-1, keepdims=True) + EPS)


def _sigmoid(x):
    return 1.0 / (1.0 + jnp.exp(-x))


def _silu(x):
    return x * _sigmoid(x)


def _log_sigmoid(x):
    return jnp.minimum(x, 0.0) - jnp.log1p(jnp.exp(-jnp.abs(x)))


def _params(n_axes):
    return pltpu.CompilerParams(dimension_semantics=("arbitrary",) * n_axes,
                                vmem_limit_bytes=VMEM_LIMIT)


def _mod_kernel(cond_ref, w_ref, b_ref, o_ref):
    s = _silu(cond_ref[...]).astype(BF16)
    o_ref[0] = _dot(s, w_ref[0].astype(BF16)) + b_ref[0]


def _modulation_all(cond, w_ada, b_ada):
    depth, d, n = w_ada.shape
    tn = n // 4
    return pl.pallas_call(
        _mod_kernel,
        grid=(depth, n // tn),
        in_specs=[pl.BlockSpec((N_COND_PAD, d), lambda l, j: (0, 0)),
                  pl.BlockSpec((1, d, tn), lambda l, j: (l, 0, j)),
                  pl.BlockSpec((1, 1, tn), lambda l, j: (l, 0, j))],
        out_specs=pl.BlockSpec((1, N_COND_PAD, tn), lambda l, j: (l, 0, j)),
        out_shape=jax.ShapeDtypeStruct((depth, N_COND_PAD, n), F32),
        compiler_params=_params(2),
        name="adaln_modulation",
    )(cond, w_ada, b_ada.reshape(depth, 1, n))


class _Rows:
    def __init__(self, n_prompt_seq, prompt_len, n_sample_seq, sample_len):
        self.prompt_len = prompt_len
        self.sample_len = sample_len
        self.n_prompt_seq = n_prompt_seq
        self.n_sample_seq = n_sample_seq
        self.n_prompt = n_prompt_seq * prompt_len
        self.n_sample = n_sample_seq * sample_len
        self.total = self.n_prompt + self.n_sample
        assert prompt_len % ROW_TILE == 0 and sample_len % ROW_TILE == 0
        assert self.n_prompt % sample_len == 0 or self.n_sample == 0

    def cond_of_tile(self, i, tile):
        n_p = self.n_prompt // tile
        per_seq = self.sample_len // tile
        return jnp.where(i < n_p, 0, 1 + (i - n_p) // per_seq)


def _proj_even_kernel(x_ref, xp_ref, xn_ref, mod_ref, gain_ref, w_ref, wg_ref, cw_ref, gb_ref,
                      qkva_ref, qkvb_ref, oz_ref, g_ref, *, tm, n_prompt, prompt_len, sample_len):
    i = pl.program_id(0)
    shift = mod_ref[0, 0:1, :]
    scale = mod_ref[0, 1:2, :]
    gain = gain_ref[...]

    def modulated(x):
        return _rms(x) * gain * (1.0 + scale) + shift

    h = modulated(x_ref[...])
    hb = h.astype(BF16)
    halo = jnp.concatenate([xp_ref[0], xn_ref[0]], axis=0)
    hh = modulated(halo).astype(BF16)

    w_qk = w_ref[:, 0:1024]
    qk = _dot(hb, w_qk)
    qk_halo = _dot(hh, w_qk)
    prev_row = qk_halo[7:8, :]
    next_row = qk_halo[8:9, :]
    local = lax.broadcasted_iota(jnp.int32, (tm, 1), 0)
    seq_len = jnp.where(i * tm < n_prompt, prompt_len, sample_len)
    pos = (i * tm + local) & (seq_len - 1)
    prev = pltpu.roll(qk, 1, axis=0)
    prev = jnp.where(local == 0, prev_row, prev)
    prev = jnp.where(pos == 0, 0.0, prev)
    nxt = pltpu.roll(qk, tm - 1, axis=0)
    nxt = jnp.where(local == tm - 1, next_row, nxt)
    nxt = jnp.where(pos == seq_len - 1, 0.0, nxt)
    cw = cw_ref[...]
    act = _silu(cw[0:1, :] * prev + cw[1:2, :] * qk + cw[2:3, :] * nxt)
    k_scale = HEAD_W ** -0.5
    qkva_ref[:, 0:512] = act[:, 0:512].astype(BF16)
    qkva_ref[:, 512:1024] = (act[:, 512:1024] * k_scale).astype(BF16)
    qkva_ref[:, 1024:1536] = _dot(hb, w_ref[:, 1024:1536]).astype(BF16)

    qkvb_ref[:, 0:512] = _dot(hb, w_ref[:, 1536:2048]).astype(BF16)
    qkvb_ref[:, 512:1024] = (_dot(hb, w_ref[:, 2048:2560]) * k_scale).astype(BF16)
    qkvb_ref[:, 1024:1536] = _dot(hb, w_ref[:, 2560:3072]).astype(BF16)
    oz_ref[:, 0:512] = _dot(hb, w_ref[:, 3072:3584]).astype(BF16)
    oz_ref[:, 512:1024] = _dot(hb, w_ref[:, 3584:4096]).astype(BF16)

    h_lo = (h - hb.astype(F32)).astype(BF16)
    g_hl = _dot(hb, wg_ref[...])
    gates = (g_hl[:, 0:256] + g_hl[:, 256:512] + _dot(h_lo, wg_ref[:, 0:256])
             + gb_ref[...])
    g_ref[:, 0:128] = gates[:, 0:128]
    g_ref[:, 128:256] = _log_sigmoid(gates[:, 128:256])


def _proj_even(x, mods, gain, w_main, w_gates, conv_w, gate_bias, rows):
    n, d = x.shape
    tm = ROW_TILE
    x8 = x.reshape(n // 8, 8, d)
    nb8 = n // 8
    kern = functools.partial(_proj_even_kernel, tm=tm, n_prompt=rows.n_prompt,
                             prompt_len=rows.prompt_len, sample_len=rows.sample_len)
    return pl.pallas_call(
        kern,
        grid=(n // tm,),
        in_specs=[pl.BlockSpec((tm, d), lambda i: (i, 0)),
                  pl.BlockSpec((1, 8, d), lambda i: (jnp.maximum(i * (tm // 8) - 1, 0), 0, 0)),
                  pl.BlockSpec((1, 8, d), lambda i: (jnp.minimum((i + 1) * (tm // 8), nb8 - 1), 0, 0)),
                  pl.BlockSpec((1, 6, d), lambda i: (rows.cond_of_tile(i, tm), 0, 0)),
                  pl.BlockSpec((1, d), lambda i: (0, 0)),
                  pl.BlockSpec(w_main.shape, lambda i: (0, 0)),
                  pl.BlockSpec(w_gates.shape, lambda i: (0, 0)),
                  pl.BlockSpec(conv_w.shape, lambda i: (0, 0)),
                  pl.BlockSpec(gate_bias.shape, lambda i: (0, 0))],
        out_specs=[pl.BlockSpec((tm, 1536), lambda i: (i, 0)),
                   pl.BlockSpec((tm, 1536), lambda i: (i, 0)),
                   pl.BlockSpec((tm, 1024), lambda i: (i, 0)),
                   pl.BlockSpec((tm, 256), lambda i: (i, 0))],
        out_shape=[jax.ShapeDtypeStruct((n, 1536), BF16),
                   jax.ShapeDtypeStruct((n, 1536), BF16),
                   jax.ShapeDtypeStruct((n, 1024), BF16),
                   jax.ShapeDtypeStruct((n, 256), F32)],
        compiler_params=_params(1),
        name="proj_even",
    )(x, x8, x8, mods, gain, w_main, w_gates, conv_w, gate_bias)


def _scan_kernel(fb_ref, bb_ref, first_ref, last_ref, sin_ref, sout_ref, isp_ref,
                 qaf_ref, qab_ref, qbf_ref, qbb_ref, gf_ref, gb_ref,
                 c0_ref, n0_ref, m0_ref, s0_ref, rl_ref,
                 hf_ref, hb_ref, cn_ref, nn_ref, mn_ref, sn_ref,
                 cext_sc, s_sc, m_sc, intra_sc, inter_sc, toend_sc, cdec_sc, *, L):
    del fb_ref, bb_ref, sin_ref, sout_ref
    step = pl.program_id(0)
    n_units = N_DIR * H_A
    lane = lax.broadcasted_iota(jnp.int32, (HEAD_W, HEAD_W), 1)
    t_idx = lax.broadcasted_iota(jnp.int32, (L, L), 0)
    s_idx = lax.broadcasted_iota(jnp.int32, (L, L), 1)
    masks = (s_idx <= t_idx, s_idx >= t_idx)

    @pl.when(step == 0)
    def _():
        log_gamma = _log_sigmoid(rl_ref[...])
        rel = jnp.abs(t_idx - s_idx).astype(F32)
        pos_col = lax.broadcasted_iota(jnp.int32, (L, HEAD_W), 0).astype(F32)
        for u in range(n_units):
            lg = log_gamma[u:u + 1, 0:1]
            intra_sc[u] = jnp.where(masks[u // H_B], jnp.exp(lg * rel), 0.0)
            pos = pos_col if u < H_B else (L - 1.0) - pos_col
            inter_sc[u] = jnp.exp(lg * (pos + 1.0))
        unit = lax.broadcasted_iota(jnp.int32, (n_units, L), 0)
        pos_row = lax.broadcasted_iota(jnp.int32, (n_units, L), 1).astype(F32)
        pos_row = jnp.where(unit < H_B, pos_row, (L - 1.0) - pos_row)
        toend_sc[...] = jnp.exp(log_gamma[:, 0:1] * ((L - 1.0) - pos_row))
        cdec_sc[...] = jnp.exp(log_gamma * float(L))

    @pl.when(jnp.logical_and(first_ref[step] == 1, isp_ref[step] == 1))
    def _():
        cext_sc[...] = jnp.zeros_like(cext_sc)
        s_sc[...] = jnp.zeros_like(s_sc)
        m_sc[...] = jnp.zeros_like(m_sc)

    @pl.when(jnp.logical_and(first_ref[step] == 1, isp_ref[step] == 0))
    def _():
        n0 = n0_ref[0]
        n0_t = jnp.concatenate([n0, jnp.zeros((HEAD_W - n_units, HEAD_W), F32)], axis=0).T
        for u in range(n_units):
            cext_sc[u, :, 0:HEAD_W] = c0_ref[0, u]
            cext_sc[u, :, HEAD_W:2 * HEAD_W] = jnp.broadcast_to(n0_t[:, u:u + 1], (HEAD_W, HEAD_W))
            s_sc[u] = s0_ref[0, u]
        m_sc[...] = m0_ref[0]

    ones_ext = jnp.ones((L, HEAD_W), BF16)
    row_l = lax.broadcasted_iota(jnp.int32, (L, HEAD_W), 0)
    tri = tuple(jnp.where(m, 1.0, 0.0) for m in masks)
    qa = (qaf_ref, qab_ref)
    qb = (qbf_ref, qbb_ref)
    g = (gf_ref, gb_ref)
    out = (hf_ref, hb_ref)
    m_all = m_sc[...]
    toend = toend_sc[...]
    cdec = cdec_sc[...]
    m_rows = []

    def wide(x):
        return x if L == HEAD_W else jnp.concatenate([x] * (L // HEAD_W), axis=1)

    for d in range(N_DIR):
        ig_all = g[d][:, 0:HEAD_W]
        lf_all = g[d][:, HEAD_W:2 * HEAD_W]
        bt_all = _dot(tri[d], lf_all, precision=_HI)
        a_all = ig_all - bt_all
        a_rows = a_all.T[0:n_units, :]
        cm_all = a_all
        shift = 1
        while shift < L:
            if d == 0:
                moved = jnp.where(row_l >= shift, pltpu.roll(cm_all, shift, axis=0), -jnp.inf)
            else:
                moved = jnp.where(row_l < L - shift, pltpu.roll(cm_all, L - shift, axis=0), -jnp.inf)
            cm_all = jnp.maximum(cm_all, moved)
            shift *= 2
        end = L - 1 if d == 0 else 0
        for h in range(H_A):
            u = d * H_A + h
            cs = slice(h * HEAD_W, (h + 1) * HEAD_W)
            m_prev = m_all[u:u + 1, :]
            g_rep = jnp.maximum(m_prev, jnp.broadcast_to(cm_all[:, u:u + 1], (L, HEAD_W)))
            bt_rep = jnp.broadcast_to(bt_all[:, u:u + 1], (L, HEAD_W))
            a_row = a_rows[u:u + 1, :]
            dmat = jnp.exp(jnp.where(masks[d], a_row - wide(g_rep), -jnp.inf))
            inter_w = jnp.exp(m_prev - g_rep)
            q = qa[d][:, cs]
            k = qa[d][:, 512 + h * HEAD_W:512 + (h + 1) * HEAD_W]
            v = qa[d][:, 1024 + h * HEAD_W:1024 + (h + 1) * HEAD_W]
            v_ext = jnp.concatenate([v, ones_ext], axis=1)
            s = (_dot_nt(q, k) * dmat).astype(BF16)
            lhs = jnp.concatenate([s, (q.astype(F32) * inter_w).astype(BF16)], axis=1)
            rhs = jnp.concatenate([v_ext, cext_sc[u].astype(BF16)], axis=0)
            num = _dot(lhs, rhs)
            inv = 1.0 / jnp.maximum(jnp.abs(num[:, HEAD_W:2 * HEAD_W]), jnp.exp(-(bt_rep + g_rep)))
            out[d][:, cs] = (num[:, 0:HEAD_W] * inv).astype(out[d].dtype)
            m_prev1 = m_prev[:, 0:1]
            g_end = jnp.maximum(m_prev1, cm_all[end:end + 1, u:u + 1])
            wk_row = jnp.exp(a_row - g_end)
            kw = (k.astype(F32).T * wk_row).astype(BF16)
            cext_sc[u] = jnp.exp(m_prev1 - g_end) * cext_sc[u] + _dot(kw, v_ext)
            m_rows.append(jnp.broadcast_to(bt_all[end:end + 1, u:u + 1] + g_end, (1, HEAD_W)))
            qr = qb[d][:, cs]
            kr = qb[d][:, 512 + h * HEAD_W:512 + (h + 1) * HEAD_W]
            vr = qb[d][:, 1024 + h * HEAD_W:1024 + (h + 1) * HEAD_W]
            sr = (_dot_nt(qr, kr) * intra_sc[u]).astype(BF16)
            lhs = jnp.concatenate([sr, (qr.astype(F32) * inter_sc[u]).astype(BF16)], axis=1)
            rhs = jnp.concatenate([vr, s_sc[u].astype(BF16)], axis=0)
            out[d][:, 512 + h * HEAD_W:512 + (h + 1) * HEAD_W] = _dot(lhs, rhs).astype(out[d].dtype)
            krw = (kr.astype(F32).T * toend[u:u + 1, :]).astype(BF16)
            s_sc[u] = cdec[u:u + 1, :] * s_sc[u] + _dot(krw, vr)
    m_sc[...] = jnp.concatenate(m_rows, axis=0)

    @pl.when(last_ref[step] == 1)
    def _():
        n_cols = jnp.zeros((HEAD_W, HEAD_W), F32)
        for u in range(n_units):
            cn_ref[0, u] = cext_sc[u, :, 0:HEAD_W]
            sn_ref[0, u] = s_sc[u]
            n_cols = jnp.where(lane == u, cext_sc[u, :, HEAD_W:HEAD_W + 1], n_cols)
        nn_ref[0] = n_cols.T[0:n_units, :]
        mn_ref[0] = m_sc[...]


def _scan_tables(rows, L):
    fb, bb, first, last, sin, sout, isp = [], [], [], [], [], [], []
    base = 0
    for kind, n_seq, seq_len in (("p", rows.n_prompt_seq, rows.prompt_len),
                                 ("s", rows.n_sample_seq, rows.sample_len)):
        nc = seq_len // L
        for b in range(n_seq):
            for c in range(nc):
                fb.append(base + b * nc + c)
                bb.append(base + b * nc + nc - 1 - c)
                first.append(int(c == 0))
                last.append(int(c == nc - 1 and kind == "p"))
                sin.append(b if kind == "s" else 0)
                sout.append(b if kind == "p" else rows.n_prompt_seq - 1)
                isp.append(int(kind == "p"))
        base += n_seq * nc
    return [jnp.asarray(np.asarray(t, np.int32)) for t in (fb, bb, first, last, sin, sout, isp)]


def _scan_even(qkva, qkvb, gates, c0, n0, m0, s0, ret_logit, rows):
    L = SCAN_CHUNK
    n = qkva.shape[0]
    tables = _scan_tables(rows, L)
    n_steps = int(tables[0].shape[0])
    nu = N_DIR * H_A
    fwd = lambda w: pl.BlockSpec((L, w), lambda s, fb, bb, fi, la, si, so, ip: (fb[s], 0))
    bwd = lambda w: pl.BlockSpec((L, w), lambda s, fb, bb, fi, la, si, so, ip: (bb[s], 0))
    st_in4 = pl.BlockSpec((1, nu, HEAD_W, HEAD_W), lambda s, fb, bb, fi, la, si, so, ip: (si[s], 0, 0, 0))
    st_in3 = pl.BlockSpec((1, nu, HEAD_W), lambda s, fb, bb, fi, la, si, so, ip: (si[s], 0, 0))
    st_out4 = pl.BlockSpec((1, nu, HEAD_W, HEAD_W), lambda s, fb, bb, fi, la, si, so, ip: (so[s], 0, 0, 0))
    st_out3 = pl.BlockSpec((1, nu, HEAD_W), lambda s, fb, bb, fi, la, si, so, ip: (so[s], 0, 0))
    nps = rows.n_prompt_seq
    return pl.pallas_call(
        functools.partial(_scan_kernel, L=L),
        grid_spec=pltpu.PrefetchScalarGridSpec(
            num_scalar_prefetch=7,
            grid=(n_steps,),
            in_specs=[fwd(1536), bwd(1536), fwd(1536), bwd(1536), fwd(256), bwd(256),
                      st_in4, st_in3, st_in3, st_in4,
                      pl.BlockSpec((nu, HEAD_W), lambda s, *_: (0, 0))],
            out_specs=[fwd(1024), bwd(1024), st_out4, st_out3, st_out3, st_out4],
            scratch_shapes=[pltpu.VMEM((nu, HEAD_W, 2 * HEAD_W), F32),
                            pltpu.VMEM((nu, HEAD_W, HEAD_W), F32),
                            pltpu.VMEM((nu, HEAD_W), F32),
                            pltpu.VMEM((nu, L, L), F32),
                            pltpu.VMEM((nu, L, HEAD_W), F32),
                            pltpu.VMEM((nu, L), F32),
                            pltpu.VMEM((nu, HEAD_W), F32)]),
        out_shape=[jax.ShapeDtypeStruct((n, 1024), BF16),
                   jax.ShapeDtypeStruct((n, 1024), BF16),
                   jax.ShapeDtypeStruct((nps, nu, HEAD_W, HEAD_W), F32),
                   jax.ShapeDtypeStruct((nps, nu, HEAD_W), F32),
                   jax.ShapeDtypeStruct((nps, nu, HEAD_W), F32),
                   jax.ShapeDtypeStruct((nps, nu, HEAD_W, HEAD_W), F32)],
        compiler_params=_params(1),
        name="scan_even",
    )(*tables, qkva, qkva, qkvb, qkvb, gates, gates, c0, n0, m0, s0, ret_logit)


def _proj_odd_kernel(x_ref, mod_ref, gain_ref, w_ref, e_ref, qg_ref, kg_ref, cos_ref, sin_ref,
                     q_ref, k_ref, v_ref, *, rope):
    shift = mod_ref[0, 0:1, :]
    scale = mod_ref[0, 1:2, :]
    hb = (_rms(x_ref[...]) * gain_ref[...] * (1.0 + scale) + shift).astype(BF16)

    def qk_norm(raw, g):
        ss = _dot((raw * raw).astype(BF16), e_ref[...])
        return raw * lax.rsqrt(ss * (1.0 / DH_C) + EPS) * g

    def rotate(y):
        if not rope:
            return y
        lane = lax.broadcasted_iota(jnp.int32, y.shape, 1)
        first_half = (lane & 31) < 16
        partner = jnp.where(first_half, pltpu.roll(y, y.shape[1] - 16, axis=1), pltpu.roll(y, 16, axis=1))
        cos = jnp.concatenate([cos_ref[...]] * (y.shape[1] // HEAD_W), axis=1)
        sin = jnp.concatenate([sin_ref[...]] * (y.shape[1] // HEAD_W), axis=1)
        return y * cos + partner * sin

    q = rotate(qk_norm(_dot(hb, w_ref[:, 0:1024]), qg_ref[...]))
    q_ref[...] = (q * (DH_C ** -0.5 * math.log2(math.e))).astype(q_ref.dtype)
    k = rotate(qk_norm(_dot(hb, w_ref[:, 1024:2048]), kg_ref[...]))
    k_ref[...] = k.astype(k_ref.dtype)
    v_ref[...] = _dot(hb, w_ref[:, 2048:3072]).astype(v_ref.dtype)


def _proj_odd(x, mods, gain, w_qkv, e64, q_gain, k_gain, cos, sin, rows, *, sample):
    d = x.shape[1]
    tm = ROW_TILE
    if sample:
        n, base, kv_dtype = rows.n_sample, rows.n_prompt // tm, BF16
        per_seq = rows.sample_len // tm
        table = lambda i: (i % per_seq, 0)
    else:
        n, base, kv_dtype = rows.n_prompt, 0, F32
        table = lambda i: (0, 0)
    return pl.pallas_call(
        functools.partial(_proj_odd_kernel, rope=sample),
        grid=(n // tm,),
        in_specs=[pl.BlockSpec((tm, d), lambda i: (base + i, 0)),
                  pl.BlockSpec((1, 6, d), lambda i: (rows.cond_of_tile(base + i, tm), 0, 0)),
                  pl.BlockSpec((1, d), lambda i: (0, 0)),
                  pl.BlockSpec(w_qkv.shape, lambda i: (0, 0)),
                  pl.BlockSpec(e64.shape, lambda i: (0, 0)),
                  pl.BlockSpec((1, d), lambda i: (0, 0)),
                  pl.BlockSpec((1, d), lambda i: (0, 0)),
                  pl.BlockSpec((tm, HEAD_W), table),
                  pl.BlockSpec((tm, HEAD_W), table)],
        out_specs=[pl.BlockSpec((tm, d), lambda i: (i, 0))] * 3,
        out_shape=[jax.ShapeDtypeStruct((n, d), BF16),
                   jax.ShapeDtypeStruct((n, d), kv_dtype),
                   jax.ShapeDtypeStruct((n, d), kv_dtype)],
        compiler_params=_params(1),
        name="proj_odd_sample" if sample else "proj_odd_prompt",
    )(x, mods, gain, w_qkv, e64, q_gain, k_gain, cos, sin)


def _rope_tables(sample_len):
    t = np.arange(sample_len)
    nf = DH_C // 4
    inv = ROPE_BASE ** (-np.arange(nf, dtype=np.float32) / nf)
    row = (t // GRID_W).astype(np.float32)[:, None] * inv[None, :]
    col = (t % GRID_W).astype(np.float32)[:, None] * inv[None, :]
    cos64 = np.concatenate([np.cos(row), np.cos(row), np.cos(col), np.cos(col)], axis=1)
    sin64 = np.concatenate([-np.sin(row), np.sin(row), -np.sin(col), np.sin(col)], axis=1)
    cos = np.concatenate([cos64, cos64], axis=1).astype(np.float32)
    sin = np.concatenate([sin64, sin64], axis=1).astype(np.float32)
    return jnp.asarray(cos), jnp.asarray(sin)


def _attn_kernel(*refs, tq, tk, n_main, n_cache, lam_init):
    if n_cache:
        q_ref, k_ref, v_ref, ck_ref, cv_ref, lq_ref, og_ref, o_ref, vt_sc = refs
    else:
        q_ref, k_ref, v_ref, lq_ref, og_ref, o_ref, vt_sc = refs
    fill_w = 256

    @pl.when(pl.program_id(2) == 0)
    def _():
        ones_rows = jnp.where(lax.broadcasted_iota(jnp.int32, (8, fill_w), 0) == 0, 1.0, 0.0).astype(BF16)
        for j in range(n_main // fill_w):
            vt = v_ref[0, j * fill_w:(j + 1) * fill_w, :].astype(F32).T.astype(BF16)
            vt_sc[:, j * fill_w:(j + 1) * fill_w] = jnp.concatenate([vt, ones_rows], axis=0)
        if n_cache:
            vt = cv_ref[0].astype(F32).T.astype(BF16)
            vt_sc[:, n_main:n_main + n_cache] = jnp.concatenate([vt, ones_rows], axis=0)

    qt = q_ref[0].astype(F32).T
    row = lax.broadcasted_iota(jnp.int32, qt.shape, 0)
    qst = jnp.concatenate([jnp.where(row < DH_C, qt, 0.0), jnp.where(row >= DH_C, qt, 0.0)],
                          axis=1).astype(BF16)

    def scores(kt):
        return _dot(kt.astype(BF16), qst)

    def accumulate(carry, s, vt):
        m, acc = carry
        m_new = jnp.maximum(m, jnp.max(s, axis=0, keepdims=True))
        p = jnp.exp2(s - m_new).astype(BF16)
        acc = jnp.exp2(m - m_new) * acc + _dot(vt, p)
        return m_new, acc

    tiles = [(k_ref, j * tk, tk, j * tk) for j in range(n_main // tk)]
    if n_cache:
        tiles = [(ck_ref, 0, n_cache, n_main)] + tiles
    carry = (jnp.full((1, 2 * tq), -jnp.inf, F32), jnp.zeros((HEAD_W + 8, 2 * tq), F32))
    ref0, r0, n0, _ = tiles[0]
    s_next = scores(ref0[0, r0:r0 + n0, :])
    for j, (_, _, n_keys, c0) in enumerate(tiles):
        s_cur = s_next
        if j + 1 < len(tiles):
            ref1, r1, n1, _ = tiles[j + 1]
            s_next = scores(ref1[0, r1:r1 + n1, :])
        carry = accumulate(carry, s_cur, vt_sc[:, c0:c0 + n_keys])
    _, acc = carry
    o_t = acc[0:HEAD_W, :] / acc[HEAD_W:HEAD_W + 1, :]
    lq = lq_ref[...]
    lam = (jnp.exp(jnp.sum(lq[0:1, :] * lq[1:2, :], axis=1, keepdims=True))
           - jnp.exp(jnp.sum(lq[2:3, :] * lq[3:4, :], axis=1, keepdims=True)) + lam_init)
    o = (o_t[:, 0:tq] - lam * o_t[:, tq:2 * tq]).T
    o_ref[0] = (_rms(o) * og_ref[...] * (1.0 - lam_init)).astype(o_ref.dtype)


def _attention(q, k, v, cache_k, cache_v, lam_qk, out_gain, lam_init):
    b, t, d = q.shape
    tq = min(ATTN_TQ, t)
    tk = min(ATTN_TK, t)
    n_cache = 0 if cache_k is None else cache_k.shape[1]
    assert t % tk == 0 and t % tq == 0 and t % 256 == 0 and n_cache in (0, 256)
    seq = lambda n: pl.BlockSpec((1, n, HEAD_W), lambda bi, h, qi: (bi, 0, h))
    in_specs = [pl.BlockSpec((1, tq, HEAD_W), lambda bi, h, qi: (bi, qi, h)), seq(t), seq(t)]
    args = [q, k, v]
    if n_cache:
        in_specs += [seq(n_cache), seq(n_cache)]
        args += [cache_k, cache_v]
    in_specs += [pl.BlockSpec(lam_qk.shape, lambda bi, h, qi: (0, 0)),
                 pl.BlockSpec(out_gain.shape, lambda bi, h, qi: (0, 0))]
    args += [lam_qk, out_gain]
    return pl.pallas_call(
        functools.partial(_attn_kernel, tq=tq, tk=tk, n_main=t, n_cache=n_cache, lam_init=lam_init),
        grid=(b, H_C, t // tq),
        in_specs=in_specs,
        out_specs=pl.BlockSpec((1, tq, HEAD_W), lambda bi, h, qi: (bi, qi, h)),
        out_shape=jax.ShapeDtypeStruct((b, t, d), BF16),
        scratch_shapes=[pltpu.VMEM((HEAD_W + 8, t + n_cache), BF16)],
        compiler_params=_params(3),
        name="diff_attention",
    )(*args)


def _out_kernel(*refs, n_tiles, **static):
    ys0_ref = refs[-4]
    ys0_ref[...] = jnp.zeros_like(ys0_ref)

    @pl.when(pl.program_id(0) < n_tiles)
    def _():
        _out_tile(*refs, **static)


def _out_tile(*refs, tm, even, n_prompt_tiles):
    if even:
        (hf_ref, hb_ref, oz_ref, ga_ref, gb_ref, x_ref, mod_ref, w_ref, fg_ref, wr_ref, rb_ref,
         x1_ref, xs_ref, ys0_ref, sl_ref, gw_ref, cnt_ref) = refs
    else:
        (op_ref, os_ref, x_ref, mod_ref, w_ref, fg_ref, wr_ref, rb_ref,
         x1_ref, xs_ref, ys0_ref, sl_ref, gw_ref, cnt_ref) = refs
    del ys0_ref
    i = pl.program_id(0)

    if even:
        hs = hf_ref[...].astype(F32) + hb_ref[...].astype(F32)
        oz = oz_ref[...].astype(F32)
        parts = []
        for h in range(H_A + H_B):
            cs = slice(h * HEAD_W, (h + 1) * HEAD_W)
            gain = ga_ref[:, cs] if h < H_A else gb_ref[:, (h - H_A) * HEAD_W:(h - H_A + 1) * HEAD_W]
            act = _sigmoid(oz[:, cs]) if h < H_A else _silu(oz[:, cs])
            parts.append((_rms(hs[:, cs]) * gain * act).astype(BF16))
        y_in = jnp.concatenate(parts, axis=1)
    else:
        y_in = jnp.where(i < n_prompt_tiles, op_ref[...], os_ref[...])

    x1 = x_ref[...] + mod_ref[0, 2:3, :] * _dot(y_in, w_ref[...])
    x1_ref[...] = x1
    h2 = _rms(x1) * fg_ref[...] * (1.0 + mod_ref[0, 4:5, :]) + mod_ref[0, 3:4, :]
    h2b = h2.astype(BF16)

    score = _sigmoid(_dot(h2b, wr_ref[...]))
    st = score.T[0:N_EXPERTS, :]
    sel = st + rb_ref[:, 0:1]
    srow = [st[e:e + 1, :] for e in range(N_EXPERTS)]
    brow = [sel[e:e + 1, :] for e in range(N_EXPERTS)]
    epg = EXPERTS_PER_GROUP
    gscore = []
    for g in range(N_GROUPS):
        a = brow[g * epg:(g + 1) * epg]
        best = None
        for p in range(epg):
            for r in range(p + 1, epg):
                pair = a[p] + a[r]
                best = pair if best is None else jnp.maximum(best, pair)
        gscore.append(best)
    gbest, gidx = gscore[0], jnp.zeros((1, tm), jnp.int32)
    for g in range(1, N_GROUPS):
        better = gscore[g] > gbest
        gbest = jnp.where(better, gscore[g], gbest)
        gidx = jnp.where(better, g, gidx)
    vals, sig = [], []
    for p in range(epg):
        vp, sp = brow[p], srow[p]
        for g in range(1, N_GROUPS):
            vp = jnp.where(gidx == g, brow[g * epg + p], vp)
            sp = jnp.where(gidx == g, srow[g * epg + p], sp)
        vals.append(vp)
        sig.append(sp)
    v1, i1, w1 = vals[0], jnp.zeros((1, tm), jnp.int32), sig[0]
    for p in range(1, epg):
        better = vals[p] > v1
        v1 = jnp.where(better, vals[p], v1)
        i1 = jnp.where(better, p, i1)
        w1 = jnp.where(better, sig[p], w1)
    v2 = jnp.full((1, tm), -jnp.inf, F32)
    i2 = jnp.zeros((1, tm), jnp.int32)
    w2 = jnp.zeros((1, tm), F32)
    for p in range(epg):
        better = jnp.logical_and(i1 != p, vals[p] > v2)
        v2 = jnp.where(better, vals[p], v2)
        i2 = jnp.where(better, p, i2)
        w2 = jnp.where(better, sig[p], w2)
    e1 = gidx * epg + i1
    e2 = gidx * epg + i2
    wsum = w1 + w2

    erow = lax.broadcasted_iota(jnp.int32, (N_EXPERTS, tm), 0)
    earlier = jnp.where(lax.broadcasted_iota(jnp.int32, (tm, tm), 0)
                        < lax.broadcasted_iota(jnp.int32, (tm, tm), 1), 1.0, 0.0).astype(BF16)
    oh1 = jnp.where(erow == e1, 1.0, 0.0)
    oh2 = jnp.where(erow == e2, 1.0, 0.0)
    before1 = _dot(oh1.astype(BF16), earlier)
    before2 = _dot(oh2.astype(BF16), earlier)
    cnt1 = jnp.sum(oh1, axis=1, keepdims=True)
    cnt = cnt1 + jnp.sum(oh2, axis=1, keepdims=True)
    padded = jnp.floor((cnt + (MOE_CHUNK - 1.0)) * (1.0 / MOE_CHUNK)) * MOE_CHUNK
    below = jnp.where(lax.broadcasted_iota(jnp.int32, (N_EXPERTS, N_EXPERTS), 1)
                      < lax.broadcasted_iota(jnp.int32, (N_EXPERTS, N_EXPERTS), 0), 1.0, 0.0)
    start = _dot(below, jnp.broadcast_to(padded, (N_EXPERTS, HEAD_W)), precision=_HI)[:, 0:1]
    slot1 = jnp.sum(oh1 * (start + before1), axis=0, keepdims=True).astype(jnp.int32)
    slot2 = jnp.sum(oh2 * (start + cnt1 + before2), axis=0, keepdims=True).astype(jnp.int32)
    local = lax.broadcasted_iota(jnp.int32, (MOE_TILE_ROWS, tm), 0)
    place = jnp.where(jnp.logical_or(local == slot1, local == slot2), 1.0, 0.0).astype(BF16)
    xs_ref[...] = _dot(place, h2b).astype(BF16)
    zi = jnp.zeros((6, tm), jnp.int32)
    sl_ref[...] = jnp.concatenate([slot1, slot2, zi], axis=0)
    gw_ref[...] = jnp.concatenate([w1 / wsum, w2 / wsum, jnp.zeros((6, tm), F32)], axis=0)
    cnt_ref[...] = jnp.broadcast_to(cnt, (N_EXPERTS, HEAD_W))


def _out_and_route(mix_in, x, mods, w_out, ffn_gain, w_router, router_bias, rows, *, even):
    n, d = x.shape
    tm = ROW_TILE
    n_tiles = n // tm
    n_pt = rows.n_prompt // tm
    tile = lambda i: jnp.minimum(i, n_tiles - 1)
    row = lambda w: pl.BlockSpec((tm, w), lambda i: (tile(i), 0))
    full = lambda a: pl.BlockSpec(a.shape, lambda i: (0,) * a.ndim)
    if even:
        hf, hb, oz, gain_a, gain_b = mix_in
        in_specs = [row(1024), row(1024), row(1024), full(gain_a), full(gain_b)]
        args = [hf, hb, oz, gain_a, gain_b]
    else:
        o_p, o_s = mix_in
        in_specs = [pl.BlockSpec((tm, d), lambda i: (jnp.minimum(i, n_pt - 1), 0)),
                    pl.BlockSpec((tm, d), lambda i: (jnp.maximum(tile(i) - n_pt, 0), 0))]
        args = [o_p, o_s]
    in_specs += [row(d), pl.BlockSpec((1, 6, d), lambda i: (rows.cond_of_tile(tile(i), tm), 0, 0)),
                 full(w_out), full(ffn_gain), full(w_router), full(router_bias)]
    args += [x, mods, w_out, ffn_gain, w_router, router_bias]
    col = lambda: pl.BlockSpec((8, tm), lambda i: (0, tile(i)))
    return pl.pallas_call(
        functools.partial(_out_kernel, n_tiles=n_tiles, tm=tm, even=even, n_prompt_tiles=n_pt),
        grid=(n_tiles + MOE_SCRATCH_TILES,),
        in_specs=in_specs,
        out_specs=[row(d), pl.BlockSpec((MOE_TILE_ROWS, d), lambda i: (tile(i), 0)),
                   pl.BlockSpec((MOE_TILE_ROWS, d), lambda i: (i, 0)), col(), col(),
                   pl.BlockSpec((N_EXPERTS, HEAD_W), lambda i: (tile(i), 0))],
        out_shape=[jax.ShapeDtypeStruct((n, d), F32),
                   jax.ShapeDtypeStruct((n_tiles * MOE_TILE_ROWS, d), BF16),
                   jax.ShapeDtypeStruct(((n_tiles + MOE_SCRATCH_TILES) * MOE_TILE_ROWS, d), BF16),
                   jax.ShapeDtypeStruct((8, n), jnp.int32),
                   jax.ShapeDtypeStruct((8, n), F32),
                   jax.ShapeDtypeStruct((n_tiles * N_EXPERTS, HEAD_W), F32)],
        compiler_params=_params(1),
        name="out_even" if even else "out_odd",
    )(*args)


def _moe_kernel(cin_ref, cout_ref, be_ref, nu_ref, xs_hbm, ys_init_hbm, wg_ref, wu_ref, wd_ref, ys_hbm,
                xbuf, ybuf, wg_sc, wu_sc, wd_sc, in_sem, out_sem):
    del ys_init_hbm
    b = pl.program_id(0)
    n_used = nu_ref[0]
    cb, ch = MOE_BLOCK_CHUNKS, MOE_CHUNK

    def in_copy(blk, slot, c):
        src = pl.multiple_of(cin_ref[blk * cb + c] * ch, ch)
        return pltpu.make_async_copy(xs_hbm.at[pl.ds(src, ch), :], xbuf.at[slot, pl.ds(c * ch, ch), :],
                                     in_sem.at[slot])

    def out_copy(blk, slot, c):
        dst = pl.multiple_of(cout_ref[blk * cb + c] * ch, ch)
        return pltpu.make_async_copy(ybuf.at[slot, pl.ds(c * ch, ch), :], ys_hbm.at[pl.ds(dst, ch), :],
                                     out_sem.at[slot])

    def start_in(blk, slot):
        for c in range(cb):
            in_copy(blk, slot, c).start()

    def wait_out(blk, slot):
        for c in range(cb):
            out_copy(blk, slot, c).wait()

    slot = b % 2

    @pl.when(b == 0)
    def _():
        start_in(0, 0)

    @pl.when(b + 1 < n_used)
    def _():
        start_in(b + 1, 1 - slot)

    @pl.when(b < n_used)
    def _():
        for c in range(cb):
            in_copy(b, slot, c).wait()

        @pl.when(b >= 2)
        def _():
            wait_out(b - 2, slot)

        @pl.when(jnp.logical_or(b == 0, be_ref[b] != be_ref[jnp.maximum(b - 1, 0)]))
        def _():
            wg_sc[...] = wg_ref[0, 0].astype(BF16)
            wu_sc[...] = wu_ref[0, 0].astype(BF16)
            wd_sc[...] = wd_ref[0, 0].astype(BF16)

        x = xbuf[slot]
        a = (_silu(_dot(x, wg_sc[...])) * _dot(x, wu_sc[...])).astype(BF16)
        ybuf[slot] = _dot(a, wd_sc[...]).astype(BF16)
        for c in range(cb):
            out_copy(b, slot, c).start()

    @pl.when(b == pl.num_programs(0) - 1)
    def _():
        @pl.when(n_used >= 2)
        def _():
            wait_out(n_used - 2, n_used % 2)

        wait_out(n_used - 1, (n_used - 1) % 2)


def _moe_experts(xs_local, ys_init, chunk_in, chunk_out, block_e, n_used, w_gate, w_up, w_down, layer):
    d = xs_local.shape[1]
    n_blocks = chunk_in.shape[0] // MOE_BLOCK_CHUNKS
    wspec = pl.BlockSpec((1, 1, d, d), lambda i, ci, co, be, nu: (layer, be[i], 0, 0))
    hbm = pl.BlockSpec(memory_space=pl.ANY)
    return pl.pallas_call(
        _moe_kernel,
        grid_spec=pltpu.PrefetchScalarGridSpec(
            num_scalar_prefetch=4,
            grid=(n_blocks,),
            in_specs=[hbm, hbm, wspec, wspec, wspec],
            out_specs=hbm,
            scratch_shapes=[pltpu.VMEM((2, MOE_BLOCK, d), BF16), pltpu.VMEM((2, MOE_BLOCK, d), BF16),
                            pltpu.VMEM((d, d), BF16), pltpu.VMEM((d, d), BF16), pltpu.VMEM((d, d), BF16),
                            pltpu.SemaphoreType.DMA((2,)), pltpu.SemaphoreType.DMA((2,))]),
        out_shape=jax.ShapeDtypeStruct(ys_init.shape, BF16),
        input_output_aliases={5: 0},
        compiler_params=_params(1),
        name="moe_experts",
    )(chunk_in, chunk_out, block_e, n_used, xs_local, ys_init, w_gate, w_up, w_down)


def _combine_kernel(x_ref, ys_ref, sl_ref, gw_ref, mod_ref, o_ref):
    sl = sl_ref[...]
    w = gw_ref[...]
    ys = ys_ref[...]
    local = lax.broadcasted_iota(jnp.int32, (sl.shape[0], ys.shape[0]), 1)
    pick = (jnp.where(local == sl[:, 0:1], w[:, 0:1], 0.0)
            + jnp.where(local == sl[:, 1:2], w[:, 1:2], 0.0)).astype(BF16)
    o_ref[...] = x_ref[...] + mod_ref[0, 5:6, :] * _dot(pick, ys)


def _combine(x1, ys_local, slot_cols, gw_cols, mods, rows):
    n, d = x1.shape
    tm = ROW_TILE
    return pl.pallas_call(
        _combine_kernel,
        grid=(n // tm,),
        in_specs=[pl.BlockSpec((tm, d), lambda i: (i, 0)),
                  pl.BlockSpec((MOE_TILE_ROWS, d), lambda i: (i, 0)),
                  pl.BlockSpec((tm, 2), lambda i: (i, 0)),
                  pl.BlockSpec((tm, 2), lambda i: (i, 0)),
                  pl.BlockSpec((1, 6, d), lambda i: (rows.cond_of_tile(i, tm), 0, 0))],
        out_specs=pl.BlockSpec((tm, d), lambda i: (i, 0)),
        out_shape=jax.ShapeDtypeStruct((n, d), F32),
        compiler_params=_params(1),
        name="moe_combine",
    )(x1, ys_local, slot_cols, gw_cols, mods)


def _chunk_tables(counts):
    n_tiles = counts.shape[0]
    cb, tc = MOE_BLOCK_CHUNKS, MOE_TILE_CHUNKS
    n_list = (n_tiles * (tc - 1) + N_EXPERTS * (cb - 1) + cb - 1) // cb * cb
    nch = (counts + MOE_CHUNK - 1) // MOE_CHUNK
    first = jnp.cumsum(nch, axis=1) - nch
    total = jnp.sum(nch, axis=0)
    padded = (total + cb - 1) // cb * cb
    e_end = jnp.cumsum(padded)
    e_start = e_end - padded
    before = (jnp.cumsum(nch, axis=0) - nch).T
    seg_start = (e_start[:, None] + before).reshape(-1)
    seg_len = nch.T.reshape(-1)
    seg_first = (jnp.arange(n_tiles, dtype=jnp.int32)[None, :] * tc + first.T).reshape(-1)
    j = jnp.arange(n_list, dtype=jnp.int32)
    seg = jnp.clip(jnp.sum((seg_start[None, :] <= j[:, None]).astype(jnp.int32), axis=1) - 1,
                   0, seg_start.shape[0] - 1)
    e_j = seg // n_tiles
    seg_off = j - seg_start[seg]
    valid = jnp.logical_and(seg_off < seg_len[seg], j < e_end[-1])
    src = seg_first[seg] + seg_off
    off = j - e_start[e_j]
    scratch = n_tiles * tc + e_j * cb + jnp.clip(off - total[e_j], 0, cb - 1)
    chunk_in = jnp.where(valid, src, tc - 1).astype(jnp.int32)
    chunk_out = jnp.where(valid, src, scratch).astype(jnp.int32)
    block_e = e_j[::cb]
    n_used = (e_end[-1] // cb).astype(jnp.int32).reshape(1)
    return chunk_in, chunk_out, block_e, n_used


def _moe_layer(x1, xs_local, ys_zero, slots, gw, counts, mods, w_gate, w_up, w_down, layer, rows):
    n_tiles = x1.shape[0] // ROW_TILE
    counts = counts[:, 0].astype(jnp.int32).reshape(n_tiles, N_EXPERTS)
    chunk_in, chunk_out, block_e, n_used = _chunk_tables(counts)
    ys_local = _moe_experts(xs_local, ys_zero, chunk_in, chunk_out, block_e, n_used, w_gate, w_up, w_down, layer)
    return _combine(x1, ys_local, slots[0:2].T, gw[0:2].T, mods, rows)


def kernel(x_prompt, x_sample, c, state_mlstm_C, state_mlstm_n, state_mlstm_m, state_ret_S, cache_k, cache_v, c_ctx, w_ada, b_ada, norm_mix_gain, norm_ffn_gain, w_in_even, mlstm_conv, mlstm_gate_bias, mlstm_out_gain, ret_decay_logit, ret_out_gain, w_out_even, w_qkv_odd, q_norm_gain, k_norm_gain, lambda_qk, attn_out_gain, w_out_odd, w_router, router_bias, moe_w_gate, moe_w_up, moe_w_down):
    bp, seq, d = x_prompt.shape
    bs, dec_seq, _ = x_sample.shape
    depth = w_ada.shape[0]
    past = cache_k.shape[2]
    rows = _Rows(bp, seq, bs, dec_seq)
    nu = N_DIR * H_A
    assert 1 + bs <= N_COND_PAD

    x = jnp.concatenate([x_prompt.reshape(bp * seq, d), x_sample.reshape(bs * dec_seq, d)], axis=0)
    cond = jnp.concatenate([c_ctx[None, :], c, jnp.zeros((N_COND_PAD - 1 - bs, d), F32)], axis=0)
    mods_all = _modulation_all(cond, w_ada, b_ada).reshape(depth, N_COND_PAD, 6, d)

    w_router_pad = jnp.pad(w_router, ((0, 0), (0, HEAD_W - N_EXPERTS))).astype(BF16)
    router_bias_col = jnp.broadcast_to(router_bias[:, None], (N_EXPERTS, HEAD_W))
    e64 = jnp.asarray(np.kron(np.eye(d // DH_C, dtype=np.float32), np.ones((DH_C, DH_C), np.float32)), BF16)
    cos, sin = _rope_tables(dec_seq)

    st_c, st_n, st_m, st_s, st_k, st_v = [], [], [], [], [], []
    for l in range(depth):
        j = l // 2
        mods = mods_all[l]
        gain_mix = norm_mix_gain[l][None, :]
        if l % 2 == 0:
            w = w_in_even[j]
            wa = 4 * HEAD_W * 4
            gcols = 4 * H_A
            ob = wa + gcols
            w_main = jnp.concatenate([w[:, 0:1536], w[:, ob:ob + 1536], w[:, 1536:2048],
                                      w[:, ob + 1536:ob + 2048]], axis=1).astype(BF16)
            w_g = w[:, wa:wa + gcols]
            zpad = jnp.zeros((d, HEAD_W - nu), F32)
            w_g32 = jnp.concatenate([w_g[:, 0:nu], zpad, w_g[:, nu:2 * nu], zpad], axis=1)
            w_g_hi = w_g32.astype(BF16)
            w_gates = jnp.concatenate([w_g_hi, (w_g32 - w_g_hi.astype(F32)).astype(BF16)], axis=1)
            gb = mlstm_gate_bias[j].reshape(2, nu)
            zb = jnp.zeros((HEAD_W - nu,), F32)
            gate_bias = jnp.concatenate([gb[0], zb, gb[1], zb])[None, :]
            qkva, qkvb, oz, gates = _proj_even(x, mods, gain_mix, w_main, w_gates, mlstm_conv[j], gate_bias, rows)
            c0 = state_mlstm_C[:, j].reshape(bs, nu, HEAD_W, HEAD_W)
            n0 = state_mlstm_n[:, j].reshape(bs, nu, HEAD_W)
            m0 = jnp.broadcast_to(state_mlstm_m[:, j].reshape(bs, nu, 1), (bs, nu, HEAD_W))
            s0 = state_ret_S[:, j].reshape(bs, nu, HEAD_W, HEAD_W)
            ret_logit = jnp.broadcast_to(ret_decay_logit[j].reshape(nu, 1), (nu, HEAD_W))
            hf, hb, cn, nn, mn, sn = _scan_even(qkva, qkvb, gates, c0, n0, m0, s0, ret_logit, rows)
            st_c.append(cn.reshape(bp, N_DIR, H_A, HEAD_W, HEAD_W))
            st_n.append(nn.reshape(bp, N_DIR, H_A, HEAD_W))
            st_m.append(mn[:, :, 0].reshape(bp, N_DIR, H_A))
            st_s.append(sn.reshape(bp, N_DIR, H_B, HEAD_W, HEAD_W))
            mix_in = (hf, hb, oz, mlstm_out_gain[j].reshape(1, H_A * HEAD_W),
                      ret_out_gain[j].reshape(1, H_B * HEAD_W))
            w_out = w_out_even[j].astype(BF16)
        else:
            lam_init = 0.8 - 0.6 * math.exp(-0.3 * l)
            w_qkv = w_qkv_odd[j].astype(BF16)
            qg = jnp.tile(q_norm_gain[j], d // DH_C)[None, :]
            kg = jnp.tile(k_norm_gain[j], d // DH_C)[None, :]
            og = attn_out_gain[j][None, :]
            q_p, k_p, v_p = _proj_odd(x, mods, gain_mix, w_qkv, e64, qg, kg, cos, sin, rows, sample=False)
            q_s, k_s, v_s = _proj_odd(x, mods, gain_mix, w_qkv, e64, qg, kg, cos, sin, rows, sample=True)
            o_p = _attention(q_p.reshape(bp, seq, d), k_p.reshape(bp, seq, d), v_p.reshape(bp, seq, d),
                             None, None, lambda_qk[j], og, lam_init)
            o_s = _attention(q_s.reshape(bs, dec_seq, d), k_s.reshape(bs, dec_seq, d), v_s.reshape(bs, dec_seq, d),
                             cache_k[:, j].reshape(bs, past, d), cache_v[:, j].reshape(bs, past, d),
                             lambda_qk[j], og, lam_init)
            st_k.append(k_p.reshape(bp, seq, H_C, 2, DH_C))
            st_v.append(v_p.reshape(bp, seq, H_C, 2 * DH_C))
            mix_in = (o_p.reshape(bp * seq, d), o_s.reshape(bs * dec_seq, d))
            w_out = w_out_odd[j].astype(BF16)
        x1, xs_local, ys_zero, slots, gw, counts = _out_and_route(
            mix_in, x, mods, w_out, norm_ffn_gain[l][None, :], w_router_pad, router_bias_col, rows,
            even=(l % 2 == 0))
        x = _moe_layer(x1, xs_local, ys_zero, slots, gw, counts, mods, moe_w_gate, moe_w_up, moe_w_down, l, rows)

    dt = x_prompt.dtype
    y_prompt = x[:rows.n_prompt].reshape(bp, seq, d)
    y_sample = x[rows.n_prompt:].reshape(bs, dec_seq, d)
    return (y_prompt, y_sample,
            jnp.stack(st_c, axis=1).astype(dt), jnp.stack(st_n, axis=1).astype(dt),
            jnp.stack(st_m, axis=1).astype(dt), jnp.stack(st_s, axis=1).astype(dt),
            jnp.stack(st_k, axis=1).astype(dt), jnp.stack(st_v, axis=1).astype(dt))
```

```python
import functools
import math

import numpy as np
import jax
import jax.numpy as jnp
from jax import lax
from jax.experimental import pallas as pl
from jax.experimental.pallas import tpu as pltpu

F32 = jnp.float32
BF16 = jnp.bfloat16

EPS = 1e-6
GRID_W = 64
ROPE_BASE = 10000.0
H_A = 4
H_B = 4
H_C = 8
N_DIR = 2
N_EXPERTS = 16
N_GROUPS = 4
EXPERTS_PER_GROUP = N_EXPERTS // N_GROUPS
HEAD_W = 128
DH_C = 64
N_COND_PAD = 16

ROW_TILE = 256
SCAN_CHUNK = 256
ATTN_TQ = 1024
ATTN_TK = 512
MOE_BLOCK = 512
MOE_CHUNK = 16
MOE_BLOCK_CHUNKS = MOE_BLOCK // MOE_CHUNK
MOE_TILE_CHUNKS = (2 * ROW_TILE + N_EXPERTS * (MOE_CHUNK - 1)) // MOE_CHUNK + 1
MOE_TILE_ROWS = MOE_TILE_CHUNKS * MOE_CHUNK
MOE_SCRATCH_TILES = -(-N_EXPERTS * MOE_BLOCK_CHUNKS // MOE_TILE_CHUNKS)
VMEM_LIMIT = 56 * 1024 * 1024

_HI = lax.Precision.HIGHEST


def _dot(a, b, precision=None):
    return jnp.dot(a, b, preferred_element_type=F32, precision=precision)


def _dot_nt(a, b):
    return lax.dot_general(a, b, (((1,), (1,)), ((), ())), preferred_element_type=F32)


def _dot_tn(a, b):
    return lax.dot_general(a, b, (((0,), (0,)), ((), ())), preferred_element_type=F32)


def _rms(x):
    return x * lax.rsqrt(jnp.mean(x * x, axis=-1, keepdims=True) + EPS)


def _sigmoid(x):
    return 1.0 / (1.0 + jnp.exp(-x))


def _silu(x):
    return x * _sigmoid(x)


def _log_sigmoid(x):
    return jnp.minimum(x, 0.0) - jnp.log1p(jnp.exp(-jnp.abs(x)))


def _params(n_axes):
    return pltpu.CompilerParams(dimension_semantics=("arbitrary",) * n_axes,
                                vmem_limit_bytes=VMEM_LIMIT)


def _mod_kernel(cond_ref, w_ref, b_ref, o_ref):
    s = _silu(cond_ref[...]).astype(BF16)
    o_ref[0] = _dot(s, w_ref[0].astype(BF16)) + b_ref[0]


def _modulation_all(cond, w_ada, b_ada):
    depth, d, n = w_ada.shape
    tn = n // 4
    return pl.pallas_call(
        _mod_kernel,
        grid=(depth, n // tn),
        in_specs=[pl.BlockSpec((N_COND_PAD, d), lambda l, j: (0, 0)),
                  pl.BlockSpec((1, d, tn), lambda l, j: (l, 0, j)),
                  pl.BlockSpec((1, 1, tn), lambda l, j: (l, 0, j))],
        out_specs=pl.BlockSpec((1, N_COND_PAD, tn), lambda l, j: (l, 0, j)),
        out_shape=jax.ShapeDtypeStruct((depth, N_COND_PAD, n), F32),
        compiler_params=_params(2),
        name="adaln_modulation",
    )(cond, w_ada, b_ada.reshape(depth, 1, n))


class _Rows:
    def __init__(self, n_prompt_seq, prompt_len, n_sample_seq, sample_len):
        self.prompt_len = prompt_len
        self.sample_len = sample_len
        self.n_prompt_seq = n_prompt_seq
        self.n_sample_seq = n_sample_seq
        self.n_prompt = n_prompt_seq * prompt_len
        self.n_sample = n_sample_seq * sample_len
        self.total = self.n_prompt + self.n_sample
        assert prompt_len % ROW_TILE == 0 and sample_len % ROW_TILE == 0
        assert self.n_prompt % sample_len == 0 or self.n_sample == 0

    def cond_of_tile(self, i, tile):
        n_p = self.n_prompt // tile
        per_seq = self.sample_len // tile
        return jnp.where(i < n_p, 0, 1 + (i - n_p) // per_seq)


def _proj_even_kernel(x_ref, xp_ref, xn_ref, mod_ref, gain_ref, w_ref, wg_ref, cw_ref, gb_ref,
                      qkva_ref, qkvb_ref, oz_ref, g_ref, *, tm, n_prompt, prompt_len, sample_len):
    i = pl.program_id(0)
    shift = mod_ref[0, 0:1, :]
    scale = mod_ref[0, 1:2, :]
    gain = gain_ref[...]

    def modulated(x):
        return _rms(x) * gain * (1.0 + scale) + shift

    h = modulated(x_ref[...])
    hb = h.astype(BF16)
    halo = jnp.concatenate([xp_ref[0], xn_ref[0]], axis=0)
    hh = modulated(halo).astype(BF16)

    w_qk = w_ref[:, 0:1024]
    qk = _dot(hb, w_qk)
    qk_halo = _dot(hh, w_qk)
    prev_row = qk_halo[7:8, :]
    next_row = qk_halo[8:9, :]
    local = lax.broadcasted_iota(jnp.int32, (tm, 1), 0)
    seq_len = jnp.where(i * tm < n_prompt, prompt_len, sample_len)
    pos = (i * tm + local) & (seq_len - 1)
    prev = pltpu.roll(qk, 1, axis=0)
    prev = jnp.where(local == 0, prev_row, prev)
    prev = jnp.where(pos == 0, 0.0, prev)
    nxt = pltpu.roll(qk, tm - 1, axis=0)
    nxt = jnp.where(local == tm - 1, next_row, nxt)
    nxt = jnp.where(pos == seq_len - 1, 0.0, nxt)
    cw = cw_ref[...]
    act = _silu(cw[0:1, :] * prev + cw[1:2, :] * qk + cw[2:3, :] * nxt)
    k_scale = HEAD_W ** -0.5
    qkva_ref[:, 0:512] = act[:, 0:512].astype(BF16)
    qkva_ref[:, 512:1024] = (act[:, 512:1024] * k_scale).astype(BF16)
    qkva_ref[:, 1024:1536] = _dot(hb, w_ref[:, 1024:1536]).astype(BF16)

    qkvb_ref[:, 0:512] = _dot(hb, w_ref[:, 1536:2048]).astype(BF16)
    qkvb_ref[:, 512:1024] = (_dot(hb, w_ref[:, 2048:2560]) * k_scale).astype(BF16)
    qkvb_ref[:, 1024:1536] = _dot(hb, w_ref[:, 2560:3072]).astype(BF16)
    oz_ref[:, 0:512] = _dot(hb, w_ref[:, 3072:3584]).astype(BF16)
    oz_ref[:, 512:1024] = _dot(hb, w_ref[:, 3584:4096]).astype(BF16)

    h_lo = (h - hb.astype(F32)).astype(BF16)
    g_hl = _dot(hb, wg_ref[...])
    gates = (g_hl[:, 0:256] + g_hl[:, 256:512] + _dot(h_lo, wg_ref[:, 0:256])
             + gb_ref[...])
    g_ref[:, 0:128] = gates[:, 0:128]
    g_ref[:, 128:256] = _log_sigmoid(gates[:, 128:256])


def _proj_even(x, mods, gain, w_main, w_gates, conv_w, gate_bias, rows):
    n, d = x.shape
    tm = ROW_TILE
    x8 = x.reshape(n // 8, 8, d)
    nb8 = n // 8
    kern = functools.partial(_proj_even_kernel, tm=tm, n_prompt=rows.n_prompt,
                             prompt_len=rows.prompt_len, sample_len=rows.sample_len)
    return pl.pallas_call(
        kern,
        grid=(n // tm,),
        in_specs=[pl.BlockSpec((tm, d), lambda i: (i, 0)),
                  pl.BlockSpec((1, 8, d), lambda i: (jnp.maximum(i * (tm // 8) - 1, 0), 0, 0)),
                  pl.BlockSpec((1, 8, d), lambda i: (jnp.minimum((i + 1) * (tm // 8), nb8 - 1), 0, 0)),
                  pl.BlockSpec((1, 6, d), lambda i: (rows.cond_of_tile(i, tm), 0, 0)),
                  pl.BlockSpec((1, d), lambda i: (0, 0)),
                  pl.BlockSpec(w_main.shape, lambda i: (0, 0)),
                  pl.BlockSpec(w_gates.shape, lambda i: (0, 0)),
                  pl.BlockSpec(conv_w.shape, lambda i: (0, 0)),
                  pl.BlockSpec(gate_bias.shape, lambda i: (0, 0))],
        out_specs=[pl.BlockSpec((tm, 1536), lambda i: (i, 0)),
                   pl.BlockSpec((tm, 1536), lambda i: (i, 0)),
                   pl.BlockSpec((tm, 1024), lambda i: (i, 0)),
                   pl.BlockSpec((tm, 256), lambda i: (i, 0))],
        out_shape=[jax.ShapeDtypeStruct((n, 1536), BF16),
                   jax.ShapeDtypeStruct((n, 1536), BF16),
                   jax.ShapeDtypeStruct((n, 1024), BF16),
                   jax.ShapeDtypeStruct((n, 256), F32)],
        compiler_params=_params(1),
        name="proj_even",
    )(x, x8, x8, mods, gain, w_main, w_gates, conv_w, gate_bias)


def _scan_kernel(fb_ref, bb_ref, first_ref, last_ref, sin_ref, sout_ref, isp_ref,
                 qaf_ref, qab_ref, qbf_ref, qbb_ref, gf_ref, gb_ref,
                 c0_ref, n0_ref, m0_ref, s0_ref, rl_ref,
                 hf_ref, hb_ref, cn_ref, nn_ref, mn_ref, sn_ref,
                 cext_sc, s_sc, m_sc, intra_sc, inter_sc, toend_sc, cdec_sc, *, L):
    del fb_ref, bb_ref, sin_ref, sout_ref
    step = pl.program_id(0)
    n_units = N_DIR * H_A
    lane = lax.broadcasted_iota(jnp.int32, (HEAD_W, HEAD_W), 1)
    t_idx = lax.broadcasted_iota(jnp.int32, (L, L), 0)
    s_idx = lax.broadcasted_iota(jnp.int32, (L, L), 1)
    masks = (s_idx <= t_idx, s_idx >= t_idx)

    @pl.when(step == 0)
    def _():
        log_gamma = _log_sigmoid(rl_ref[...])
        rel = jnp.abs(t_idx - s_idx).astype(F32)
        pos_col = lax.broadcasted_iota(jnp.int32, (L, HEAD_W), 0).astype(F32)
        for u in range(n_units):
            lg = log_gamma[u:u + 1, 0:1]
            intra_sc[u] = jnp.where(masks[u // H_B], jnp.exp(lg * rel), 0.0)
            pos = pos_col if u < H_B else (L - 1.0) - pos_col
            inter_sc[u] = jnp.exp(lg * (pos + 1.0))
        unit = lax.broadcasted_iota(jnp.int32, (n_units, L), 0)
        pos_row = lax.broadcasted_iota(jnp.int32, (n_units, L), 1).astype(F32)
        pos_row = jnp.where(unit < H_B, pos_row, (L - 1.0) - pos_row)
        toend_sc[...] = jnp.exp(log_gamma[:, 0:1] * ((L - 1.0) - pos_row))
        cdec_sc[...] = jnp.exp(log_gamma * float(L))

    @pl.when(jnp.logical_and(first_ref[step] == 1, isp_ref[step] == 1))
    def _():
        cext_sc[...] = jnp.zeros_like(cext_sc)
        s_sc[...] = jnp.zeros_like(s_sc)
        m_sc[...] = jnp.zeros_like(m_sc)

    @pl.when(jnp.logical_and(first_ref[step] == 1, isp_ref[step] == 0))
    def _():
        n0 = n0_ref[0]
        n0_t = jnp.concatenate([n0, jnp.zeros((HEAD_W - n_units, HEAD_W), F32)], axis=0).T
        for u in range(n_units):
            cext_sc[u, :, 0:HEAD_W] = c0_ref[0, u]
            cext_sc[u, :, HEAD_W:2 * HEAD_W] = jnp.broadcast_to(n0_t[:, u:u + 1], (HEAD_W, HEAD_W))
            s_sc[u] = s0_ref[0, u]
        m_sc[...] = m0_ref[0]

    ones_ext = jnp.ones((L, HEAD_W), BF16)
    row_l = lax.broadcasted_iota(jnp.int32, (L, HEAD_W), 0)
    tri = tuple(jnp.where(m, 1.0, 0.0) for m in masks)
    qa = (qaf_ref, qab_ref)
    qb = (qbf_ref, qbb_ref)
    g = (gf_ref, gb_ref)
    out = (hf_ref, hb_ref)
    m_all = m_sc[...]
    toend = toend_sc[...]
    cdec = cdec_sc[...]
    m_rows = []

    def wide(x):
        return x if L == HEAD_W else jnp.concatenate([x] * (L // HEAD_W), axis=1)

    for d in range(N_DIR):
        ig_all = g[d][:, 0:HEAD_W]
        lf_all = g[d][:, HEAD_W:2 * HEAD_W]
        bt_all = _dot(tri[d], lf_all, precision=_HI)
        a_all = ig_all - bt_all
        a_rows = a_all.T[0:n_units, :]
        cm_all = a_all
        shift = 1
        while shift < L:
            if d == 0:
                moved = jnp.where(row_l >= shift, pltpu.roll(cm_all, shift, axis=0), -jnp.inf)
            else:
                moved = jnp.where(row_l < L - shift, pltpu.roll(cm_all, L - shift, axis=0), -jnp.inf)
            cm_all = jnp.maximum(cm_all, moved)
            shift *= 2
        end = L - 1 if d == 0 else 0
        for h in range(H_A):
            u = d * H_A + h
            cs = slice(h * HEAD_W, (h + 1) * HEAD_W)
            m_prev = m_all[u:u + 1, :]
            g_rep = jnp.maximum(m_prev, jnp.broadcast_to(cm_all[:, u:u + 1], (L, HEAD_W)))
            bt_rep = jnp.broadcast_to(bt_all[:, u:u + 1], (L, HEAD_W))
            a_row = a_rows[u:u + 1, :]
            dmat = jnp.exp(jnp.where(masks[d], a_row - wide(g_rep), -jnp.inf))
            inter_w = jnp.exp(m_prev - g_rep)
            q = qa[d][:, cs]
            k = qa[d][:, 512 + h * HEAD_W:512 + (h + 1) * HEAD_W]
            v = qa[d][:, 1024 + h * HEAD_W:1024 + (h + 1) * HEAD_W]
            v_ext = jnp.concatenate([v, ones_ext], axis=1)
            s = (_dot_nt(q, k) * dmat).astype(BF16)
            lhs = jnp.concatenate([s, (q.astype(F32) * inter_w).astype(BF16)], axis=1)
            rhs = jnp.concatenate([v_ext, cext_sc[u].astype(BF16)], axis=0)
            num = _dot(lhs, rhs)
            inv = 1.0 / jnp.maximum(jnp.abs(num[:, HEAD_W:2 * HEAD_W]), jnp.exp(-(bt_rep + g_rep)))
            out[d][:, cs] = (num[:, 0:HEAD_W] * inv).astype(out[d].dtype)
            m_prev1 = m_prev[:, 0:1]
            g_end = jnp.maximum(m_prev1, cm_all[end:end + 1, u:u + 1])
            wk_row = jnp.exp(a_row - g_end)
            kw = (k.astype(F32).T * wk_row).astype(BF16)
            cext_sc[u] = jnp.exp(m_prev1 - g_end) * cext_sc[u] + _dot(kw, v_ext)
            m_rows.append(jnp.broadcast_to(bt_all[end:end + 1, u:u + 1] + g_end, (1, HEAD_W)))
            qr = qb[d][:, cs]
            kr = qb[d][:, 512 + h * HEAD_W:512 + (h + 1) * HEAD_W]
            vr = qb[d][:, 1024 + h * HEAD_W:1024 + (h + 1) * HEAD_W]
            sr = (_dot_nt(qr, kr) * intra_sc[u]).astype(BF16)
            lhs = jnp.concatenate([sr, (qr.astype(F32) * inter_sc[u]).astype(BF16)], axis=1)
            rhs = jnp.concatenate([vr, s_sc[u].astype(BF16)], axis=0)
            out[d][:, 512 + h * HEAD_W:512 + (h + 1) * HEAD_W] = _dot(lhs, rhs).astype(out[d].dtype)
            krw = (kr.astype(F32).T * toend[u:u + 1, :]).astype(BF16)
            s_sc[u] = cdec[u:u + 1, :] * s_sc[u] + _dot(krw, vr)
    m_sc[...] = jnp.concatenate(m_rows, axis=0)

    @pl.when(last_ref[step] == 1)
    def _():
        n_cols = jnp.zeros((HEAD_W, HEAD_W), F32)
        for u in range(n_units):
            cn_ref[0, u] = cext_sc[u, :, 0:HEAD_W]
            sn_ref[0, u] = s_sc[u]
            n_cols = jnp.where(lane == u, cext_sc[u, :, HEAD_W:HEAD_W + 1], n_cols)
        nn_ref[0] = n_cols.T[0:n_units, :]
        mn_ref[0] = m_sc[...]


def _scan_tables(rows, L):
    fb, bb, first, last, sin, sout, isp = [], [], [], [], [], [], []
    base = 0
    for kind, n_seq, seq_len in (("p", rows.n_prompt_seq, rows.prompt_len),
                                 ("s", rows.n_sample_seq, rows.sample_len)):
        nc = seq_len // L
        for b in range(n_seq):
            for c in range(nc):
                fb.append(base + b * nc + c)
                bb.append(base + b * nc + nc - 1 - c)
                first.append(int(c == 0))
                last.append(int(c == nc - 1 and kind == "p"))
                sin.append(b if kind == "s" else 0)
                sout.append(b if kind == "p" else rows.n_prompt_seq - 1)
                isp.append(int(kind == "p"))
        base += n_seq * nc
    return [jnp.asarray(np.asarray(t, np.int32)) for t in (fb, bb, first, last, sin, sout, isp)]


def _scan_even(qkva, qkvb, gates, c0, n0, m0, s0, ret_logit, rows):
    L = SCAN_CHUNK
    n = qkva.shape[0]
    tables = _scan_tables(rows, L)
    n_steps = int(tables[0].shape[0])
    nu = N_DIR * H_A
    fwd = lambda w: pl.BlockSpec((L, w), lambda s, fb, bb, fi, la, si, so, ip: (fb[s], 0))
    bwd = lambda w: pl.BlockSpec((L, w), lambda s, fb, bb, fi, la, si, so, ip: (bb[s], 0))
    st_in4 = pl.BlockSpec((1, nu, HEAD_W, HEAD_W), lambda s, fb, bb, fi, la, si, so, ip: (si[s], 0, 0, 0))
    st_in3 = pl.BlockSpec((1, nu, HEAD_W), lambda s, fb, bb, fi, la, si, so, ip: (si[s], 0, 0))
    st_out4 = pl.BlockSpec((1, nu, HEAD_W, HEAD_W), lambda s, fb, bb, fi, la, si, so, ip: (so[s], 0, 0, 0))
    st_out3 = pl.BlockSpec((1, nu, HEAD_W), lambda s, fb, bb, fi, la, si, so, ip: (so[s], 0, 0))
    nps = rows.n_prompt_seq
    return pl.pallas_call(
        functools.partial(_scan_kernel, L=L),
        grid_spec=pltpu.PrefetchScalarGridSpec(
            num_scalar_prefetch=7,
            grid=(n_steps,),
            in_specs=[fwd(1536), bwd(1536), fwd(1536), bwd(1536), fwd(256), bwd(256),
                      st_in4, st_in3, st_in3, st_in4,
                      pl.BlockSpec((nu, HEAD_W), lambda s, *_: (0, 0))],
            out_specs=[fwd(1024), bwd(1024), st_out4, st_out3, st_out3, st_out4],
            scratch_shapes=[pltpu.VMEM((nu, HEAD_W, 2 * HEAD_W), F32),
                            pltpu.VMEM((nu, HEAD_W, HEAD_W), F32),
                            pltpu.VMEM((nu, HEAD_W), F32),
                            pltpu.VMEM((nu, L, L), F32),
                            pltpu.VMEM((nu, L, HEAD_W), F32),
                            pltpu.VMEM((nu, L), F32),
                            pltpu.VMEM((nu, HEAD_W), F32)]),
        out_shape=[jax.ShapeDtypeStruct((n, 1024), BF16),
                   jax.ShapeDtypeStruct((n, 1024), BF16),
                   jax.ShapeDtypeStruct((nps, nu, HEAD_W, HEAD_W), F32),
                   jax.ShapeDtypeStruct((nps, nu, HEAD_W), F32),
                   jax.ShapeDtypeStruct((nps, nu, HEAD_W), F32),
                   jax.ShapeDtypeStruct((nps, nu, HEAD_W, HEAD_W), F32)],
        compiler_params=_params(1),
        name="scan_even",
    )(*tables, qkva, qkva, qkvb, qkvb, gates, gates, c0, n0, m0, s0, ret_logit)


def _proj_odd_kernel(x_ref, mod_ref, gain_ref, w_ref, e_ref, qg_ref, kg_ref, cos_ref, sin_ref,
                     q_ref, k_ref, v_ref, *, rope):
    shift = mod_ref[0, 0:1, :]
    scale = mod_ref[0, 1:2, :]
    hb = (_rms(x_ref[...]) * gain_ref[...] * (1.0 + scale) + shift).astype(BF16)

    def qk_norm(raw, g):
        ss = _dot((raw * raw).astype(BF16), e_ref[...])
        return raw * lax.rsqrt(ss * (1.0 / DH_C) + EPS) * g

    def rotate(y):
        if not rope:
            return y
        lane = lax.broadcasted_iota(jnp.int32, y.shape, 1)
        first_half = (lane & 31) < 16
        partner = jnp.where(first_half, pltpu.roll(y, y.shape[1] - 16, axis=1), pltpu.roll(y, 16, axis=1))
        cos = jnp.concatenate([cos_ref[...]] * (y.shape[1] // HEAD_W), axis=1)
        sin = jnp.concatenate([sin_ref[...]] * (y.shape[1] // HEAD_W), axis=1)
        return y * cos + partner * sin

    q = rotate(qk_norm(_dot(hb, w_ref[:, 0:1024]), qg_ref[...]))
    q_ref[...] = (q * (DH_C ** -0.5 * math.log2(math.e))).astype(q_ref.dtype)
    k = rotate(qk_norm(_dot(hb, w_ref[:, 1024:2048]), kg_ref[...]))
    k_ref[...] = k.astype(k_ref.dtype)
    v_ref[...] = _dot(hb, w_ref[:, 2048:3072]).astype(v_ref.dtype)


def _proj_odd(x, mods, gain, w_qkv, e64, q_gain, k_gain, cos, sin, rows, *, sample):
    d = x.shape[1]
    tm = ROW_TILE
    if sample:
        n, base, kv_dtype = rows.n_sample, rows.n_prompt // tm, BF16
        per_seq = rows.sample_len // tm
        table = lambda i: (i % per_seq, 0)
    else:
        n, base, kv_dtype = rows.n_prompt, 0, F32
        table = lambda i: (0, 0)
    return pl.pallas_call(
        functools.partial(_proj_odd_kernel, rope=sample),
        grid=(n // tm,),
        in_specs=[pl.BlockSpec((tm, d), lambda i: (base + i, 0)),
                  pl.BlockSpec((1, 6, d), lambda i: (rows.cond_of_tile(base + i, tm), 0, 0)),
                  pl.BlockSpec((1, d), lambda i: (0, 0)),
                  pl.BlockSpec(w_qkv.shape, lambda i: (0, 0)),
                  pl.BlockSpec(e64.shape, lambda i: (0, 0)),
                  pl.BlockSpec((1, d), lambda i: (0, 0)),
                  pl.BlockSpec((1, d), lambda i: (0, 0)),
                  pl.BlockSpec((tm, HEAD_W), table),
                  pl.BlockSpec((tm, HEAD_W), table)],
        out_specs=[pl.BlockSpec((tm, d), lambda i: (i, 0))] * 3,
        out_shape=[jax.ShapeDtypeStruct((n, d), BF16),
                   jax.ShapeDtypeStruct((n, d), kv_dtype),
                   jax.ShapeDtypeStruct((n, d), kv_dtype)],
        compiler_params=_params(1),
        name="proj_odd_sample" if sample else "proj_odd_prompt",
    )(x, mods, gain, w_qkv, e64, q_gain, k_gain, cos, sin)


def _rope_tables(sample_len):
    t = np.arange(sample_len)
    nf = DH_C // 4
    inv = ROPE_BASE ** (-np.arange(nf, dtype=np.float32) / nf)
    row = (t // GRID_W).astype(np.float32)[:, None] * inv[None, :]
    col = (t % GRID_W).astype(np.float32)[:, None] * inv[None, :]
    cos64 = np.concatenate([np.cos(row), np.cos(row), np.cos(col), np.cos(col)], axis=1)
    sin64 = np.concatenate([-np.sin(row), np.sin(row), -np.sin(col), np.sin(col)], axis=1)
    cos = np.concatenate([cos64, cos64], axis=1).astype(np.float32)
    sin = np.concatenate([sin64, sin64], axis=1).astype(np.float32)
    return jnp.asarray(cos), jnp.asarray(sin)


def _attn_kernel(*refs, tq, tk, n_main, n_cache, lam_init):
    if n_cache:
        q_ref, k_ref, v_ref, ck_ref, cv_ref, lq_ref, og_ref, o_ref, vt_sc = refs
    else:
        q_ref, k_ref, v_ref, lq_ref, og_ref, o_ref, vt_sc = refs
    fill_w = 256

    @pl.when(pl.program_id(2) == 0)
    def _():
        ones_rows = jnp.where(lax.broadcasted_iota(jnp.int32, (8, fill_w), 0) == 0, 1.0, 0.0).astype(BF16)
        for j in range(n_main // fill_w):
            vt = v_ref[0, j * fill_w:(j + 1) * fill_w, :].astype(F32).T.astype(BF16)
            vt_sc[:, j * fill_w:(j + 1) * fill_w] = jnp.concatenate([vt, ones_rows], axis=0)
        if n_cache:
            vt = cv_ref[0].astype(F32).T.astype(BF16)
            vt_sc[:, n_main:n_main + n_cache] = jnp.concatenate([vt, ones_rows], axis=0)

    qt = q_ref[0].astype(F32).T
    row = lax.broadcasted_iota(jnp.int32, qt.shape, 0)
    qst = jnp.concatenate([jnp.where(row < DH_C, qt, 0.0), jnp.where(row >= DH_C, qt, 0.0)],
                          axis=1).astype(BF16)

    def scores(kt):
        return _dot(kt.astype(BF16), qst)

    def accumulate(carry, s, vt):
        m, acc = carry
        m_new = jnp.maximum(m, jnp.max(s, axis=0, keepdims=True))
        p = jnp.exp2(s - m_new).astype(BF16)
        acc = jnp.exp2(m - m_new) * acc + _dot(vt, p)
        return m_new, acc

    tiles = [(k_ref, j * tk, tk, j * tk) for j in range(n_main // tk)]
    if n_cache:
        tiles = [(ck_ref, 0, n_cache, n_main)] + tiles
    carry = (jnp.full((1, 2 * tq), -jnp.inf, F32), jnp.zeros((HEAD_W + 8, 2 * tq), F32))
    ref0, r0, n0, _ = tiles[0]
    s_next = scores(ref0[0, r0:r0 + n0, :])
    for j, (_, _, n_keys, c0) in enumerate(tiles):
        s_cur = s_next
        if j + 1 < len(tiles):
            ref1, r1, n1, _ = tiles[j + 1]
            s_next = scores(ref1[0, r1:r1 + n1, :])
        carry = accumulate(carry, s_cur, vt_sc[:, c0:c0 + n_keys])
    _, acc = carry
    o_t = acc[0:HEAD_W, :] / acc[HEAD_W:HEAD_W + 1, :]
    lq = lq_ref[...]
    lam = (jnp.exp(jnp.sum(lq[0:1, :] * lq[1:2, :], axis=1, keepdims=True))
           - jnp.exp(jnp.sum(lq[2:3, :] * lq[3:4, :], axis=1, keepdims=True)) + lam_init)
    o = (o_t[:, 0:tq] - lam * o_t[:, tq:2 * tq]).T
    o_ref[0] = (_rms(o) * og_ref[...] * (1.0 - lam_init)).astype(o_ref.dtype)


def _attention(q, k, v, cache_k, cache_v, lam_qk, out_gain, lam_init):
    b, t, d = q.shape
    tq = min(ATTN_TQ, t)
    tk = min(ATTN_TK, t)
    n_cache = 0 if cache_k is None else cache_k.shape[1]
    assert t % tk == 0 and t % tq == 0 and t % 256 == 0 and n_cache in (0, 256)
    seq = lambda n: pl.BlockSpec((1, n, HEAD_W), lambda bi, h, qi: (bi, 0, h))
    in_specs = [pl.BlockSpec((1, tq, HEAD_W), lambda bi, h, qi: (bi, qi, h)), seq(t), seq(t)]
    args = [q, k, v]
    if n_cache:
        in_specs += [seq(n_cache), seq(n_cache)]
        args += [cache_k, cache_v]
    in_specs += [pl.BlockSpec(lam_qk.shape, lambda bi, h, qi: (0, 0)),
                 pl.BlockSpec(out_gain.shape, lambda bi, h, qi: (0, 0))]
    args += [lam_qk, out_gain]
    return pl.pallas_call(
        functools.partial(_attn_kernel, tq=tq, tk=tk, n_main=t, n_cache=n_cache, lam_init=lam_init),
        grid=(b, H_C, t // tq),
        in_specs=in_specs,
        out_specs=pl.BlockSpec((1, tq, HEAD_W), lambda bi, h, qi: (bi, qi, h)),
        out_shape=jax.ShapeDtypeStruct((b, t, d), BF16),
        scratch_shapes=[pltpu.VMEM((HEAD_W + 8, t + n_cache), BF16)],
        compiler_params=_params(3),
        name="diff_attention",
    )(*args)


def _out_kernel(*refs, n_tiles, **static):
    ys0_ref = refs[-4]
    ys0_ref[...] = jnp.zeros_like(ys0_ref)

    @pl.when(pl.program_id(0) < n_tiles)
    def _():
        _out_tile(*refs, **static)


def _out_tile(*refs, tm, even, n_prompt_tiles):
    if even:
        (hf_ref, hb_ref, oz_ref, ga_ref, gb_ref, x_ref, mod_ref, w_ref, fg_ref, wr_ref, rb_ref,
         x1_ref, xs_ref, ys0_ref, sl_ref, gw_ref, cnt_ref) = refs
    else:
        (op_ref, os_ref, x_ref, mod_ref, w_ref, fg_ref, wr_ref, rb_ref,
         x1_ref, xs_ref, ys0_ref, sl_ref, gw_ref, cnt_ref) = refs
    del ys0_ref
    i = pl.program_id(0)

    if even:
        hs = hf_ref[...].astype(F32) + hb_ref[...].astype(F32)
        oz = oz_ref[...].astype(F32)
        parts = []
        for h in range(H_A + H_B):
            cs = slice(h * HEAD_W, (h + 1) * HEAD_W)
            gain = ga_ref[:, cs] if h < H_A else gb_ref[:, (h - H_A) * HEAD_W:(h - H_A + 1) * HEAD_W]
            act = _sigmoid(oz[:, cs]) if h < H_A else _silu(oz[:, cs])
            parts.append((_rms(hs[:, cs]) * gain * act).astype(BF16))
        y_in = jnp.concatenate(parts, axis=1)
    else:
        y_in = jnp.where(i < n_prompt_tiles, op_ref[...], os_ref[...])

    x1 = x_ref[...] + mod_ref[0, 2:3, :] * _dot(y_in, w_ref[...])
    x1_ref[...] = x1
    h2 = _rms(x1) * fg_ref[...] * (1.0 + mod_ref[0, 4:5, :]) + mod_ref[0, 3:4, :]
    h2b = h2.astype(BF16)

    score = _sigmoid(_dot(h2b, wr_ref[...]))
    st = score.T[0:N_EXPERTS, :]
    sel = st + rb_ref[:, 0:1]
    srow = [st[e:e + 1, :] for e in range(N_EXPERTS)]
    brow = [sel[e:e + 1, :] for e in range(N_EXPERTS)]
    epg = EXPERTS_PER_GROUP
    gscore = []
    for g in range(N_GROUPS):
        a = brow[g * epg:(g + 1) * epg]
        best = None
        for p in range(epg):
            for r in range(p + 1, epg):
                pair = a[p] + a[r]
                best = pair if best is None else jnp.maximum(best, pair)
        gscore.append(best)
    gbest, gidx = gscore[0], jnp.zeros((1, tm), jnp.int32)
    for g in range(1, N_GROUPS):
        better = gscore[g] > gbest
        gbest = jnp.where(better, gscore[g], gbest)
        gidx = jnp.where(better, g, gidx)
    vals, sig = [], []
    for p in range(epg):
        vp, sp = brow[p], srow[p]
        for g in range(1, N_GROUPS):
            vp = jnp.where(gidx == g, brow[g * epg + p], vp)
            sp = jnp.where(gidx == g, srow[g * epg + p], sp)
        vals.append(vp)
        sig.append(sp)
    v1, i1, w1 = vals[0], jnp.zeros((1, tm), jnp.int32), sig[0]
    for p in range(1, epg):
        better = vals[p] > v1
        v1 = jnp.where(better, vals[p], v1)
        i1 = jnp.where(better, p, i1)
        w1 = jnp.where(better, sig[p], w1)
    v2 = jnp.full((1, tm), -jnp.inf, F32)
    i2 = jnp.zeros((1, tm), jnp.int32)
    w2 = jnp.zeros((1, tm), F32)
    for p in range(epg):
        better = jnp.logical_and(i1 != p, vals[p] > v2)
        v2 = jnp.where(better, vals[p], v2)
        i2 = jnp.where(better, p, i2)
        w2 = jnp.where(better, sig[p], w2)
    e1 = gidx * epg + i1
    e2 = gidx * epg + i2
    wsum = w1 + w2

    erow = lax.broadcasted_iota(jnp.int32, (N_EXPERTS, tm), 0)
    earlier = jnp.where(lax.broadcasted_iota(jnp.int32, (tm, tm), 0)
                        < lax.broadcasted_iota(jnp.int32, (tm, tm), 1), 1.0, 0.0).astype(BF16)
    oh1 = jnp.where(erow == e1, 1.0, 0.0)
    oh2 = jnp.where(erow == e2, 1.0, 0.0)
    before1 = _dot(oh1.astype(BF16), earlier)
    before2 = _dot(oh2.astype(BF16), earlier)
    cnt1 = jnp.sum(oh1, axis=1, keepdims=True)
    cnt = cnt1 + jnp.sum(oh2, axis=1, keepdims=True)
    padded = jnp.floor((cnt + (MOE_CHUNK - 1.0)) * (1.0 / MOE_CHUNK)) * MOE_CHUNK
    below = jnp.where(lax.broadcasted_iota(jnp.int32, (N_EXPERTS, N_EXPERTS), 1)
                      < lax.broadcasted_iota(jnp.int32, (N_EXPERTS, N_EXPERTS), 0), 1.0, 0.0)
    start = _dot(below, jnp.broadcast_to(padded, (N_EXPERTS, HEAD_W)), precision=_HI)[:, 0:1]
    slot1 = jnp.sum(oh1 * (start + before1), axis=0, keepdims=True).astype(jnp.int32)
    slot2 = jnp.sum(oh2 * (start + cnt1 + before2), axis=0, keepdims=True).astype(jnp.int32)
    local = lax.broadcasted_iota(jnp.int32, (MOE_TILE_ROWS, tm), 0)
    place = jnp.where(jnp.logical_or(local == slot1, local == slot2), 1.0, 0.0).astype(BF16)
    xs_ref[...] = _dot(place, h2b).astype(BF16)
    zi = jnp.zeros((6, tm), jnp.int32)
    sl_ref[...] = jnp.concatenate([slot1, slot2, zi], axis=0)
    gw_ref[...] = jnp.concatenate([w1 / wsum, w2 / wsum, jnp.zeros((6, tm), F32)], axis=0)
    cnt_ref[...] = jnp.broadcast_to(cnt, (N_EXPERTS, HEAD_W))


def _out_and_route(mix_in, x, mods, w_out, ffn_gain, w_router, router_bias, rows, *, even):
    n, d = x.shape
    tm = ROW_TILE
    n_tiles = n // tm
    n_pt = rows.n_prompt // tm
    tile = lambda i: jnp.minimum(i, n_tiles - 1)
    row = lambda w: pl.BlockSpec((tm, w), lambda i: (tile(i), 0))
    full = lambda a: pl.BlockSpec(a.shape, lambda i: (0,) * a.ndim)
    if even:
        hf, hb, oz, gain_a, gain_b = mix_in
        in_specs = [row(1024), row(1024), row(1024), full(gain_a), full(gain_b)]
        args = [hf, hb, oz, gain_a, gain_b]
    else:
        o_p, o_s = mix_in
        in_specs = [pl.BlockSpec((tm, d), lambda i: (jnp.minimum(i, n_pt - 1), 0)),
                    pl.BlockSpec((tm, d), lambda i: (jnp.maximum(tile(i) - n_pt, 0), 0))]
        args = [o_p, o_s]
    in_specs += [row(d), pl.BlockSpec((1, 6, d), lambda i: (rows.cond_of_tile(tile(i), tm), 0, 0)),
                 full(w_out), full(ffn_gain), full(w_router), full(router_bias)]
    args += [x, mods, w_out, ffn_gain, w_router, router_bias]
    col = lambda: pl.BlockSpec((8, tm), lambda i: (0, tile(i)))
    return pl.pallas_call(
        functools.partial(_out_kernel, n_tiles=n_tiles, tm=tm, even=even, n_prompt_tiles=n_pt),
        grid=(n_tiles + MOE_SCRATCH_TILES,),
        in_specs=in_specs,
        out_specs=[row(d), pl.BlockSpec((MOE_TILE_ROWS, d), lambda i: (tile(i), 0)),
                   pl.BlockSpec((MOE_TILE_ROWS, d), lambda i: (i, 0)), col(), col(),
                   pl.BlockSpec((N_EXPERTS, HEAD_W), lambda i: (tile(i), 0))],
        out_shape=[jax.ShapeDtypeStruct((n, d), F32),
                   jax.ShapeDtypeStruct((n_tiles * MOE_TILE_ROWS, d), BF16),
                   jax.ShapeDtypeStruct(((n_tiles + MOE_SCRATCH_TILES) * MOE_TILE_ROWS, d), BF16),
                   jax.ShapeDtypeStruct((8, n), jnp.int32),
                   jax.ShapeDtypeStruct((8, n), F32),
                   jax.ShapeDtypeStruct((n_tiles * N_EXPERTS, HEAD_W), F32)],
        compiler_params=_params(1),
        name="out_even" if even else "out_odd",
    )(*args)


def _moe_kernel(cin_ref, cout_ref, be_ref, nu_ref, xs_hbm, ys_init_hbm, wg_ref, wu_ref, wd_ref, ys_hbm,
                xbuf, ybuf, wg_sc, wu_sc, wd_sc, in_sem, out_sem):
    del ys_init_hbm
    b = pl.program_id(0)
    n_used = nu_ref[0]
    cb, ch = MOE_BLOCK_CHUNKS, MOE_CHUNK

    def in_copy(blk, slot, c):
        src = pl.multiple_of(cin_ref[blk * cb + c] * ch, ch)
        return pltpu.make_async_copy(xs_hbm.at[pl.ds(src, ch), :], xbuf.at[slot, pl.ds(c * ch, ch), :],
                                     in_sem.at[slot])

    def out_copy(blk, slot, c):
        dst = pl.multiple_of(cout_ref[blk * cb + c] * ch, ch)
        return pltpu.make_async_copy(ybuf.at[slot, pl.ds(c * ch, ch), :], ys_hbm.at[pl.ds(dst, ch), :],
                                     out_sem.at[slot])

    def start_in(blk, slot):
        for c in range(cb):
            in_copy(blk, slot, c).start()

    def wait_out(blk, slot):
        for c in range(cb):
            out_copy(blk, slot, c).wait()

    slot = b % 2

    @pl.when(b == 0)
    def _():
        start_in(0, 0)

    @pl.when(b + 1 < n_used)
    def _():
        start_in(b + 1, 1 - slot)

    @pl.when(b < n_used)
    def _():
        for c in range(cb):
            in_copy(b, slot, c).wait()

        @pl.when(b >= 2)
        def _():
            wait_out(b - 2, slot)

        @pl.when(jnp.logical_or(b == 0, be_ref[b] != be_ref[jnp.maximum(b - 1, 0)]))
        def _():
            wg_sc[...] = wg_ref[0, 0].astype(BF16)
            wu_sc[...] = wu_ref[0, 0].astype(BF16)
            wd_sc[...] = wd_ref[0, 0].astype(BF16)

        x = xbuf[slot]
        a = (_silu(_dot(x, wg_sc[...])) * _dot(x, wu_sc[...])).astype(BF16)
        ybuf[slot] = _dot(a, wd_sc[...]).astype(BF16)
        for c in range(cb):
            out_copy(b, slot, c).start()

    @pl.when(b == pl.num_programs(0) - 1)
    def _():
        @pl.when(n_used >= 2)
        def _():
            wait_out(n_used - 2, n_used % 2)

        wait_out(n_used - 1, (n_used - 1) % 2)


def _moe_experts(xs_local, ys_init, chunk_in, chunk_out, block_e, n_used, w_gate, w_up, w_down, layer):
    d = xs_local.shape[1]
    n_blocks = chunk_in.shape[0] // MOE_BLOCK_CHUNKS
    wspec = pl.BlockSpec((1, 1, d, d), lambda i, ci, co, be, nu: (layer, be[i], 0, 0))
    hbm = pl.BlockSpec(memory_space=pl.ANY)
    return pl.pallas_call(
        _moe_kernel,
        grid_spec=pltpu.PrefetchScalarGridSpec(
            num_scalar_prefetch=4,
            grid=(n_blocks,),
            in_specs=[hbm, hbm, wspec, wspec, wspec],
            out_specs=hbm,
            scratch_shapes=[pltpu.VMEM((2, MOE_BLOCK, d), BF16), pltpu.VMEM((2, MOE_BLOCK, d), BF16),
                            pltpu.VMEM((d, d), BF16), pltpu.VMEM((d, d), BF16), pltpu.VMEM((d, d), BF16),
                            pltpu.SemaphoreType.DMA((2,)), pltpu.SemaphoreType.DMA((2,))]),
        out_shape=jax.ShapeDtypeStruct(ys_init.shape, BF16),
        input_output_aliases={5: 0},
        compiler_params=_params(1),
        name="moe_experts",
    )(chunk_in, chunk_out, block_e, n_used, xs_local, ys_init, w_gate, w_up, w_down)


def _combine_kernel(x_ref, ys_ref, sl_ref, gw_ref, mod_ref, o_ref):
    sl = sl_ref[...]
    w = gw_ref[...]
    ys = ys_ref[...]
    local = lax.broadcasted_iota(jnp.int32, (sl.shape[0], ys.shape[0]), 1)
    pick = (jnp.where(local == sl[:, 0:1], w[:, 0:1], 0.0)
            + jnp.where(local == sl[:, 1:2], w[:, 1:2], 0.0)).astype(BF16)
    o_ref[...] = x_ref[...] + mod_ref[0, 5:6, :] * _dot(pick, ys)


def _combine(x1, ys_local, slot_cols, gw_cols, mods, rows):
    n, d = x1.shape
    tm = ROW_TILE
    return pl.pallas_call(
        _combine_kernel,
        grid=(n // tm,),
        in_specs=[pl.BlockSpec((tm, d), lambda i: (i, 0)),
                  pl.BlockSpec((MOE_TILE_ROWS, d), lambda i: (i, 0)),
                  pl.BlockSpec((tm, 2), lambda i: (i, 0)),
                  pl.BlockSpec((tm, 2), lambda i: (i, 0)),
                  pl.BlockSpec((1, 6, d), lambda i: (rows.cond_of_tile(i, tm), 0, 0))],
        out_specs=pl.BlockSpec((tm, d), lambda i: (i, 0)),
        out_shape=jax.ShapeDtypeStruct((n, d), F32),
        compiler_params=_params(1),
        name="moe_combine",
    )(x1, ys_local, slot_cols, gw_cols, mods)


def _chunk_tables(counts):
    n_tiles = counts.shape[0]
    cb, tc = MOE_BLOCK_CHUNKS, MOE_TILE_CHUNKS
    n_list = (n_tiles * (tc - 1) + N_EXPERTS * (cb - 1) + cb - 1) // cb * cb
    nch = (counts + MOE_CHUNK - 1) // MOE_CHUNK
    first = jnp.cumsum(nch, axis=1) - nch
    total = jnp.sum(nch, axis=0)
    padded = (total + cb - 1) // cb * cb
    e_end = jnp.cumsum(padded)
    e_start = e_end - padded
    before = (jnp.cumsum(nch, axis=0) - nch).T
    seg_start = (e_start[:, None] + before).reshape(-1)
    seg_len = nch.T.reshape(-1)
    seg_first = (jnp.arange(n_tiles, dtype=jnp.int32)[None, :] * tc + first.T).reshape(-1)
    j = jnp.arange(n_list, dtype=jnp.int32)
    seg = jnp.clip(jnp.sum((seg_start[None, :] <= j[:, None]).astype(jnp.int32), axis=1) - 1,
                   0, seg_start.shape[0] - 1)
    e_j = seg // n_tiles
    assert n_list < (1 << 13) and tc < (1 << 6) and n_tiles * tc < (1 << 13)
    packed = (seg_start.astype(jnp.uint32) | (seg_len.astype(jnp.uint32) << 13)
              | (seg_first.astype(jnp.uint32) << 19))[seg]
    seg_off = j - (packed & 0x1FFF).astype(jnp.int32)
    valid = jnp.logical_and(seg_off < ((packed >> 13) & 0x3F).astype(jnp.int32), j < e_end[-1])
    src = (packed >> 19).astype(jnp.int32) + seg_off
    experts = jnp.arange(N_EXPERTS, dtype=jnp.int32)
    real_end = jnp.sum(jnp.where(e_j[:, None] == experts[None, :], (e_start + total)[None, :], 0), axis=1)
    scratch = n_tiles * tc + e_j * cb + jnp.clip(j - real_end, 0, cb - 1)
    chunk_in = jnp.where(valid, src, tc - 1).astype(jnp.int32)
    chunk_out = jnp.where(valid, src, scratch).astype(jnp.int32)
    block_e = e_j[::cb]
    n_used = (e_end[-1] // cb).astype(jnp.int32).reshape(1)
    return chunk_in, chunk_out, block_e, n_used


def _moe_layer(x1, xs_local, ys_zero, slots, gw, counts, mods, w_gate, w_up, w_down, layer, rows):
    n_tiles = x1.shape[0] // ROW_TILE
    counts = counts[:, 0].astype(jnp.int32).reshape(n_tiles, N_EXPERTS)
    chunk_in, chunk_out, block_e, n_used = _chunk_tables(counts)
    ys_local = _moe_experts(xs_local, ys_zero, chunk_in, chunk_out, block_e, n_used, w_gate, w_up, w_down, layer)
    return _combine(x1, ys_local, slots[0:2].T, gw[0:2].T, mods, rows)


def kernel(x_prompt, x_sample, c, state_mlstm_C, state_mlstm_n, state_mlstm_m, state_ret_S, cache_k, cache_v, c_ctx, w_ada, b_ada, norm_mix_gain, norm_ffn_gain, w_in_even, mlstm_conv, mlstm_gate_bias, mlstm_out_gain, ret_decay_logit, ret_out_gain, w_out_even, w_qkv_odd, q_norm_gain, k_norm_gain, lambda_qk, attn_out_gain, w_out_odd, w_router, router_bias, moe_w_gate, moe_w_up, moe_w_down):
    bp, seq, d = x_prompt.shape
    bs, dec_seq, _ = x_sample.shape
    depth = w_ada.shape[0]
    past = cache_k.shape[2]
    rows = _Rows(bp, seq, bs, dec_seq)
    nu = N_DIR * H_A
    assert 1 + bs <= N_COND_PAD

    x = jnp.concatenate([x_prompt.reshape(bp * seq, d), x_sample.reshape(bs * dec_seq, d)], axis=0)
    cond = jnp.concatenate([c_ctx[None, :], c, jnp.zeros((N_COND_PAD - 1 - bs, d), F32)], axis=0)
    mods_all = _modulation_all(cond, w_ada, b_ada).reshape(depth, N_COND_PAD, 6, d)

    w_router_pad = jnp.pad(w_router, ((0, 0), (0, HEAD_W - N_EXPERTS))).astype(BF16)
    router_bias_col = jnp.broadcast_to(router_bias[:, None], (N_EXPERTS, HEAD_W))
    e64 = jnp.asarray(np.kron(np.eye(d // DH_C, dtype=np.float32), np.ones((DH_C, DH_C), np.float32)), BF16)
    cos, sin = _rope_tables(dec_seq)

    st_c, st_n, st_m, st_s, st_k, st_v = [], [], [], [], [], []
    for l in range(depth):
        j = l // 2
        mods = mods_all[l]
        gain_mix = norm_mix_gain[l][None, :]
        if l % 2 == 0:
            w = w_in_even[j]
            wa = 4 * HEAD_W * 4
            gcols = 4 * H_A
            ob = wa + gcols
            w_main = jnp.concatenate([w[:, 0:1536], w[:, ob:ob + 1536], w[:, 1536:2048],
                                      w[:, ob + 1536:ob + 2048]], axis=1).astype(BF16)
            w_g = w[:, wa:wa + gcols]
            zpad = jnp.zeros((d, HEAD_W - nu), F32)
            w_g32 = jnp.concatenate([w_g[:, 0:nu], zpad, w_g[:, nu:2 * nu], zpad], axis=1)
            w_g_hi = w_g32.astype(BF16)
            w_gates = jnp.concatenate([w_g_hi, (w_g32 - w_g_hi.astype(F32)).astype(BF16)], axis=1)
            gb = mlstm_gate_bias[j].reshape(2, nu)
            zb = jnp.zeros((HEAD_W - nu,), F32)
            gate_bias = jnp.concatenate([gb[0], zb, gb[1], zb])[None, :]
            qkva, qkvb, oz, gates = _proj_even(x, mods, gain_mix, w_main, w_gates, mlstm_conv[j], gate_bias, rows)
            c0 = state_mlstm_C[:, j].reshape(bs, nu, HEAD_W, HEAD_W)
            n0 = state_mlstm_n[:, j].reshape(bs, nu, HEAD_W)
            m0 = jnp.broadcast_to(state_mlstm_m[:, j].reshape(bs, nu, 1), (bs, nu, HEAD_W))
            s0 = state_ret_S[:, j].reshape(bs, nu, HEAD_W, HEAD_W)
            ret_logit = jnp.broadcast_to(ret_decay_logit[j].reshape(nu, 1), (nu, HEAD_W))
            hf, hb, cn, nn, mn, sn = _scan_even(qkva, qkvb, gates, c0, n0, m0, s0, ret_logit, rows)
            st_c.append(cn.reshape(bp, N_DIR, H_A, HEAD_W, HEAD_W))
            st_n.append(nn.reshape(bp, N_DIR, H_A, HEAD_W))
            st_m.append(mn[:, :, 0].reshape(bp, N_DIR, H_A))
            st_s.append(sn.reshape(bp, N_DIR, H_B, HEAD_W, HEAD_W))
            mix_in = (hf, hb, oz, mlstm_out_gain[j].reshape(1, H_A * HEAD_W),
                      ret_out_gain[j].reshape(1, H_B * HEAD_W))
            w_out = w_out_even[j].astype(BF16)
        else:
            lam_init = 0.8 - 0.6 * math.exp(-0.3 * l)
            w_qkv = w_qkv_odd[j].astype(BF16)
            qg = jnp.tile(q_norm_gain[j], d // DH_C)[None, :]
            kg = jnp.tile(k_norm_gain[j], d // DH_C)[None, :]
            og = attn_out_gain[j][None, :]
            q_p, k_p, v_p = _proj_odd(x, mods, gain_mix, w_qkv, e64, qg, kg, cos, sin, rows, sample=False)
            q_s, k_s, v_s = _proj_odd(x, mods, gain_mix, w_qkv, e64, qg, kg, cos, sin, rows, sample=True)
            o_p = _attention(q_p.reshape(bp, seq, d), k_p.reshape(bp, seq, d), v_p.reshape(bp, seq, d),
                             None, None, lambda_qk[j], og, lam_init)
            o_s = _attention(q_s.reshape(bs, dec_seq, d), k_s.reshape(bs, dec_seq, d), v_s.reshape(bs, dec_seq, d),
                             cache_k[:, j].reshape(bs, past, d), cache_v[:, j].reshape(bs, past, d),
                             lambda_qk[j], og, lam_init)
            st_k.append(k_p.reshape(bp, seq, H_C, 2, DH_C))
            st_v.append(v_p.reshape(bp, seq, H_C, 2 * DH_C))
            mix_in = (o_p.reshape(bp * seq, d), o_s.reshape(bs * dec_seq, d))
            w_out = w_out_odd[j].astype(BF16)
        x1, xs_local, ys_zero, slots, gw, counts = _out_and_route(
            mix_in, x, mods, w_out, norm_ffn_gain[l][None, :], w_router_pad, router_bias_col, rows,
            even=(l % 2 == 0))
        x = _moe_layer(x1, xs_local, ys_zero, slots, gw, counts, mods, moe_w_gate, moe_w_up, moe_w_down, l, rows)

    dt = x_prompt.dtype
    y_prompt = x[:rows.n_prompt].reshape(bp, seq, d)
    y_sample = x[rows.n_prompt:].reshape(bs, dec_seq, d)
    return (y_prompt, y_sample,
            jnp.stack(st_c, axis=1).astype(dt), jnp.stack(st_n, axis=1).astype(dt),
            jnp.stack(st_m, axis=1).astype(dt), jnp.stack(st_s, axis=1).astype(dt),
            jnp.stack(st_k, axis=1).astype(dt), jnp.stack(st_v, axis=1).astype(dt))
```

```python
import functools
import math

import numpy as np
import jax
import jax.numpy as jnp
from jax import lax
from jax.experimental import pallas as pl
from jax.experimental.pallas import tpu as pltpu

F32 = jnp.float32
BF16 = jnp.bfloat16

EPS = 1e-6
GRID_W = 64
ROPE_BASE = 10000.0
H_A = 4
H_B = 4
H_C = 8
N_DIR = 2
N_EXPERTS = 16
N_GROUPS = 4
EXPERTS_PER_GROUP = N_EXPERTS // N_GROUPS
HEAD_W = 128
DH_C = 64
N_COND_PAD = 16

ROW_TILE = 256
SCAN_CHUNK = 256
ATTN_TQ = 1024
ATTN_TK = 512
MOE_BLOCK = 512
MOE_CHUNK = 16
MOE_BLOCK_CHUNKS = MOE_BLOCK // MOE_CHUNK
MOE_TILE_CHUNKS = (2 * ROW_TILE + N_EXPERTS * (MOE_CHUNK - 1)) // MOE_CHUNK + 1
MOE_TILE_ROWS = MOE_TILE_CHUNKS * MOE_CHUNK
MOE_SCRATCH_TILES = -(-N_EXPERTS * MOE_BLOCK_CHUNKS // MOE_TILE_CHUNKS)
VMEM_LIMIT = 56 * 1024 * 1024

_HI = lax.Precision.HIGHEST


def _dot(a, b, precision=None):
    return jnp.dot(a, b, preferred_element_type=F32, precision=precision)


def _dot_nt(a, b):
    return lax.dot_general(a, b, (((1,), (1,)), ((), ())), preferred_element_type=F32)


def _dot_tn(a, b):
    return lax.dot_general(a, b, (((0,), (0,)), ((), ())), preferred_element_type=F32)


def _rms(x):
    return x * lax.rsqrt(jnp.mean(x * x, axis=-1, keepdims=True) + EPS)


def _sigmoid(x):
    return 1.0 / (1.0 + jnp.exp(-x))


def _silu(x):
    return x * _sigmoid(x)


def _log_sigmoid(x):
    return jnp.minimum(x, 0.0) - jnp.log1p(jnp.exp(-jnp.abs(x)))


def _params(n_axes):
    return pltpu.CompilerParams(dimension_semantics=("arbitrary",) * n_axes,
                                vmem_limit_bytes=VMEM_LIMIT)


def _mod_kernel(cond_ref, w_ref, b_ref, o_ref):
    s = _silu(cond_ref[...]).astype(BF16)
    o_ref[0] = _dot(s, w_ref[0].astype(BF16)) + b_ref[0]


def _modulation_all(cond, w_ada, b_ada):
    depth, d, n = w_ada.shape
    tn = n // 4
    return pl.pallas_call(
        _mod_kernel,
        grid=(depth, n // tn),
        in_specs=[pl.BlockSpec((N_COND_PAD, d), lambda l, j: (0, 0)),
                  pl.BlockSpec((1, d, tn), lambda l, j: (l, 0, j)),
                  pl.BlockSpec((1, 1, tn), lambda l, j: (l, 0, j))],
        out_specs=pl.BlockSpec((1, N_COND_PAD, tn), lambda l, j: (l, 0, j)),
        out_shape=jax.ShapeDtypeStruct((depth, N_COND_PAD, n), F32),
        compiler_params=_params(2),
        name="adaln_modulation",
    )(cond, w_ada, b_ada.reshape(depth, 1, n))


class _Rows:
    def __init__(self, n_prompt_seq, prompt_len, n_sample_seq, sample_len):
        self.prompt_len = prompt_len
        self.sample_len = sample_len
        self.n_prompt_seq = n_prompt_seq
        self.n_sample_seq = n_sample_seq
        self.n_prompt = n_prompt_seq * prompt_len
        self.n_sample = n_sample_seq * sample_len
        self.total = self.n_prompt + self.n_sample
        assert prompt_len % ROW_TILE == 0 and sample_len % ROW_TILE == 0
        assert self.n_prompt % sample_len == 0 or self.n_sample == 0

    def cond_of_tile(self, i, tile):
        n_p = self.n_prompt // tile
        per_seq = self.sample_len // tile
        return jnp.where(i < n_p, 0, 1 + (i - n_p) // per_seq)


def _proj_even_kernel(x_ref, xp_ref, xn_ref, mod_ref, gain_ref, w_ref, wg_ref, cw_ref, gb_ref,
                      qkva_ref, qkvb_ref, oz_ref, g_ref, *, tm, n_prompt, prompt_len, sample_len):
    i = pl.program_id(0)
    shift = mod_ref[0, 0:1, :]
    scale = mod_ref[0, 1:2, :]
    gain = gain_ref[...]

    def modulated(x):
        return _rms(x) * gain * (1.0 + scale) + shift

    h = modulated(x_ref[...])
    hb = h.astype(BF16)
    halo = jnp.concatenate([xp_ref[0], xn_ref[0]], axis=0)
    hh = modulated(halo).astype(BF16)

    w_qk = w_ref[:, 0:1024]
    qk = _dot(hb, w_qk)
    qk_halo = _dot(hh, w_qk)
    prev_row = qk_halo[7:8, :]
    next_row = qk_halo[8:9, :]
    local = lax.broadcasted_iota(jnp.int32, (tm, 1), 0)
    seq_len = jnp.where(i * tm < n_prompt, prompt_len, sample_len)
    pos = (i * tm + local) & (seq_len - 1)
    prev = pltpu.roll(qk, 1, axis=0)
    prev = jnp.where(local == 0, prev_row, prev)
    prev = jnp.where(pos == 0, 0.0, prev)
    nxt = pltpu.roll(qk, tm - 1, axis=0)
    nxt = jnp.where(local == tm - 1, next_row, nxt)
    nxt = jnp.where(pos == seq_len - 1, 0.0, nxt)
    cw = cw_ref[...]
    act = _silu(cw[0:1, :] * prev + cw[1:2, :] * qk + cw[2:3, :] * nxt)
    k_scale = HEAD_W ** -0.5
    qkva_ref[:, 0:512] = act[:, 0:512].astype(BF16)
    qkva_ref[:, 512:1024] = (act[:, 512:1024] * k_scale).astype(BF16)
    qkva_ref[:, 1024:1536] = _dot(hb, w_ref[:, 1024:1536]).astype(BF16)

    qkvb_ref[:, 0:512] = _dot(hb, w_ref[:, 1536:2048]).astype(BF16)
    qkvb_ref[:, 512:1024] = (_dot(hb, w_ref[:, 2048:2560]) * k_scale).astype(BF16)
    qkvb_ref[:, 1024:1536] = _dot(hb, w_ref[:, 2560:3072]).astype(BF16)
    oz_ref[:, 0:512] = _dot(hb, w_ref[:, 3072:3584]).astype(BF16)
    oz_ref[:, 512:1024] = _dot(hb, w_ref[:, 3584:4096]).astype(BF16)

    h_lo = (h - hb.astype(F32)).astype(BF16)
    g_hl = _dot(hb, wg_ref[...])
    gates = (g_hl[:, 0:256] + g_hl[:, 256:512] + _dot(h_lo, wg_ref[:, 0:256])
             + gb_ref[...])
    g_ref[:, 0:128] = gates[:, 0:128]
    g_ref[:, 128:256] = _log_sigmoid(gates[:, 128:256])


def _proj_even(x, mods, gain, w_main, w_gates, conv_w, gate_bias, rows):
    n, d = x.shape
    tm = ROW_TILE
    x8 = x.reshape(n // 8, 8, d)
    nb8 = n // 8
    kern = functools.partial(_proj_even_kernel, tm=tm, n_prompt=rows.n_prompt,
                             prompt_len=rows.prompt_len, sample_len=rows.sample_len)
    return pl.pallas_call(
        kern,
        grid=(n // tm,),
        in_specs=[pl.BlockSpec((tm, d), lambda i: (i, 0)),
                  pl.BlockSpec((1, 8, d), lambda i: (jnp.maximum(i * (tm // 8) - 1, 0), 0, 0)),
                  pl.BlockSpec((1, 8, d), lambda i: (jnp.minimum((i + 1) * (tm // 8), nb8 - 1), 0, 0)),
                  pl.BlockSpec((1, 6, d), lambda i: (rows.cond_of_tile(i, tm), 0, 0)),
                  pl.BlockSpec((1, d), lambda i: (0, 0)),
                  pl.BlockSpec(w_main.shape, lambda i: (0, 0)),
                  pl.BlockSpec(w_gates.shape, lambda i: (0, 0)),
                  pl.BlockSpec(conv_w.shape, lambda i: (0, 0)),
                  pl.BlockSpec(gate_bias.shape, lambda i: (0, 0))],
        out_specs=[pl.BlockSpec((tm, 1536), lambda i: (i, 0)),
                   pl.BlockSpec((tm, 1536), lambda i: (i, 0)),
                   pl.BlockSpec((tm, 1024), lambda i: (i, 0)),
                   pl.BlockSpec((tm, 256), lambda i: (i, 0))],
        out_shape=[jax.ShapeDtypeStruct((n, 1536), BF16),
                   jax.ShapeDtypeStruct((n, 1536), BF16),
                   jax.ShapeDtypeStruct((n, 1024), BF16),
                   jax.ShapeDtypeStruct((n, 256), F32)],
        compiler_params=_params(1),
        name="proj_even",
    )(x, x8, x8, mods, gain, w_main, w_gates, conv_w, gate_bias)


def _scan_kernel(fb_ref, bb_ref, first_ref, last_ref, sin_ref, sout_ref, isp_ref,
                 qaf_ref, qab_ref, qbf_ref, qbb_ref, gf_ref, gb_ref,
                 c0_ref, n0_ref, m0_ref, s0_ref, rl_ref,
                 hf_ref, hb_ref, cn_ref, nn_ref, mn_ref, sn_ref,
                 cext_sc, s_sc, m_sc, intra_sc, inter_sc, toend_sc, cdec_sc, *, L):
    del fb_ref, bb_ref, sin_ref, sout_ref
    step = pl.program_id(0)
    n_units = N_DIR * H_A
    lane = lax.broadcasted_iota(jnp.int32, (HEAD_W, HEAD_W), 1)
    t_idx = lax.broadcasted_iota(jnp.int32, (L, L), 0)
    s_idx = lax.broadcasted_iota(jnp.int32, (L, L), 1)
    masks = (s_idx <= t_idx, s_idx >= t_idx)

    @pl.when(step == 0)
    def _():
        log_gamma = _log_sigmoid(rl_ref[...])
        rel = jnp.abs(t_idx - s_idx).astype(F32)
        pos_col = lax.broadcasted_iota(jnp.int32, (L, HEAD_W), 0).astype(F32)
        for u in range(n_units):
            lg = log_gamma[u:u + 1, 0:1]
            intra_sc[u] = jnp.where(masks[u // H_B], jnp.exp(lg * rel), 0.0)
            pos = pos_col if u < H_B else (L - 1.0) - pos_col
            inter_sc[u] = jnp.exp(lg * (pos + 1.0))
        unit = lax.broadcasted_iota(jnp.int32, (n_units, L), 0)
        pos_row = lax.broadcasted_iota(jnp.int32, (n_units, L), 1).astype(F32)
        pos_row = jnp.where(unit < H_B, pos_row, (L - 1.0) - pos_row)
        toend_sc[...] = jnp.exp(log_gamma[:, 0:1] * ((L - 1.0) - pos_row))
        cdec_sc[...] = jnp.exp(log_gamma * float(L))

    @pl.when(jnp.logical_and(first_ref[step] == 1, isp_ref[step] == 1))
    def _():
        cext_sc[...] = jnp.zeros_like(cext_sc)
        s_sc[...] = jnp.zeros_like(s_sc)
        m_sc[...] = jnp.zeros_like(m_sc)

    @pl.when(jnp.logical_and(first_ref[step] == 1, isp_ref[step] == 0))
    def _():
        n0 = n0_ref[0]
        n0_t = jnp.concatenate([n0, jnp.zeros((HEAD_W - n_units, HEAD_W), F32)], axis=0).T
        for u in range(n_units):
            cext_sc[u, :, 0:HEAD_W] = c0_ref[0, u]
            cext_sc[u, :, HEAD_W:2 * HEAD_W] = jnp.broadcast_to(n0_t[:, u:u + 1], (HEAD_W, HEAD_W))
            s_sc[u] = s0_ref[0, u]
        m_sc[...] = m0_ref[0]

    ones_ext = jnp.ones((L, HEAD_W), BF16)
    row_l = lax.broadcasted_iota(jnp.int32, (L, HEAD_W), 0)
    tri = tuple(jnp.where(m, 1.0, 0.0) for m in masks)
    qa = (qaf_ref, qab_ref)
    qb = (qbf_ref, qbb_ref)
    g = (gf_ref, gb_ref)
    out = (hf_ref, hb_ref)
    m_all = m_sc[...]
    toend = toend_sc[...]
    cdec = cdec_sc[...]
    m_rows = []

    def wide(x):
        return x if L == HEAD_W else jnp.concatenate([x] * (L // HEAD_W), axis=1)

    for d in range(N_DIR):
        ig_all = g[d][:, 0:HEAD_W]
        lf_all = g[d][:, HEAD_W:2 * HEAD_W]
        bt_all = _dot(tri[d], lf_all, precision=_HI)
        a_all = ig_all - bt_all
        a_rows = a_all.T[0:n_units, :]
        cm_all = a_all
        shift = 1
        while shift < L:
            if d == 0:
                moved = jnp.where(row_l >= shift, pltpu.roll(cm_all, shift, axis=0), -jnp.inf)
            else:
                moved = jnp.where(row_l < L - shift, pltpu.roll(cm_all, L - shift, axis=0), -jnp.inf)
            cm_all = jnp.maximum(cm_all, moved)
            shift *= 2
        end = L - 1 if d == 0 else 0
        for h in range(H_A):
            u = d * H_A + h
            cs = slice(h * HEAD_W, (h + 1) * HEAD_W)
            m_prev = m_all[u:u + 1, :]
            g_rep = jnp.maximum(m_prev, jnp.broadcast_to(cm_all[:, u:u + 1], (L, HEAD_W)))
            bt_rep = jnp.broadcast_to(bt_all[:, u:u + 1], (L, HEAD_W))
            a_row = a_rows[u:u + 1, :]
            dmat = jnp.exp(jnp.where(masks[d], a_row - wide(g_rep), -jnp.inf))
            inter_w = jnp.exp(m_prev - g_rep)
            q = qa[d][:, cs]
            k = qa[d][:, 512 + h * HEAD_W:512 + (h + 1) * HEAD_W]
            v = qa[d][:, 1024 + h * HEAD_W:1024 + (h + 1) * HEAD_W]
            v_ext = jnp.concatenate([v, ones_ext], axis=1)
            s = (_dot_nt(q, k) * dmat).astype(BF16)
            lhs = jnp.concatenate([s, (q.astype(F32) * inter_w).astype(BF16)], axis=1)
            rhs = jnp.concatenate([v_ext, cext_sc[u].astype(BF16)], axis=0)
            num = _dot(lhs, rhs)
            inv = 1.0 / jnp.maximum(jnp.abs(num[:, HEAD_W:2 * HEAD_W]), jnp.exp(-(bt_rep + g_rep)))
            out[d][:, cs] = (num[:, 0:HEAD_W] * inv).astype(out[d].dtype)
            m_prev1 = m_prev[:, 0:1]
            g_end = jnp.maximum(m_prev1, cm_all[end:end + 1, u:u + 1])
            wk_row = jnp.exp(a_row - g_end)
            kw = (k.astype(F32).T * wk_row).astype(BF16)
            cext_sc[u] = jnp.exp(m_prev1 - g_end) * cext_sc[u] + _dot(kw, v_ext)
            m_rows.append(jnp.broadcast_to(bt_all[end:end + 1, u:u + 1] + g_end, (1, HEAD_W)))
            qr = qb[d][:, cs]
            kr = qb[d][:, 512 + h * HEAD_W:512 + (h + 1) * HEAD_W]
            vr = qb[d][:, 1024 + h * HEAD_W:1024 + (h + 1) * HEAD_W]
            sr = (_dot_nt(qr, kr) * intra_sc[u]).astype(BF16)
            lhs = jnp.concatenate([sr, (qr.astype(F32) * inter_sc[u]).astype(BF16)], axis=1)
            rhs = jnp.concatenate([vr, s_sc[u].astype(BF16)], axis=0)
            out[d][:, 512 + h * HEAD_W:512 + (h + 1) * HEAD_W] = _dot(lhs, rhs).astype(out[d].dtype)
            krw = (kr.astype(F32).T * toend[u:u + 1, :]).astype(BF16)
            s_sc[u] = cdec[u:u + 1, :] * s_sc[u] + _dot(krw, vr)
    m_sc[...] = jnp.concatenate(m_rows, axis=0)

    @pl.when(last_ref[step] == 1)
    def _():
        n_cols = jnp.zeros((HEAD_W, HEAD_W), F32)
        for u in range(n_units):
            cn_ref[0, u] = cext_sc[u, :, 0:HEAD_W]
            sn_ref[0, u] = s_sc[u]
            n_cols = jnp.where(lane == u, cext_sc[u, :, HEAD_W:HEAD_W + 1], n_cols)
        nn_ref[0] = n_cols.T[0:n_units, :]
        mn_ref[0] = m_sc[...]


def _scan_tables(rows, L):
    fb, bb, first, last, sin, sout, isp = [], [], [], [], [], [], []
    base = 0
    for kind, n_seq, seq_len in (("p", rows.n_prompt_seq, rows.prompt_len),
                                 ("s", rows.n_sample_seq, rows.sample_len)):
        nc = seq_len // L
        for b in range(n_seq):
            for c in range(nc):
                fb.append(base + b * nc + c)
                bb.append(base + b * nc + nc - 1 - c)
                first.append(int(c == 0))
                last.append(int(c == nc - 1 and kind == "p"))
                sin.append(b if kind == "s" else 0)
                sout.append(b if kind == "p" else rows.n_prompt_seq - 1)
                isp.append(int(kind == "p"))
        base += n_seq * nc
    return [jnp.asarray(np.asarray(t, np.int32)) for t in (fb, bb, first, last, sin, sout, isp)]


def _scan_even(qkva, qkvb, gates, c0, n0, m0, s0, ret_logit, rows):
    L = SCAN_CHUNK
    n = qkva.shape[0]
    tables = _scan_tables(rows, L)
    n_steps = int(tables[0].shape[0])
    nu = N_DIR * H_A
    fwd = lambda w: pl.BlockSpec((L, w), lambda s, fb, bb, fi, la, si, so, ip: (fb[s], 0))
    bwd = lambda w: pl.BlockSpec((L, w), lambda s, fb, bb, fi, la, si, so, ip: (bb[s], 0))
    st_in4 = pl.BlockSpec((1, nu, HEAD_W, HEAD_W), lambda s, fb, bb, fi, la, si, so, ip: (si[s], 0, 0, 0))
    st_in3 = pl.BlockSpec((1, nu, HEAD_W), lambda s, fb, bb, fi, la, si, so, ip: (si[s], 0, 0))
    st_out4 = pl.BlockSpec((1, nu, HEAD_W, HEAD_W), lambda s, fb, bb, fi, la, si, so, ip: (so[s], 0, 0, 0))
    st_out3 = pl.BlockSpec((1, nu, HEAD_W), lambda s, fb, bb, fi, la, si, so, ip: (so[s], 0, 0))
    nps = rows.n_prompt_seq
    return pl.pallas_call(
        functools.partial(_scan_kernel, L=L),
        grid_spec=pltpu.PrefetchScalarGridSpec(
            num_scalar_prefetch=7,
            grid=(n_steps,),
            in_specs=[fwd(1536), bwd(1536), fwd(1536), bwd(1536), fwd(256), bwd(256),
                      st_in4, st_in3, st_in3, st_in4,
                      pl.BlockSpec((nu, HEAD_W), lambda s, *_: (0, 0))],
            out_specs=[fwd(1024), bwd(1024), st_out4, st_out3, st_out3, st_out4],
            scratch_shapes=[pltpu.VMEM((nu, HEAD_W, 2 * HEAD_W), F32),
                            pltpu.VMEM((nu, HEAD_W, HEAD_W), F32),
                            pltpu.VMEM((nu, HEAD_W), F32),
                            pltpu.VMEM((nu, L, L), F32),
                            pltpu.VMEM((nu, L, HEAD_W), F32),
                            pltpu.VMEM((nu, L), F32),
                            pltpu.VMEM((nu, HEAD_W), F32)]),
        out_shape=[jax.ShapeDtypeStruct((n, 1024), BF16),
                   jax.ShapeDtypeStruct((n, 1024), BF16),
                   jax.ShapeDtypeStruct((nps, nu, HEAD_W, HEAD_W), F32),
                   jax.ShapeDtypeStruct((nps, nu, HEAD_W), F32),
                   jax.ShapeDtypeStruct((nps, nu, HEAD_W), F32),
                   jax.ShapeDtypeStruct((nps, nu, HEAD_W, HEAD_W), F32)],
        compiler_params=_params(1),
        name="scan_even",
    )(*tables, qkva, qkva, qkvb, qkvb, gates, gates, c0, n0, m0, s0, ret_logit)


def _proj_odd_kernel(x_ref, mod_ref, gain_ref, w_ref, e_ref, qg_ref, kg_ref, cos_ref, sin_ref,
                     q_ref, k_ref, v_ref, *, rope):
    shift = mod_ref[0, 0:1, :]
    scale = mod_ref[0, 1:2, :]
    hb = (_rms(x_ref[...]) * gain_ref[...] * (1.0 + scale) + shift).astype(BF16)

    def qk_norm(raw, g):
        ss = _dot((raw * raw).astype(BF16), e_ref[...])
        return raw * lax.rsqrt(ss * (1.0 / DH_C) + EPS) * g

    def rotate(y):
        if not rope:
            return y
        lane = lax.broadcasted_iota(jnp.int32, y.shape, 1)
        first_half = (lane & 31) < 16
        partner = jnp.where(first_half, pltpu.roll(y, y.shape[1] - 16, axis=1), pltpu.roll(y, 16, axis=1))
        cos = jnp.concatenate([cos_ref[...]] * (y.shape[1] // HEAD_W), axis=1)
        sin = jnp.concatenate([sin_ref[...]] * (y.shape[1] // HEAD_W), axis=1)
        return y * cos + partner * sin

    q = rotate(qk_norm(_dot(hb, w_ref[:, 0:1024]), qg_ref[...]))
    q_ref[...] = (q * (DH_C ** -0.5 * math.log2(math.e))).astype(q_ref.dtype)
    k = rotate(qk_norm(_dot(hb, w_ref[:, 1024:2048]), kg_ref[...]))
    k_ref[...] = k.astype(k_ref.dtype)
    v_ref[...] = _dot(hb, w_ref[:, 2048:3072]).astype(v_ref.dtype)


def _proj_odd(x, mods, gain, w_qkv, e64, q_gain, k_gain, cos, sin, rows, *, sample):
    d = x.shape[1]
    tm = ROW_TILE
    if sample:
        n, base, kv_dtype = rows.n_sample, rows.n_prompt // tm, BF16
        per_seq = rows.sample_len // tm
        table = lambda i: (i % per_seq, 0)
    else:
        n, base, kv_dtype = rows.n_prompt, 0, F32
        table = lambda i: (0, 0)
    return pl.pallas_call(
        functools.partial(_proj_odd_kernel, rope=sample),
        grid=(n // tm,),
        in_specs=[pl.BlockSpec((tm, d), lambda i: (base + i, 0)),
                  pl.BlockSpec((1, 6, d), lambda i: (rows.cond_of_tile(base + i, tm), 0, 0)),
                  pl.BlockSpec((1, d), lambda i: (0, 0)),
                  pl.BlockSpec(w_qkv.shape, lambda i: (0, 0)),
                  pl.BlockSpec(e64.shape, lambda i: (0, 0)),
                  pl.BlockSpec((1, d), lambda i: (0, 0)),
                  pl.BlockSpec((1, d), lambda i: (0, 0)),
                  pl.BlockSpec((tm, HEAD_W), table),
                  pl.BlockSpec((tm, HEAD_W), table)],
        out_specs=[pl.BlockSpec((tm, d), lambda i: (i, 0))] * 3,
        out_shape=[jax.ShapeDtypeStruct((n, d), BF16),
                   jax.ShapeDtypeStruct((n, d), kv_dtype),
                   jax.ShapeDtypeStruct((n, d), kv_dtype)],
        compiler_params=_params(1),
        name="proj_odd_sample" if sample else "proj_odd_prompt",
    )(x, mods, gain, w_qkv, e64, q_gain, k_gain, cos, sin)


def _rope_tables(sample_len):
    t = np.arange(sample_len)
    nf = DH_C // 4
    inv = ROPE_BASE ** (-np.arange(nf, dtype=np.float32) / nf)
    row = (t // GRID_W).astype(np.float32)[:, None] * inv[None, :]
    col = (t % GRID_W).astype(np.float32)[:, None] * inv[None, :]
    cos64 = np.concatenate([np.cos(row), np.cos(row), np.cos(col), np.cos(col)], axis=1)
    sin64 = np.concatenate([-np.sin(row), np.sin(row), -np.sin(col), np.sin(col)], axis=1)
    cos = np.concatenate([cos64, cos64], axis=1).astype(np.float32)
    sin = np.concatenate([sin64, sin64], axis=1).astype(np.float32)
    return jnp.asarray(cos), jnp.asarray(sin)


def _attn_kernel(*refs, tq, tk, n_main, n_cache, lam_init, hps):
    if n_cache:
        q_ref, k_ref, v_ref, ck_ref, cv_ref, lq_ref, og_ref, o_ref, vt_sc = refs
    else:
        q_ref, k_ref, v_ref, lq_ref, og_ref, o_ref, vt_sc = refs
    fill_w = 256

    @pl.when(pl.program_id(2) == 0)
    def _():
        ones_rows = jnp.where(lax.broadcasted_iota(jnp.int32, (8, fill_w), 0) == 0, 1.0, 0.0).astype(BF16)
        for hh in range(hps):
            hs = slice(hh * HEAD_W, (hh + 1) * HEAD_W)
            for j in range(n_main // fill_w):
                vt = v_ref[0, j * fill_w:(j + 1) * fill_w, hs].astype(F32).T.astype(BF16)
                vt_sc[hh, :, j * fill_w:(j + 1) * fill_w] = jnp.concatenate([vt, ones_rows], axis=0)
            if n_cache:
                vt = cv_ref[0, :, hs].astype(F32).T.astype(BF16)
                vt_sc[hh, :, n_main:n_main + n_cache] = jnp.concatenate([vt, ones_rows], axis=0)

    lq = lq_ref[...]
    lam = (jnp.exp(jnp.sum(lq[0:1, :] * lq[1:2, :], axis=1, keepdims=True))
           - jnp.exp(jnp.sum(lq[2:3, :] * lq[3:4, :], axis=1, keepdims=True)) + lam_init)
    for hh in range(hps):
        _attn_head(hh, q_ref, k_ref, ck_ref if n_cache else None, vt_sc, og_ref, o_ref, lam,
                   tq=tq, tk=tk, n_main=n_main, n_cache=n_cache, lam_init=lam_init)


def _attn_head(hh, q_ref, k_ref, ck_ref, vt_sc, og_ref, o_ref, lam, *, tq, tk, n_main, n_cache, lam_init):
    hs = slice(hh * HEAD_W, (hh + 1) * HEAD_W)
    qt = q_ref[0, :, hs].astype(F32).T
    row = lax.broadcasted_iota(jnp.int32, qt.shape, 0)
    qst = jnp.concatenate([jnp.where(row < DH_C, qt, 0.0), jnp.where(row >= DH_C, qt, 0.0)],
                          axis=1).astype(BF16)

    def scores(kt):
        return _dot(kt.astype(BF16), qst)

    def accumulate(carry, s, vt):
        m, acc = carry
        m_new = jnp.maximum(m, jnp.max(s, axis=0, keepdims=True))
        p = jnp.exp2(s - m_new).astype(BF16)
        acc = jnp.exp2(m - m_new) * acc + _dot(vt, p)
        return m_new, acc

    tiles = [(k_ref, j * tk, tk, j * tk) for j in range(n_main // tk)]
    if n_cache:
        tiles = [(ck_ref, 0, n_cache, n_main)] + tiles
    carry = (jnp.full((1, 2 * tq), -jnp.inf, F32), jnp.zeros((HEAD_W + 8, 2 * tq), F32))
    ref0, r0, n0, _ = tiles[0]
    s_next = scores(ref0[0, r0:r0 + n0, hs])
    for j, (_, _, n_keys, c0) in enumerate(tiles):
        s_cur = s_next
        if j + 1 < len(tiles):
            ref1, r1, n1, _ = tiles[j + 1]
            s_next = scores(ref1[0, r1:r1 + n1, hs])
        carry = accumulate(carry, s_cur, vt_sc[hh, :, c0:c0 + n_keys])
    _, acc = carry
    o_t = acc[0:HEAD_W, :] / acc[HEAD_W:HEAD_W + 1, :]
    o = (o_t[:, 0:tq] - lam * o_t[:, tq:2 * tq]).T
    o_ref[0, :, hs] = (_rms(o) * og_ref[...] * (1.0 - lam_init)).astype(o_ref.dtype)


def _attention(q, k, v, cache_k, cache_v, lam_qk, out_gain, lam_init):
    b, t, d = q.shape
    tq = min(ATTN_TQ, t)
    tk = min(ATTN_TK, t)
    n_cache = 0 if cache_k is None else cache_k.shape[1]
    assert t % tk == 0 and t % tq == 0 and t % 256 == 0 and n_cache in (0, 256)
    hps = H_C if t + n_cache <= 512 else 1
    hw = hps * HEAD_W
    seq = lambda n: pl.BlockSpec((1, n, hw), lambda bi, h, qi: (bi, 0, h))
    in_specs = [pl.BlockSpec((1, tq, hw), lambda bi, h, qi: (bi, qi, h)), seq(t), seq(t)]
    args = [q, k, v]
    if n_cache:
        in_specs += [seq(n_cache), seq(n_cache)]
        args += [cache_k, cache_v]
    in_specs += [pl.BlockSpec(lam_qk.shape, lambda bi, h, qi: (0, 0)),
                 pl.BlockSpec(out_gain.shape, lambda bi, h, qi: (0, 0))]
    args += [lam_qk, out_gain]
    return pl.pallas_call(
        functools.partial(_attn_kernel, tq=tq, tk=tk, n_main=t, n_cache=n_cache, lam_init=lam_init, hps=hps),
        grid=(b, H_C // hps, t // tq),
        in_specs=in_specs,
        out_specs=pl.BlockSpec((1, tq, hw), lambda bi, h, qi: (bi, qi, h)),
        out_shape=jax.ShapeDtypeStruct((b, t, d), BF16),
        scratch_shapes=[pltpu.VMEM((hps, HEAD_W + 8, t + n_cache), BF16)],
        compiler_params=_params(3),
        name="diff_attention",
    )(*args)


def _out_kernel(*refs, n_tiles, **static):
    ys0_ref = refs[-4]
    ys0_ref[...] = jnp.zeros_like(ys0_ref)

    @pl.when(pl.program_id(0) < n_tiles)
    def _():
        _out_tile(*refs, **static)


def _out_tile(*refs, tm, even, n_prompt_tiles):
    if even:
        (hf_ref, hb_ref, oz_ref, ga_ref, gb_ref, x_ref, mod_ref, w_ref, fg_ref, wr_ref, rb_ref,
         x1_ref, xs_ref, ys0_ref, sl_ref, gw_ref, cnt_ref) = refs
    else:
        (op_ref, os_ref, x_ref, mod_ref, w_ref, fg_ref, wr_ref, rb_ref,
         x1_ref, xs_ref, ys0_ref, sl_ref, gw_ref, cnt_ref) = refs
    del ys0_ref
    i = pl.program_id(0)

    if even:
        hs = hf_ref[...].astype(F32) + hb_ref[...].astype(F32)
        oz = oz_ref[...].astype(F32)
        parts = []
        for h in range(H_A + H_B):
            cs = slice(h * HEAD_W, (h + 1) * HEAD_W)
            gain = ga_ref[:, cs] if h < H_A else gb_ref[:, (h - H_A) * HEAD_W:(h - H_A + 1) * HEAD_W]
            act = _sigmoid(oz[:, cs]) if h < H_A else _silu(oz[:, cs])
            parts.append((_rms(hs[:, cs]) * gain * act).astype(BF16))
        y_in = jnp.concatenate(parts, axis=1)
    else:
        y_in = jnp.where(i < n_prompt_tiles, op_ref[...], os_ref[...])

    x1 = x_ref[...] + mod_ref[0, 2:3, :] * _dot(y_in, w_ref[...])
    x1_ref[...] = x1
    h2 = _rms(x1) * fg_ref[...] * (1.0 + mod_ref[0, 4:5, :]) + mod_ref[0, 3:4, :]
    h2b = h2.astype(BF16)

    score = _sigmoid(_dot(h2b, wr_ref[...]))
    st = score.T[0:N_EXPERTS, :]
    sel = st + rb_ref[:, 0:1]
    srow = [st[e:e + 1, :] for e in range(N_EXPERTS)]
    brow = [sel[e:e + 1, :] for e in range(N_EXPERTS)]
    epg = EXPERTS_PER_GROUP
    gscore = []
    for g in range(N_GROUPS):
        a = brow[g * epg:(g + 1) * epg]
        best = None
        for p in range(epg):
            for r in range(p + 1, epg):
                pair = a[p] + a[r]
                best = pair if best is None else jnp.maximum(best, pair)
        gscore.append(best)
    gbest, gidx = gscore[0], jnp.zeros((1, tm), jnp.int32)
    for g in range(1, N_GROUPS):
        better = gscore[g] > gbest
        gbest = jnp.where(better, gscore[g], gbest)
        gidx = jnp.where(better, g, gidx)
    vals, sig = [], []
    for p in range(epg):
        vp, sp = brow[p], srow[p]
        for g in range(1, N_GROUPS):
            vp = jnp.where(gidx == g, brow[g * epg + p], vp)
            sp = jnp.where(gidx == g, srow[g * epg + p], sp)
        vals.append(vp)
        sig.append(sp)
    v1, i1, w1 = vals[0], jnp.zeros((1, tm), jnp.int32), sig[0]
    for p in range(1, epg):
        better = vals[p] > v1
        v1 = jnp.where(better, vals[p], v1)
        i1 = jnp.where(better, p, i1)
        w1 = jnp.where(better, sig[p], w1)
    v2 = jnp.full((1, tm), -jnp.inf, F32)
    i2 = jnp.zeros((1, tm), jnp.int32)
    w2 = jnp.zeros((1, tm), F32)
    for p in range(epg):
        better = jnp.logical_and(i1 != p, vals[p] > v2)
        v2 = jnp.where(better, vals[p], v2)
        i2 = jnp.where(better, p, i2)
        w2 = jnp.where(better, sig[p], w2)
    e1 = gidx * epg + i1
    e2 = gidx * epg + i2
    wsum = w1 + w2

    erow = lax.broadcasted_iota(jnp.int32, (N_EXPERTS, tm), 0)
    earlier = jnp.where(lax.broadcasted_iota(jnp.int32, (tm, tm), 0)
                        < lax.broadcasted_iota(jnp.int32, (tm, tm), 1), 1.0, 0.0).astype(BF16)
    oh1 = jnp.where(erow == e1, 1.0, 0.0)
    oh2 = jnp.where(erow == e2, 1.0, 0.0)
    before1 = _dot(oh1.astype(BF16), earlier)
    before2 = _dot(oh2.astype(BF16), earlier)
    cnt1 = jnp.sum(oh1, axis=1, keepdims=True)
    cnt = cnt1 + jnp.sum(oh2, axis=1, keepdims=True)
    padded = jnp.floor((cnt + (MOE_CHUNK - 1.0)) * (1.0 / MOE_CHUNK)) * MOE_CHUNK
    below = jnp.where(lax.broadcasted_iota(jnp.int32, (N_EXPERTS, N_EXPERTS), 1)
                      < lax.broadcasted_iota(jnp.int32, (N_EXPERTS, N_EXPERTS), 0), 1.0, 0.0)
    start = _dot(below, jnp.broadcast_to(padded, (N_EXPERTS, HEAD_W)), precision=_HI)[:, 0:1]
    slot1 = jnp.sum(oh1 * (start + before1), axis=0, keepdims=True).astype(jnp.int32)
    slot2 = jnp.sum(oh2 * (start + cnt1 + before2), axis=0, keepdims=True).astype(jnp.int32)
    local = lax.broadcasted_iota(jnp.int32, (MOE_TILE_ROWS, tm), 0)
    place = jnp.where(jnp.logical_or(local == slot1, local == slot2), 1.0, 0.0).astype(BF16)
    xs_ref[...] = _dot(place, h2b).astype(BF16)
    zi = jnp.zeros((6, tm), jnp.int32)
    sl_ref[...] = jnp.concatenate([slot1, slot2, zi], axis=0)
    gw_ref[...] = jnp.concatenate([w1 / wsum, w2 / wsum, jnp.zeros((6, tm), F32)], axis=0)
    cnt_ref[...] = jnp.broadcast_to(cnt, (N_EXPERTS, HEAD_W))


def _out_and_route(mix_in, x, mods, w_out, ffn_gain, w_router, router_bias, rows, *, even):
    n, d = x.shape
    tm = ROW_TILE
    n_tiles = n // tm
    n_pt = rows.n_prompt // tm
    tile = lambda i: jnp.minimum(i, n_tiles - 1)
    row = lambda w: pl.BlockSpec((tm, w), lambda i: (tile(i), 0))
    full = lambda a: pl.BlockSpec(a.shape, lambda i: (0,) * a.ndim)
    if even:
        hf, hb, oz, gain_a, gain_b = mix_in
        in_specs = [row(1024), row(1024), row(1024), full(gain_a), full(gain_b)]
        args = [hf, hb, oz, gain_a, gain_b]
    else:
        o_p, o_s = mix_in
        in_specs = [pl.BlockSpec((tm, d), lambda i: (jnp.minimum(i, n_pt - 1), 0)),
                    pl.BlockSpec((tm, d), lambda i: (jnp.maximum(tile(i) - n_pt, 0), 0))]
        args = [o_p, o_s]
    in_specs += [row(d), pl.BlockSpec((1, 6, d), lambda i: (rows.cond_of_tile(tile(i), tm), 0, 0)),
                 full(w_out), full(ffn_gain), full(w_router), full(router_bias)]
    args += [x, mods, w_out, ffn_gain, w_router, router_bias]
    col = lambda: pl.BlockSpec((8, tm), lambda i: (0, tile(i)))
    return pl.pallas_call(
        functools.partial(_out_kernel, n_tiles=n_tiles, tm=tm, even=even, n_prompt_tiles=n_pt),
        grid=(n_tiles + MOE_SCRATCH_TILES,),
        in_specs=in_specs,
        out_specs=[row(d), pl.BlockSpec((MOE_TILE_ROWS, d), lambda i: (tile(i), 0)),
                   pl.BlockSpec((MOE_TILE_ROWS, d), lambda i: (i, 0)), col(), col(),
                   pl.BlockSpec((N_EXPERTS, HEAD_W), lambda i: (tile(i), 0))],
        out_shape=[jax.ShapeDtypeStruct((n, d), F32),
                   jax.ShapeDtypeStruct((n_tiles * MOE_TILE_ROWS, d), BF16),
                   jax.ShapeDtypeStruct(((n_tiles + MOE_SCRATCH_TILES) * MOE_TILE_ROWS, d), BF16),
                   jax.ShapeDtypeStruct((8, n), jnp.int32),
                   jax.ShapeDtypeStruct((8, n), F32),
                   jax.ShapeDtypeStruct((n_tiles * N_EXPERTS, HEAD_W), F32)],
        compiler_params=_params(1),
        name="out_even" if even else "out_odd",
    )(*args)


def _moe_kernel(cin_ref, cout_ref, be_ref, nu_ref, xs_hbm, ys_init_hbm, wg_ref, wu_ref, wd_ref, ys_hbm,
                xbuf, ybuf, wg_sc, wu_sc, wd_sc, in_sem, out_sem):
    del ys_init_hbm
    b = pl.program_id(0)
    n_used = nu_ref[0]
    cb, ch = MOE_BLOCK_CHUNKS, MOE_CHUNK

    def in_copy(blk, slot, c):
        src = pl.multiple_of(cin_ref[blk * cb + c] * ch, ch)
        return pltpu.make_async_copy(xs_hbm.at[pl.ds(src, ch), :], xbuf.at[slot, pl.ds(c * ch, ch), :],
                                     in_sem.at[slot])

    def out_copy(blk, slot, c):
        dst = pl.multiple_of(cout_ref[blk * cb + c] * ch, ch)
        return pltpu.make_async_copy(ybuf.at[slot, pl.ds(c * ch, ch), :], ys_hbm.at[pl.ds(dst, ch), :],
                                     out_sem.at[slot])

    def start_in(blk, slot):
        for c in range(cb):
            in_copy(blk, slot, c).start()

    def wait_out(blk, slot):
        for c in range(cb):
            out_copy(blk, slot, c).wait()

    slot = b % 2

    @pl.when(b == 0)
    def _():
        start_in(0, 0)

    @pl.when(b + 1 < n_used)
    def _():
        start_in(b + 1, 1 - slot)

    @pl.when(b < n_used)
    def _():
        for c in range(cb):
            in_copy(b, slot, c).wait()

        @pl.when(b >= 2)
        def _():
            wait_out(b - 2, slot)

        @pl.when(jnp.logical_or(b == 0, be_ref[b] != be_ref[jnp.maximum(b - 1, 0)]))
        def _():
            wg_sc[...] = wg_ref[0, 0].astype(BF16)
            wu_sc[...] = wu_ref[0, 0].astype(BF16)
            wd_sc[...] = wd_ref[0, 0].astype(BF16)

        x = xbuf[slot]
        a = (_silu(_dot(x, wg_sc[...])) * _dot(x, wu_sc[...])).astype(BF16)
        ybuf[slot] = _dot(a, wd_sc[...]).astype(BF16)
        for c in range(cb):
            out_copy(b, slot, c).start()

    @pl.when(b == pl.num_programs(0) - 1)
    def _():
        @pl.when(n_used >= 2)
        def _():
            wait_out(n_used - 2, n_used % 2)

        wait_out(n_used - 1, (n_used - 1) % 2)


def _moe_experts(xs_local, ys_init, chunk_in, chunk_out, block_e, n_used, w_gate, w_up, w_down, layer):
    d = xs_local.shape[1]
    n_blocks = chunk_in.shape[0] // MOE_BLOCK_CHUNKS
    wspec = pl.BlockSpec((1, 1, d, d), lambda i, ci, co, be, nu: (layer, be[i], 0, 0))
    hbm = pl.BlockSpec(memory_space=pl.ANY)
    return pl.pallas_call(
        _moe_kernel,
        grid_spec=pltpu.PrefetchScalarGridSpec(
            num_scalar_prefetch=4,
            grid=(n_blocks,),
            in_specs=[hbm, hbm, wspec, wspec, wspec],
            out_specs=hbm,
            scratch_shapes=[pltpu.VMEM((2, MOE_BLOCK, d), BF16), pltpu.VMEM((2, MOE_BLOCK, d), BF16),
                            pltpu.VMEM((d, d), BF16), pltpu.VMEM((d, d), BF16), pltpu.VMEM((d, d), BF16),
                            pltpu.SemaphoreType.DMA((2,)), pltpu.SemaphoreType.DMA((2,))]),
        out_shape=jax.ShapeDtypeStruct(ys_init.shape, BF16),
        input_output_aliases={5: 0},
        compiler_params=_params(1),
        name="moe_experts",
    )(chunk_in, chunk_out, block_e, n_used, xs_local, ys_init, w_gate, w_up, w_down)


def _combine_kernel(x_ref, ys_ref, sl_ref, gw_ref, mod_ref, o_ref):
    sl = sl_ref[...]
    w = gw_ref[...]
    ys = ys_ref[...]
    local = lax.broadcasted_iota(jnp.int32, (sl.shape[0], ys.shape[0]), 1)
    pick = (jnp.where(local == sl[:, 0:1], w[:, 0:1], 0.0)
            + jnp.where(local == sl[:, 1:2], w[:, 1:2], 0.0)).astype(BF16)
    o_ref[...] = x_ref[...] + mod_ref[0, 5:6, :] * _dot(pick, ys)


def _combine(x1, ys_local, slot_cols, gw_cols, mods, rows):
    n, d = x1.shape
    tm = ROW_TILE
    return pl.pallas_call(
        _combine_kernel,
        grid=(n // tm,),
        in_specs=[pl.BlockSpec((tm, d), lambda i: (i, 0)),
                  pl.BlockSpec((MOE_TILE_ROWS, d), lambda i: (i, 0)),
                  pl.BlockSpec((tm, 2), lambda i: (i, 0)),
                  pl.BlockSpec((tm, 2), lambda i: (i, 0)),
                  pl.BlockSpec((1, 6, d), lambda i: (rows.cond_of_tile(i, tm), 0, 0))],
        out_specs=pl.BlockSpec((tm, d), lambda i: (i, 0)),
        out_shape=jax.ShapeDtypeStruct((n, d), F32),
        compiler_params=_params(1),
        name="moe_combine",
    )(x1, ys_local, slot_cols, gw_cols, mods)


def _chunk_tables(counts):
    n_tiles = counts.shape[0]
    cb, tc = MOE_BLOCK_CHUNKS, MOE_TILE_CHUNKS
    n_list = (n_tiles * (tc - 1) + N_EXPERTS * (cb - 1) + cb - 1) // cb * cb
    nch = (counts + MOE_CHUNK - 1) // MOE_CHUNK
    first = jnp.cumsum(nch, axis=1) - nch
    total = jnp.sum(nch, axis=0)
    padded = (total + cb - 1) // cb * cb
    e_end = jnp.cumsum(padded)
    e_start = e_end - padded
    before = (jnp.cumsum(nch, axis=0) - nch).T
    seg_start = (e_start[:, None] + before).reshape(-1)
    seg_len = nch.T.reshape(-1)
    seg_first = (jnp.arange(n_tiles, dtype=jnp.int32)[None, :] * tc + first.T).reshape(-1)
    j = jnp.arange(n_list, dtype=jnp.int32)
    seg = jnp.clip(jnp.sum((seg_start[None, :] <= j[:, None]).astype(jnp.int32), axis=1) - 1,
                   0, seg_start.shape[0] - 1)
    e_j = seg // n_tiles
    assert n_list < (1 << 13) and tc < (1 << 6) and n_tiles * tc < (1 << 13)
    packed = (seg_start.astype(jnp.uint32) | (seg_len.astype(jnp.uint32) << 13)
              | (seg_first.astype(jnp.uint32) << 19))[seg]
    seg_off = j - (packed & 0x1FFF).astype(jnp.int32)
    valid = jnp.logical_and(seg_off < ((packed >> 13) & 0x3F).astype(jnp.int32), j < e_end[-1])
    src = (packed >> 19).astype(jnp.int32) + seg_off
    experts = jnp.arange(N_EXPERTS, dtype=jnp.int32)
    real_end = jnp.sum(jnp.where(e_j[:, None] == experts[None, :], (e_start + total)[None, :], 0), axis=1)
    scratch = n_tiles * tc + e_j * cb + jnp.clip(j - real_end, 0, cb - 1)
    chunk_in = jnp.where(valid, src, tc - 1).astype(jnp.int32)
    chunk_out = jnp.where(valid, src, scratch).astype(jnp.int32)
    block_e = e_j[::cb]
    n_used = (e_end[-1] // cb).astype(jnp.int32).reshape(1)
    return chunk_in, chunk_out, block_e, n_used


def _moe_layer(x1, xs_local, ys_zero, slots, gw, counts, mods, w_gate, w_up, w_down, layer, rows):
    n_tiles = x1.shape[0] // ROW_TILE
    counts = counts[:, 0].astype(jnp.int32).reshape(n_tiles, N_EXPERTS)
    chunk_in, chunk_out, block_e, n_used = _chunk_tables(counts)
    ys_local = _moe_experts(xs_local, ys_zero, chunk_in, chunk_out, block_e, n_used, w_gate, w_up, w_down, layer)
    return _combine(x1, ys_local, slots[0:2].T, gw[0:2].T, mods, rows)


def kernel(x_prompt, x_sample, c, state_mlstm_C, state_mlstm_n, state_mlstm_m, state_ret_S, cache_k, cache_v, c_ctx, w_ada, b_ada, norm_mix_gain, norm_ffn_gain, w_in_even, mlstm_conv, mlstm_gate_bias, mlstm_out_gain, ret_decay_logit, ret_out_gain, w_out_even, w_qkv_odd, q_norm_gain, k_norm_gain, lambda_qk, attn_out_gain, w_out_odd, w_router, router_bias, moe_w_gate, moe_w_up, moe_w_down):
    bp, seq, d = x_prompt.shape
    bs, dec_seq, _ = x_sample.shape
    depth = w_ada.shape[0]
    past = cache_k.shape[2]
    rows = _Rows(bp, seq, bs, dec_seq)
    nu = N_DIR * H_A
    assert 1 + bs <= N_COND_PAD

    x = jnp.concatenate([x_prompt.reshape(bp * seq, d), x_sample.reshape(bs * dec_seq, d)], axis=0)
    cond = jnp.concatenate([c_ctx[None, :], c, jnp.zeros((N_COND_PAD - 1 - bs, d), F32)], axis=0)
    mods_all = _modulation_all(cond, w_ada, b_ada).reshape(depth, N_COND_PAD, 6, d)

    w_router_pad = jnp.pad(w_router, ((0, 0), (0, HEAD_W - N_EXPERTS))).astype(BF16)
    router_bias_col = jnp.broadcast_to(router_bias[:, None], (N_EXPERTS, HEAD_W))
    e64 = jnp.asarray(np.kron(np.eye(d // DH_C, dtype=np.float32), np.ones((DH_C, DH_C), np.float32)), BF16)
    cos, sin = _rope_tables(dec_seq)

    st_c, st_n, st_m, st_s, st_k, st_v = [], [], [], [], [], []
    for l in range(depth):
        j = l // 2
        mods = mods_all[l]
        gain_mix = norm_mix_gain[l][None, :]
        if l % 2 == 0:
            w = w_in_even[j]
            wa = 4 * HEAD_W * 4
            gcols = 4 * H_A
            ob = wa + gcols
            w_main = jnp.concatenate([w[:, 0:1536], w[:, ob:ob + 1536], w[:, 1536:2048],
                                      w[:, ob + 1536:ob + 2048]], axis=1).astype(BF16)
            w_g = w[:, wa:wa + gcols]
            zpad = jnp.zeros((d, HEAD_W - nu), F32)
            w_g32 = jnp.concatenate([w_g[:, 0:nu], zpad, w_g[:, nu:2 * nu], zpad], axis=1)
            w_g_hi = w_g32.astype(BF16)
            w_gates = jnp.concatenate([w_g_hi, (w_g32 - w_g_hi.astype(F32)).astype(BF16)], axis=1)
            gb = mlstm_gate_bias[j].reshape(2, nu)
            zb = jnp.zeros((HEAD_W - nu,), F32)
            gate_bias = jnp.concatenate([gb[0], zb, gb[1], zb])[None, :]
            qkva, qkvb, oz, gates = _proj_even(x, mods, gain_mix, w_main, w_gates, mlstm_conv[j], gate_bias, rows)
            c0 = state_mlstm_C[:, j].reshape(bs, nu, HEAD_W, HEAD_W)
            n0 = state_mlstm_n[:, j].reshape(bs, nu, HEAD_W)
            m0 = jnp.broadcast_to(state_mlstm_m[:, j].reshape(bs, nu, 1), (bs, nu, HEAD_W))
            s0 = state_ret_S[:, j].reshape(bs, nu, HEAD_W, HEAD_W)
            ret_logit = jnp.broadcast_to(ret_decay_logit[j].reshape(nu, 1), (nu, HEAD_W))
            hf, hb, cn, nn, mn, sn = _scan_even(qkva, qkvb, gates, c0, n0, m0, s0, ret_logit, rows)
            st_c.append(cn.reshape(bp, N_DIR, H_A, HEAD_W, HEAD_W))
            st_n.append(nn.reshape(bp, N_DIR, H_A, HEAD_W))
            st_m.append(mn[:, :, 0].reshape(bp, N_DIR, H_A))
            st_s.append(sn.reshape(bp, N_DIR, H_B, HEAD_W, HEAD_W))
            mix_in = (hf, hb, oz, mlstm_out_gain[j].reshape(1, H_A * HEAD_W),
                      ret_out_gain[j].reshape(1, H_B * HEAD_W))
            w_out = w_out_even[j].astype(BF16)
        else:
            lam_init = 0.8 - 0.6 * math.exp(-0.3 * l)
            w_qkv = w_qkv_odd[j].astype(BF16)
            qg = jnp.tile(q_norm_gain[j], d // DH_C)[None, :]
            kg = jnp.tile(k_norm_gain[j], d // DH_C)[None, :]
            og = attn_out_gain[j][None, :]
            q_p, k_p, v_p = _proj_odd(x, mods, gain_mix, w_qkv, e64, qg, kg, cos, sin, rows, sample=False)
            q_s, k_s, v_s = _proj_odd(x, mods, gain_mix, w_qkv, e64, qg, kg, cos, sin, rows, sample=True)
            o_p = _attention(q_p.reshape(bp, seq, d), k_p.reshape(bp, seq, d), v_p.reshape(bp, seq, d),
                             None, None, lambda_qk[j], og, lam_init)
            o_s = _attention(q_s.reshape(bs, dec_seq, d), k_s.reshape(bs, dec_seq, d), v_s.reshape(bs, dec_seq, d),
                             cache_k[:, j].reshape(bs, past, d), cache_v[:, j].reshape(bs, past, d),
                             lambda_qk[j], og, lam_init)
            st_k.append(k_p.reshape(bp, seq, H_C, 2, DH_C))
            st_v.append(v_p.reshape(bp, seq, H_C, 2 * DH_C))
            mix_in = (o_p.reshape(bp * seq, d), o_s.reshape(bs * dec_seq, d))
            w_out = w_out_odd[j].astype(BF16)
        x1, xs_local, ys_zero, slots, gw, counts = _out_and_route(
            mix_in, x, mods, w_out, norm_ffn_gain[l][None, :], w_router_pad, router_bias_col, rows,
            even=(l % 2 == 0))
        x = _moe_layer(x1, xs_local, ys_zero, slots, gw, counts, mods, moe_w_gate, moe_w_up, moe_w_down, l, rows)

    dt = x_prompt.dtype
    y_prompt = x[:rows.n_prompt].reshape(bp, seq, d)
    y_sample = x[rows.n_prompt:].reshape(bs, dec_seq, d)
    return (y_prompt, y_sample,
            jnp.stack(st_c, axis=1).astype(dt), jnp.stack(st_n, axis=1).astype(dt),
            jnp.stack(st_m, axis=1).astype(dt), jnp.stack(st_s, axis=1).astype(dt),
            jnp.stack(st_k, axis=1).astype(dt), jnp.stack(st_v, axis=1).astype(dt))
```

```python
import functools
import math

import numpy as np
import jax
import jax.numpy as jnp
from jax import lax
from jax.experimental import pallas as pl
from jax.experimental.pallas import tpu as pltpu

F32 = jnp.float32
BF16 = jnp.bfloat16

EPS = 1e-6
GRID_W = 64
ROPE_BASE = 10000.0
H_A = 4
H_B = 4
H_C = 8
N_DIR = 2
N_EXPERTS = 16
N_GROUPS = 4
EXPERTS_PER_GROUP = N_EXPERTS // N_GROUPS
HEAD_W = 128
DH_C = 64
N_COND_PAD = 16

ROW_TILE = 256
SCAN_CHUNK = 256
ATTN_TQ = 1024
ATTN_TK = 512
MOE_BLOCK = 512
MOE_CHUNK = 16
MOE_BLOCK_CHUNKS = MOE_BLOCK // MOE_CHUNK
MOE_TILE_CHUNKS = (2 * ROW_TILE + N_EXPERTS * (MOE_CHUNK - 1)) // MOE_CHUNK + 1
MOE_TILE_ROWS = MOE_TILE_CHUNKS * MOE_CHUNK
MOE_SCRATCH_TILES = -(-N_EXPERTS * MOE_BLOCK_CHUNKS // MOE_TILE_CHUNKS)
VMEM_LIMIT = 56 * 1024 * 1024

_HI = lax.Precision.HIGHEST


def _dot(a, b, precision=None):
    return jnp.dot(a, b, preferred_element_type=F32, precision=precision)


def _dot_nt(a, b):
    return lax.dot_general(a, b, (((1,), (1,)), ((), ())), preferred_element_type=F32)


def _dot_tn(a, b):
    return lax.dot_general(a, b, (((0,), (0,)), ((), ())), preferred_element_type=F32)


def _rms(x):
    return x * lax.rsqrt(jnp.mean(x * x, axis=-1, keepdims=True) + EPS)


def _sigmoid(x):
    return 1.0 / (1.0 + jnp.exp(-x))


def _silu(x):
    return x * _sigmoid(x)


def _log_sigmoid(x):
    return jnp.minimum(x, 0.0) - jnp.log1p(jnp.exp(-jnp.abs(x)))


def _params(n_axes):
    return pltpu.CompilerParams(dimension_semantics=("arbitrary",) * n_axes,
                                vmem_limit_bytes=VMEM_LIMIT)


def _mod_kernel(cond_ref, w_ref, b_ref, o_ref):
    s = _silu(cond_ref[...]).astype(BF16)
    o_ref[0] = _dot(s, w_ref[0].astype(BF16)) + b_ref[0]


def _modulation_all(cond, w_ada, b_ada):
    depth, d, n = w_ada.shape
    tn = n // 4
    return pl.pallas_call(
        _mod_kernel,
        grid=(depth, n // tn),
        in_specs=[pl.BlockSpec((N_COND_PAD, d), lambda l, j: (0, 0)),
                  pl.BlockSpec((1, d, tn), lambda l, j: (l, 0, j)),
                  pl.BlockSpec((1, 1, tn), lambda l, j: (l, 0, j))],
        out_specs=pl.BlockSpec((1, N_COND_PAD, tn), lambda l, j: (l, 0, j)),
        out_shape=jax.ShapeDtypeStruct((depth, N_COND_PAD, n), F32),
        compiler_params=_params(2),
        name="adaln_modulation",
    )(cond, w_ada, b_ada.reshape(depth, 1, n))


class _Rows:
    def __init__(self, n_prompt_seq, prompt_len, n_sample_seq, sample_len):
        self.prompt_len = prompt_len
        self.sample_len = sample_len
        self.n_prompt_seq = n_prompt_seq
        self.n_sample_seq = n_sample_seq
        self.n_prompt = n_prompt_seq * prompt_len
        self.n_sample = n_sample_seq * sample_len
        self.total = self.n_prompt + self.n_sample
        assert prompt_len % ROW_TILE == 0 and sample_len % ROW_TILE == 0
        assert self.n_prompt % sample_len == 0 or self.n_sample == 0

    def cond_of_tile(self, i, tile):
        n_p = self.n_prompt // tile
        per_seq = self.sample_len // tile
        return jnp.where(i < n_p, 0, 1 + (i - n_p) // per_seq)


def _proj_even_kernel(x_ref, xp_ref, xn_ref, mod_ref, gain_ref, w_ref, wg_ref, cw_ref, gb_ref,
                      qkva_ref, qkvb_ref, oz_ref, g_ref, *, tm, n_prompt, prompt_len, sample_len):
    i = pl.program_id(0)
    shift = mod_ref[0, 0:1, :]
    scale = mod_ref[0, 1:2, :]
    gain = gain_ref[...]

    def modulated(x):
        return _rms(x) * gain * (1.0 + scale) + shift

    h = modulated(x_ref[...])
    hb = h.astype(BF16)
    halo = jnp.concatenate([xp_ref[0], xn_ref[0]], axis=0)
    hh = modulated(halo).astype(BF16)

    w_qk = w_ref[:, 0:1024]
    qk = _dot(hb, w_qk)
    qk_halo = _dot(hh, w_qk)
    prev_row = qk_halo[7:8, :]
    next_row = qk_halo[8:9, :]
    local = lax.broadcasted_iota(jnp.int32, (tm, 1), 0)
    seq_len = jnp.where(i * tm < n_prompt, prompt_len, sample_len)
    pos = (i * tm + local) & (seq_len - 1)
    prev = pltpu.roll(qk, 1, axis=0)
    prev = jnp.where(local == 0, prev_row, prev)
    prev = jnp.where(pos == 0, 0.0, prev)
    nxt = pltpu.roll(qk, tm - 1, axis=0)
    nxt = jnp.where(local == tm - 1, next_row, nxt)
    nxt = jnp.where(pos == seq_len - 1, 0.0, nxt)
    cw = cw_ref[...]
    act = _silu(cw[0:1, :] * prev + cw[1:2, :] * qk + cw[2:3, :] * nxt)
    k_scale = HEAD_W ** -0.5
    qkva_ref[:, 0:512] = act[:, 0:512].astype(BF16)
    qkva_ref[:, 512:1024] = (act[:, 512:1024] * k_scale).astype(BF16)
    qkva_ref[:, 1024:1536] = _dot(hb, w_ref[:, 1024:1536]).astype(BF16)

    qkvb_ref[:, 0:512] = _dot(hb, w_ref[:, 1536:2048]).astype(BF16)
    qkvb_ref[:, 512:1024] = (_dot(hb, w_ref[:, 2048:2560]) * k_scale).astype(BF16)
    qkvb_ref[:, 1024:1536] = _dot(hb, w_ref[:, 2560:3072]).astype(BF16)
    oz_ref[:, 0:512] = _dot(hb, w_ref[:, 3072:3584]).astype(BF16)
    oz_ref[:, 512:1024] = _dot(hb, w_ref[:, 3584:4096]).astype(BF16)

    h_lo = (h - hb.astype(F32)).astype(BF16)
    g_hl = _dot(hb, wg_ref[...])
    gates = (g_hl[:, 0:256] + g_hl[:, 256:512] + _dot(h_lo, wg_ref[:, 0:256])
             + gb_ref[...])
    g_ref[:, 0:128] = gates[:, 0:128]
    g_ref[:, 128:256] = _log_sigmoid(gates[:, 128:256])


def _proj_even(x, mods, gain, w_main, w_gates, conv_w, gate_bias, rows):
    n, d = x.shape
    tm = ROW_TILE
    x8 = x.reshape(n // 8, 8, d)
    nb8 = n // 8
    kern = functools.partial(_proj_even_kernel, tm=tm, n_prompt=rows.n_prompt,
                             prompt_len=rows.prompt_len, sample_len=rows.sample_len)
    return pl.pallas_call(
        kern,
        grid=(n // tm,),
        in_specs=[pl.BlockSpec((tm, d), lambda i: (i, 0)),
                  pl.BlockSpec((1, 8, d), lambda i: (jnp.maximum(i * (tm // 8) - 1, 0), 0, 0)),
                  pl.BlockSpec((1, 8, d), lambda i: (jnp.minimum((i + 1) * (tm // 8), nb8 - 1), 0, 0)),
                  pl.BlockSpec((1, 6, d), lambda i: (rows.cond_of_tile(i, tm), 0, 0)),
                  pl.BlockSpec((1, d), lambda i: (0, 0)),
                  pl.BlockSpec(w_main.shape, lambda i: (0, 0)),
                  pl.BlockSpec(w_gates.shape, lambda i: (0, 0)),
                  pl.BlockSpec(conv_w.shape, lambda i: (0, 0)),
                  pl.BlockSpec(gate_bias.shape, lambda i: (0, 0))],
        out_specs=[pl.BlockSpec((tm, 1536), lambda i: (i, 0)),
                   pl.BlockSpec((tm, 1536), lambda i: (i, 0)),
                   pl.BlockSpec((tm, 1024), lambda i: (i, 0)),
                   pl.BlockSpec((tm, 256), lambda i: (i, 0))],
        out_shape=[jax.ShapeDtypeStruct((n, 1536), BF16),
                   jax.ShapeDtypeStruct((n, 1536), BF16),
                   jax.ShapeDtypeStruct((n, 1024), BF16),
                   jax.ShapeDtypeStruct((n, 256), F32)],
        compiler_params=_params(1),
        name="proj_even",
    )(x, x8, x8, mods, gain, w_main, w_gates, conv_w, gate_bias)


def _scan_kernel(fb_ref, bb_ref, first_ref, last_ref, sin_ref, sout_ref, isp_ref,
                 qaf_ref, qab_ref, qbf_ref, qbb_ref, gf_ref, gb_ref,
                 c0_ref, n0_ref, m0_ref, s0_ref, rl_ref,
                 hf_ref, hb_ref, cn_ref, nn_ref, mn_ref, sn_ref,
                 cext_sc, s_sc, m_sc, intra_sc, inter_sc, toend_sc, cdec_sc, *, L):
    del fb_ref, bb_ref, sin_ref, sout_ref
    step = pl.program_id(0)
    n_units = N_DIR * H_A
    lane = lax.broadcasted_iota(jnp.int32, (HEAD_W, HEAD_W), 1)
    t_idx = lax.broadcasted_iota(jnp.int32, (L, L), 0)
    s_idx = lax.broadcasted_iota(jnp.int32, (L, L), 1)
    masks = (s_idx <= t_idx, s_idx >= t_idx)

    @pl.when(step == 0)
    def _():
        log_gamma = _log_sigmoid(rl_ref[...])
        rel = jnp.abs(t_idx - s_idx).astype(F32)
        pos_col = lax.broadcasted_iota(jnp.int32, (L, HEAD_W), 0).astype(F32)
        for u in range(n_units):
            lg = log_gamma[u:u + 1, 0:1]
            intra_sc[u] = jnp.where(masks[u // H_B], jnp.exp(lg * rel), 0.0)
            pos = pos_col if u < H_B else (L - 1.0) - pos_col
            inter_sc[u] = jnp.exp(lg * (pos + 1.0))
        unit = lax.broadcasted_iota(jnp.int32, (n_units, L), 0)
        pos_row = lax.broadcasted_iota(jnp.int32, (n_units, L), 1).astype(F32)
        pos_row = jnp.where(unit < H_B, pos_row, (L - 1.0) - pos_row)
        toend_sc[...] = jnp.exp(log_gamma[:, 0:1] * ((L - 1.0) - pos_row))
        cdec_sc[...] = jnp.exp(log_gamma * float(L))

    @pl.when(jnp.logical_and(first_ref[step] == 1, isp_ref[step] == 1))
    def _():
        cext_sc[...] = jnp.zeros_like(cext_sc)
        s_sc[...] = jnp.zeros_like(s_sc)
        m_sc[...] = jnp.zeros_like(m_sc)

    @pl.when(jnp.logical_and(first_ref[step] == 1, isp_ref[step] == 0))
    def _():
        n0 = n0_ref[0]
        n0_t = jnp.concatenate([n0, jnp.zeros((HEAD_W - n_units, HEAD_W), F32)], axis=0).T
        for u in range(n_units):
            cext_sc[u, :, 0:HEAD_W] = c0_ref[0, u]
            cext_sc[u, :, HEAD_W:2 * HEAD_W] = jnp.broadcast_to(n0_t[:, u:u + 1], (HEAD_W, HEAD_W))
            s_sc[u] = s0_ref[0, u]
        m_sc[...] = m0_ref[0]

    ones_ext = jnp.ones((L, HEAD_W), BF16)
    row_l = lax.broadcasted_iota(jnp.int32, (L, HEAD_W), 0)
    tri = tuple(jnp.where(m, 1.0, 0.0) for m in masks)
    qa = (qaf_ref, qab_ref)
    qb = (qbf_ref, qbb_ref)
    g = (gf_ref, gb_ref)
    out = (hf_ref, hb_ref)
    m_all = m_sc[...]
    toend = toend_sc[...]
    cdec = cdec_sc[...]
    m_rows = []

    def wide(x):
        return x if L == HEAD_W else jnp.concatenate([x] * (L // HEAD_W), axis=1)

    for d in range(N_DIR):
        ig_all = g[d][:, 0:HEAD_W]
        lf_all = g[d][:, HEAD_W:2 * HEAD_W]
        bt_all = lf_all
        shift = 1
        while shift < L:
            if d == 0:
                bt_all = bt_all + jnp.where(row_l >= shift, pltpu.roll(bt_all, shift, axis=0), 0.0)
            else:
                bt_all = bt_all + jnp.where(row_l < L - shift, pltpu.roll(bt_all, L - shift, axis=0), 0.0)
            shift *= 2
        a_all = ig_all - bt_all
        a_rows = a_all.T[0:n_units, :]
        cm_all = a_all
        shift = 1
        while shift < L:
            if d == 0:
                moved = jnp.where(row_l >= shift, pltpu.roll(cm_all, shift, axis=0), -jnp.inf)
            else:
                moved = jnp.where(row_l < L - shift, pltpu.roll(cm_all, L - shift, axis=0), -jnp.inf)
            cm_all = jnp.maximum(cm_all, moved)
            shift *= 2
        end = L - 1 if d == 0 else 0
        for h in range(H_A):
            u = d * H_A + h
            cs = slice(h * HEAD_W, (h + 1) * HEAD_W)
            m_prev = m_all[u:u + 1, :]
            g_rep = jnp.maximum(m_prev, jnp.broadcast_to(cm_all[:, u:u + 1], (L, HEAD_W)))
            bt_rep = jnp.broadcast_to(bt_all[:, u:u + 1], (L, HEAD_W))
            a_row = a_rows[u:u + 1, :]
            dmat = jnp.exp(jnp.where(masks[d], a_row - wide(g_rep), -jnp.inf))
            inter_w = jnp.exp(m_prev - g_rep)
            q = qa[d][:, cs]
            k = qa[d][:, 512 + h * HEAD_W:512 + (h + 1) * HEAD_W]
            v = qa[d][:, 1024 + h * HEAD_W:1024 + (h + 1) * HEAD_W]
            v_ext = jnp.concatenate([v, ones_ext], axis=1)
            s = (_dot_nt(q, k) * dmat).astype(BF16)
            lhs = jnp.concatenate([s, (q.astype(F32) * inter_w).astype(BF16)], axis=1)
            rhs = jnp.concatenate([v_ext, cext_sc[u].astype(BF16)], axis=0)
            num = _dot(lhs, rhs)
            inv = 1.0 / jnp.maximum(jnp.abs(num[:, HEAD_W:2 * HEAD_W]), jnp.exp(-(bt_rep + g_rep)))
            out[d][:, cs] = (num[:, 0:HEAD_W] * inv).astype(out[d].dtype)
            m_prev1 = m_prev[:, 0:1]
            g_end = jnp.maximum(m_prev1, cm_all[end:end + 1, u:u + 1])
            wk_row = jnp.exp(a_row - g_end)
            kw = (k.astype(F32).T * wk_row).astype(BF16)
            cext_sc[u] = jnp.exp(m_prev1 - g_end) * cext_sc[u] + _dot(kw, v_ext)
            m_rows.append(jnp.broadcast_to(bt_all[end:end + 1, u:u + 1] + g_end, (1, HEAD_W)))
            qr = qb[d][:, cs]
            kr = qb[d][:, 512 + h * HEAD_W:512 + (h + 1) * HEAD_W]
            vr = qb[d][:, 1024 + h * HEAD_W:1024 + (h + 1) * HEAD_W]
            sr = (_dot_nt(qr, kr) * intra_sc[u]).astype(BF16)
            lhs = jnp.concatenate([sr, (qr.astype(F32) * inter_sc[u]).astype(BF16)], axis=1)
            rhs = jnp.concatenate([vr, s_sc[u].astype(BF16)], axis=0)
            out[d][:, 512 + h * HEAD_W:512 + (h + 1) * HEAD_W] = _dot(lhs, rhs).astype(out[d].dtype)
            krw = (kr.astype(F32).T * toend[u:u + 1, :]).astype(BF16)
            s_sc[u] = cdec[u:u + 1, :] * s_sc[u] + _dot(krw, vr)
    m_sc[...] = jnp.concatenate(m_rows, axis=0)

    @pl.when(last_ref[step] == 1)
    def _():
        n_cols = jnp.zeros((HEAD_W, HEAD_W), F32)
        for u in range(n_units):
            cn_ref[0, u] = cext_sc[u, :, 0:HEAD_W]
            sn_ref[0, u] = s_sc[u]
            n_cols = jnp.where(lane == u, cext_sc[u, :, HEAD_W:HEAD_W + 1], n_cols)
        nn_ref[0] = n_cols.T[0:n_units, :]
        mn_ref[0] = m_sc[...]


def _scan_tables(rows, L):
    fb, bb, first, last, sin, sout, isp = [], [], [], [], [], [], []
    base = 0
    for kind, n_seq, seq_len in (("p", rows.n_prompt_seq, rows.prompt_len),
                                 ("s", rows.n_sample_seq, rows.sample_len)):
        nc = seq_len // L
        for b in range(n_seq):
            for c in range(nc):
                fb.append(base + b * nc + c)
                bb.append(base + b * nc + nc - 1 - c)
                first.append(int(c == 0))
                last.append(int(c == nc - 1 and kind == "p"))
                sin.append(b if kind == "s" else 0)
                sout.append(b if kind == "p" else rows.n_prompt_seq - 1)
                isp.append(int(kind == "p"))
        base += n_seq * nc
    return [jnp.asarray(np.asarray(t, np.int32)) for t in (fb, bb, first, last, sin, sout, isp)]


def _scan_even(qkva, qkvb, gates, c0, n0, m0, s0, ret_logit, rows):
    L = SCAN_CHUNK
    n = qkva.shape[0]
    tables = _scan_tables(rows, L)
    n_steps = int(tables[0].shape[0])
    nu = N_DIR * H_A
    fwd = lambda w: pl.BlockSpec((L, w), lambda s, fb, bb, fi, la, si, so, ip: (fb[s], 0))
    bwd = lambda w: pl.BlockSpec((L, w), lambda s, fb, bb, fi, la, si, so, ip: (bb[s], 0))
    st_in4 = pl.BlockSpec((1, nu, HEAD_W, HEAD_W), lambda s, fb, bb, fi, la, si, so, ip: (si[s], 0, 0, 0))
    st_in3 = pl.BlockSpec((1, nu, HEAD_W), lambda s, fb, bb, fi, la, si, so, ip: (si[s], 0, 0))
    st_out4 = pl.BlockSpec((1, nu, HEAD_W, HEAD_W), lambda s, fb, bb, fi, la, si, so, ip: (so[s], 0, 0, 0))
    st_out3 = pl.BlockSpec((1, nu, HEAD_W), lambda s, fb, bb, fi, la, si, so, ip: (so[s], 0, 0))
    nps = rows.n_prompt_seq
    return pl.pallas_call(
        functools.partial(_scan_kernel, L=L),
        grid_spec=pltpu.PrefetchScalarGridSpec(
            num_scalar_prefetch=7,
            grid=(n_steps,),
            in_specs=[fwd(1536), bwd(1536), fwd(1536), bwd(1536), fwd(256), bwd(256),
                      st_in4, st_in3, st_in3, st_in4,
                      pl.BlockSpec((nu, HEAD_W), lambda s, *_: (0, 0))],
            out_specs=[fwd(1024), bwd(1024), st_out4, st_out3, st_out3, st_out4],
            scratch_shapes=[pltpu.VMEM((nu, HEAD_W, 2 * HEAD_W), F32),
                            pltpu.VMEM((nu, HEAD_W, HEAD_W), F32),
                            pltpu.VMEM((nu, HEAD_W), F32),
                            pltpu.VMEM((nu, L, L), F32),
                            pltpu.VMEM((nu, L, HEAD_W), F32),
                            pltpu.VMEM((nu, L), F32),
                            pltpu.VMEM((nu, HEAD_W), F32)]),
        out_shape=[jax.ShapeDtypeStruct((n, 1024), BF16),
                   jax.ShapeDtypeStruct((n, 1024), BF16),
                   jax.ShapeDtypeStruct((nps, nu, HEAD_W, HEAD_W), F32),
                   jax.ShapeDtypeStruct((nps, nu, HEAD_W), F32),
                   jax.ShapeDtypeStruct((nps, nu, HEAD_W), F32),
                   jax.ShapeDtypeStruct((nps, nu, HEAD_W, HEAD_W), F32)],
        compiler_params=_params(1),
        name="scan_even",
    )(*tables, qkva, qkva, qkvb, qkvb, gates, gates, c0, n0, m0, s0, ret_logit)


def _proj_odd_kernel(x_ref, mod_ref, gain_ref, w_ref, e_ref, qg_ref, kg_ref, cos_ref, sin_ref,
                     q_ref, k_ref, v_ref, *, rope):
    shift = mod_ref[0, 0:1, :]
    scale = mod_ref[0, 1:2, :]
    hb = (_rms(x_ref[...]) * gain_ref[...] * (1.0 + scale) + shift).astype(BF16)

    def qk_norm(raw, g):
        ss = _dot((raw * raw).astype(BF16), e_ref[...])
        return raw * lax.rsqrt(ss * (1.0 / DH_C) + EPS) * g

    def rotate(y):
        if not rope:
            return y
        lane = lax.broadcasted_iota(jnp.int32, y.shape, 1)
        first_half = (lane & 31) < 16
        partner = jnp.where(first_half, pltpu.roll(y, y.shape[1] - 16, axis=1), pltpu.roll(y, 16, axis=1))
        cos = jnp.concatenate([cos_ref[...]] * (y.shape[1] // HEAD_W), axis=1)
        sin = jnp.concatenate([sin_ref[...]] * (y.shape[1] // HEAD_W), axis=1)
        return y * cos + partner * sin

    q = rotate(qk_norm(_dot(hb, w_ref[:, 0:1024]), qg_ref[...]))
    q_ref[...] = (q * (DH_C ** -0.5 * math.log2(math.e))).astype(q_ref.dtype)
    k = rotate(qk_norm(_dot(hb, w_ref[:, 1024:2048]), kg_ref[...]))
    k_ref[...] = k.astype(k_ref.dtype)
    v_ref[...] = _dot(hb, w_ref[:, 2048:3072]).astype(v_ref.dtype)


def _proj_odd(x, mods, gain, w_qkv, e64, q_gain, k_gain, cos, sin, rows, *, sample):
    d = x.shape[1]
    tm = ROW_TILE
    if sample:
        n, base, kv_dtype = rows.n_sample, rows.n_prompt // tm, BF16
        per_seq = rows.sample_len // tm
        table = lambda i: (i % per_seq, 0)
    else:
        n, base, kv_dtype = rows.n_prompt, 0, F32
        table = lambda i: (0, 0)
    return pl.pallas_call(
        functools.partial(_proj_odd_kernel, rope=sample),
        grid=(n // tm,),
        in_specs=[pl.BlockSpec((tm, d), lambda i: (base + i, 0)),
                  pl.BlockSpec((1, 6, d), lambda i: (rows.cond_of_tile(base + i, tm), 0, 0)),
                  pl.BlockSpec((1, d), lambda i: (0, 0)),
                  pl.BlockSpec(w_qkv.shape, lambda i: (0, 0)),
                  pl.BlockSpec(e64.shape, lambda i: (0, 0)),
                  pl.BlockSpec((1, d), lambda i: (0, 0)),
                  pl.BlockSpec((1, d), lambda i: (0, 0)),
                  pl.BlockSpec((tm, HEAD_W), table),
                  pl.BlockSpec((tm, HEAD_W), table)],
        out_specs=[pl.BlockSpec((tm, d), lambda i: (i, 0))] * 3,
        out_shape=[jax.ShapeDtypeStruct((n, d), BF16),
                   jax.ShapeDtypeStruct((n, d), kv_dtype),
                   jax.ShapeDtypeStruct((n, d), kv_dtype)],
        compiler_params=_params(1),
        name="proj_odd_sample" if sample else "proj_odd_prompt",
    )(x, mods, gain, w_qkv, e64, q_gain, k_gain, cos, sin)


def _rope_tables(sample_len):
    t = np.arange(sample_len)
    nf = DH_C // 4
    inv = ROPE_BASE ** (-np.arange(nf, dtype=np.float32) / nf)
    row = (t // GRID_W).astype(np.float32)[:, None] * inv[None, :]
    col = (t % GRID_W).astype(np.float32)[:, None] * inv[None, :]
    cos64 = np.concatenate([np.cos(row), np.cos(row), np.cos(col), np.cos(col)], axis=1)
    sin64 = np.concatenate([-np.sin(row), np.sin(row), -np.sin(col), np.sin(col)], axis=1)
    cos = np.concatenate([cos64, cos64], axis=1).astype(np.float32)
    sin = np.concatenate([sin64, sin64], axis=1).astype(np.float32)
    return jnp.asarray(cos), jnp.asarray(sin)


def _attn_kernel(*refs, tq, tk, n_main, n_cache, lam_init, hps):
    if n_cache:
        q_ref, k_ref, v_ref, ck_ref, cv_ref, lq_ref, og_ref, o_ref, vt_sc = refs
    else:
        q_ref, k_ref, v_ref, lq_ref, og_ref, o_ref, vt_sc = refs
    fill_w = 256

    @pl.when(pl.program_id(2) == 0)
    def _():
        ones_rows = jnp.where(lax.broadcasted_iota(jnp.int32, (8, fill_w), 0) == 0, 1.0, 0.0).astype(BF16)
        for hh in range(hps):
            hs = slice(hh * HEAD_W, (hh + 1) * HEAD_W)
            for j in range(n_main // fill_w):
                vt = v_ref[0, j * fill_w:(j + 1) * fill_w, hs].astype(F32).T.astype(BF16)
                vt_sc[hh, :, j * fill_w:(j + 1) * fill_w] = jnp.concatenate([vt, ones_rows], axis=0)
            if n_cache:
                vt = cv_ref[0, :, hs].astype(F32).T.astype(BF16)
                vt_sc[hh, :, n_main:n_main + n_cache] = jnp.concatenate([vt, ones_rows], axis=0)

    lq = lq_ref[...]
    lam = (jnp.exp(jnp.sum(lq[0:1, :] * lq[1:2, :], axis=1, keepdims=True))
           - jnp.exp(jnp.sum(lq[2:3, :] * lq[3:4, :], axis=1, keepdims=True)) + lam_init)
    for hh in range(hps):
        _attn_head(hh, q_ref, k_ref, ck_ref if n_cache else None, vt_sc, og_ref, o_ref, lam,
                   tq=tq, tk=tk, n_main=n_main, n_cache=n_cache, lam_init=lam_init)


def _attn_head(hh, q_ref, k_ref, ck_ref, vt_sc, og_ref, o_ref, lam, *, tq, tk, n_main, n_cache, lam_init):
    hs = slice(hh * HEAD_W, (hh + 1) * HEAD_W)
    qt = q_ref[0, :, hs].astype(F32).T
    row = lax.broadcasted_iota(jnp.int32, qt.shape, 0)
    qst = jnp.concatenate([jnp.where(row < DH_C, qt, 0.0), jnp.where(row >= DH_C, qt, 0.0)],
                          axis=1).astype(BF16)

    def scores(kt):
        return _dot(kt.astype(BF16), qst)

    def accumulate(carry, s, vt):
        m, acc = carry
        m_new = jnp.maximum(m, jnp.max(s, axis=0, keepdims=True))
        p = jnp.exp2(s - m_new).astype(BF16)
        acc = jnp.exp2(m - m_new) * acc + _dot(vt, p)
        return m_new, acc

    tiles = [(k_ref, j * tk, tk, j * tk) for j in range(n_main // tk)]
    if n_cache:
        tiles = [(ck_ref, 0, n_cache, n_main)] + tiles
    carry = (jnp.full((1, 2 * tq), -jnp.inf, F32), jnp.zeros((HEAD_W + 8, 2 * tq), F32))
    ref0, r0, n0, _ = tiles[0]
    s_next = scores(ref0[0, r0:r0 + n0, hs])
    for j, (_, _, n_keys, c0) in enumerate(tiles):
        s_cur = s_next
        if j + 1 < len(tiles):
            ref1, r1, n1, _ = tiles[j + 1]
            s_next = scores(ref1[0, r1:r1 + n1, hs])
        carry = accumulate(carry, s_cur, vt_sc[hh, :, c0:c0 + n_keys])
    _, acc = carry
    o_t = acc[0:HEAD_W, :] / acc[HEAD_W:HEAD_W + 1, :]
    o = (o_t[:, 0:tq] - lam * o_t[:, tq:2 * tq]).T
    o_ref[0, :, hs] = (_rms(o) * og_ref[...] * (1.0 - lam_init)).astype(o_ref.dtype)


def _attention(q, k, v, cache_k, cache_v, lam_qk, out_gain, lam_init):
    b, t, d = q.shape
    tq = min(ATTN_TQ, t)
    tk = min(ATTN_TK, t)
    n_cache = 0 if cache_k is None else cache_k.shape[1]
    assert t % tk == 0 and t % tq == 0 and t % 256 == 0 and n_cache in (0, 256)
    hps = H_C if t + n_cache <= 512 else 1
    hw = hps * HEAD_W
    seq = lambda n: pl.BlockSpec((1, n, hw), lambda bi, h, qi: (bi, 0, h))
    in_specs = [pl.BlockSpec((1, tq, hw), lambda bi, h, qi: (bi, qi, h)), seq(t), seq(t)]
    args = [q, k, v]
    if n_cache:
        in_specs += [seq(n_cache), seq(n_cache)]
        args += [cache_k, cache_v]
    in_specs += [pl.BlockSpec(lam_qk.shape, lambda bi, h, qi: (0, 0)),
                 pl.BlockSpec(out_gain.shape, lambda bi, h, qi: (0, 0))]
    args += [lam_qk, out_gain]
    return pl.pallas_call(
        functools.partial(_attn_kernel, tq=tq, tk=tk, n_main=t, n_cache=n_cache, lam_init=lam_init, hps=hps),
        grid=(b, H_C // hps, t // tq),
        in_specs=in_specs,
        out_specs=pl.BlockSpec((1, tq, hw), lambda bi, h, qi: (bi, qi, h)),
        out_shape=jax.ShapeDtypeStruct((b, t, d), BF16),
        scratch_shapes=[pltpu.VMEM((hps, HEAD_W + 8, t + n_cache), BF16)],
        compiler_params=_params(3),
        name="diff_attention",
    )(*args)


def _out_kernel(*refs, n_tiles, **static):
    ys0_ref = refs[-4]
    ys0_ref[...] = jnp.zeros_like(ys0_ref)

    @pl.when(pl.program_id(0) < n_tiles)
    def _():
        _out_tile(*refs, **static)


def _out_tile(*refs, tm, even, n_prompt_tiles):
    if even:
        (hf_ref, hb_ref, oz_ref, ga_ref, gb_ref, x_ref, mod_ref, w_ref, fg_ref, wr_ref, rb_ref,
         x1_ref, xs_ref, ys0_ref, sl_ref, gw_ref, cnt_ref) = refs
    else:
        (op_ref, os_ref, x_ref, mod_ref, w_ref, fg_ref, wr_ref, rb_ref,
         x1_ref, xs_ref, ys0_ref, sl_ref, gw_ref, cnt_ref) = refs
    del ys0_ref
    i = pl.program_id(0)

    if even:
        hs = hf_ref[...].astype(F32) + hb_ref[...].astype(F32)
        oz = oz_ref[...].astype(F32)
        parts = []
        for h in range(H_A + H_B):
            cs = slice(h * HEAD_W, (h + 1) * HEAD_W)
            gain = ga_ref[:, cs] if h < H_A else gb_ref[:, (h - H_A) * HEAD_W:(h - H_A + 1) * HEAD_W]
            act = _sigmoid(oz[:, cs]) if h < H_A else _silu(oz[:, cs])
            parts.append((_rms(hs[:, cs]) * gain * act).astype(BF16))
        y_in = jnp.concatenate(parts, axis=1)
    else:
        y_in = jnp.where(i < n_prompt_tiles, op_ref[...], os_ref[...])

    x1 = x_ref[...] + mod_ref[0, 2:3, :] * _dot(y_in, w_ref[...])
    x1_ref[...] = x1
    h2 = _rms(x1) * fg_ref[...] * (1.0 + mod_ref[0, 4:5, :]) + mod_ref[0, 3:4, :]
    h2b = h2.astype(BF16)

    score = _sigmoid(_dot(h2b, wr_ref[...]))
    st = score.T[0:N_EXPERTS, :]
    sel = st + rb_ref[:, 0:1]
    srow = [st[e:e + 1, :] for e in range(N_EXPERTS)]
    brow = [sel[e:e + 1, :] for e in range(N_EXPERTS)]
    epg = EXPERTS_PER_GROUP
    gscore = []
    for g in range(N_GROUPS):
        a = brow[g * epg:(g + 1) * epg]
        best = None
        for p in range(epg):
            for r in range(p + 1, epg):
                pair = a[p] + a[r]
                best = pair if best is None else jnp.maximum(best, pair)
        gscore.append(best)
    gbest, gidx = gscore[0], jnp.zeros((1, tm), jnp.int32)
    for g in range(1, N_GROUPS):
        better = gscore[g] > gbest
        gbest = jnp.where(better, gscore[g], gbest)
        gidx = jnp.where(better, g, gidx)
    vals, sig = [], []
    for p in range(epg):
        vp, sp = brow[p], srow[p]
        for g in range(1, N_GROUPS):
            vp = jnp.where(gidx == g, brow[g * epg + p], vp)
            sp = jnp.where(gidx == g, srow[g * epg + p], sp)
        vals.append(vp)
        sig.append(sp)
    v1, i1, w1 = vals[0], jnp.zeros((1, tm), jnp.int32), sig[0]
    for p in range(1, epg):
        better = vals[p] > v1
        v1 = jnp.where(better, vals[p], v1)
        i1 = jnp.where(better, p, i1)
        w1 = jnp.where(better, sig[p], w1)
    v2 = jnp.full((1, tm), -jnp.inf, F32)
    i2 = jnp.zeros((1, tm), jnp.int32)
    w2 = jnp.zeros((1, tm), F32)
    for p in range(epg):
        better = jnp.logical_and(i1 != p, vals[p] > v2)
        v2 = jnp.where(better, vals[p], v2)
        i2 = jnp.where(better, p, i2)
        w2 = jnp.where(better, sig[p], w2)
    e1 = gidx * epg + i1
    e2 = gidx * epg + i2
    wsum = w1 + w2

    erow = lax.broadcasted_iota(jnp.int32, (N_EXPERTS, tm), 0)
    earlier = jnp.where(lax.broadcasted_iota(jnp.int32, (tm, tm), 0)
                        < lax.broadcasted_iota(jnp.int32, (tm, tm), 1), 1.0, 0.0).astype(BF16)
    oh1 = jnp.where(erow == e1, 1.0, 0.0)
    oh2 = jnp.where(erow == e2, 1.0, 0.0)
    before1 = _dot(oh1.astype(BF16), earlier)
    before2 = _dot(oh2.astype(BF16), earlier)
    cnt1 = jnp.sum(oh1, axis=1, keepdims=True)
    cnt = cnt1 + jnp.sum(oh2, axis=1, keepdims=True)
    padded = jnp.floor((cnt + (MOE_CHUNK - 1.0)) * (1.0 / MOE_CHUNK)) * MOE_CHUNK
    below = jnp.where(lax.broadcasted_iota(jnp.int32, (N_EXPERTS, N_EXPERTS), 1)
                      < lax.broadcasted_iota(jnp.int32, (N_EXPERTS, N_EXPERTS), 0), 1.0, 0.0)
    start = _dot(below, jnp.broadcast_to(padded, (N_EXPERTS, HEAD_W)), precision=_HI)[:, 0:1]
    slot1 = jnp.sum(oh1 * (start + before1), axis=0, keepdims=True).astype(jnp.int32)
    slot2 = jnp.sum(oh2 * (start + cnt1 + before2), axis=0, keepdims=True).astype(jnp.int32)
    local = lax.broadcasted_iota(jnp.int32, (MOE_TILE_ROWS, tm), 0)
    place = jnp.where(jnp.logical_or(local == slot1, local == slot2), 1.0, 0.0).astype(BF16)
    xs_ref[...] = _dot(place, h2b).astype(BF16)
    zi = jnp.zeros((6, tm), jnp.int32)
    sl_ref[...] = jnp.concatenate([slot1, slot2, zi], axis=0)
    gw_ref[...] = jnp.concatenate([w1 / wsum, w2 / wsum, jnp.zeros((6, tm), F32)], axis=0)
    cnt_ref[...] = jnp.broadcast_to(cnt, (N_EXPERTS, HEAD_W))


def _out_and_route(mix_in, x, mods, w_out, ffn_gain, w_router, router_bias, rows, *, even):
    n, d = x.shape
    tm = ROW_TILE
    n_tiles = n // tm
    n_pt = rows.n_prompt // tm
    tile = lambda i: jnp.minimum(i, n_tiles - 1)
    row = lambda w: pl.BlockSpec((tm, w), lambda i: (tile(i), 0))
    full = lambda a: pl.BlockSpec(a.shape, lambda i: (0,) * a.ndim)
    if even:
        hf, hb, oz, gain_a, gain_b = mix_in
        in_specs = [row(1024), row(1024), row(1024), full(gain_a), full(gain_b)]
        args = [hf, hb, oz, gain_a, gain_b]
    else:
        o_p, o_s = mix_in
        in_specs = [pl.BlockSpec((tm, d), lambda i: (jnp.minimum(i, n_pt - 1), 0)),
                    pl.BlockSpec((tm, d), lambda i: (jnp.maximum(tile(i) - n_pt, 0), 0))]
        args = [o_p, o_s]
    in_specs += [row(d), pl.BlockSpec((1, 6, d), lambda i: (rows.cond_of_tile(tile(i), tm), 0, 0)),
                 full(w_out), full(ffn_gain), full(w_router), full(router_bias)]
    args += [x, mods, w_out, ffn_gain, w_router, router_bias]
    col = lambda: pl.BlockSpec((8, tm), lambda i: (0, tile(i)))
    return pl.pallas_call(
        functools.partial(_out_kernel, n_tiles=n_tiles, tm=tm, even=even, n_prompt_tiles=n_pt),
        grid=(n_tiles + MOE_SCRATCH_TILES,),
        in_specs=in_specs,
        out_specs=[row(d), pl.BlockSpec((MOE_TILE_ROWS, d), lambda i: (tile(i), 0)),
                   pl.BlockSpec((MOE_TILE_ROWS, d), lambda i: (i, 0)), col(), col(),
                   pl.BlockSpec((N_EXPERTS, HEAD_W), lambda i: (tile(i), 0))],
        out_shape=[jax.ShapeDtypeStruct((n, d), F32),
                   jax.ShapeDtypeStruct((n_tiles * MOE_TILE_ROWS, d), BF16),
                   jax.ShapeDtypeStruct(((n_tiles + MOE_SCRATCH_TILES) * MOE_TILE_ROWS, d), BF16),
                   jax.ShapeDtypeStruct((8, n), jnp.int32),
                   jax.ShapeDtypeStruct((8, n), F32),
                   jax.ShapeDtypeStruct((n_tiles * N_EXPERTS, HEAD_W), F32)],
        compiler_params=_params(1),
        name="out_even" if even else "out_odd",
    )(*args)


def _moe_kernel(cin_ref, cout_ref, be_ref, nu_ref, xs_hbm, ys_init_hbm, wg_ref, wu_ref, wd_ref, ys_hbm,
                xbuf, ybuf, wg_sc, wu_sc, wd_sc, in_sem, out_sem):
    del ys_init_hbm
    b = pl.program_id(0)
    n_used = nu_ref[0]
    cb, ch = MOE_BLOCK_CHUNKS, MOE_CHUNK

    def in_copy(blk, slot, c):
        src = pl.multiple_of(cin_ref[blk * cb + c] * ch, ch)
        return pltpu.make_async_copy(xs_hbm.at[pl.ds(src, ch), :], xbuf.at[slot, pl.ds(c * ch, ch), :],
                                     in_sem.at[slot])

    def out_copy(blk, slot, c):
        dst = pl.multiple_of(cout_ref[blk * cb + c] * ch, ch)
        return pltpu.make_async_copy(ybuf.at[slot, pl.ds(c * ch, ch), :], ys_hbm.at[pl.ds(dst, ch), :],
                                     out_sem.at[slot])

    def start_in(blk, slot):
        for c in range(cb):
            in_copy(blk, slot, c).start()

    def wait_out(blk, slot):
        for c in range(cb):
            out_copy(blk, slot, c).wait()

    slot = b % 2

    @pl.when(b == 0)
    def _():
        start_in(0, 0)

    @pl.when(b + 1 < n_used)
    def _():
        start_in(b + 1, 1 - slot)

    @pl.when(b < n_used)
    def _():
        for c in range(cb):
            in_copy(b, slot, c).wait()

        @pl.when(b >= 2)
        def _():
            wait_out(b - 2, slot)

        @pl.when(jnp.logical_or(b == 0, be_ref[b] != be_ref[jnp.maximum(b - 1, 0)]))
        def _():
            wg_sc[...] = wg_ref[0, 0].astype(BF16)
            wu_sc[...] = wu_ref[0, 0].astype(BF16)
            wd_sc[...] = wd_ref[0, 0].astype(BF16)

        x = xbuf[slot]
        a = (_silu(_dot(x, wg_sc[...])) * _dot(x, wu_sc[...])).astype(BF16)
        ybuf[slot] = _dot(a, wd_sc[...]).astype(BF16)
        for c in range(cb):
            out_copy(b, slot, c).start()

    @pl.when(b == pl.num_programs(0) - 1)
    def _():
        @pl.when(n_used >= 2)
        def _():
            wait_out(n_used - 2, n_used % 2)

        wait_out(n_used - 1, (n_used - 1) % 2)


def _moe_experts(xs_local, ys_init, chunk_in, chunk_out, block_e, n_used, w_gate, w_up, w_down, layer):
    d = xs_local.shape[1]
    n_blocks = chunk_in.shape[0] // MOE_BLOCK_CHUNKS
    wspec = pl.BlockSpec((1, 1, d, d), lambda i, ci, co, be, nu: (layer, be[i], 0, 0))
    hbm = pl.BlockSpec(memory_space=pl.ANY)
    return pl.pallas_call(
        _moe_kernel,
        grid_spec=pltpu.PrefetchScalarGridSpec(
            num_scalar_prefetch=4,
            grid=(n_blocks,),
            in_specs=[hbm, hbm, wspec, wspec, wspec],
            out_specs=hbm,
            scratch_shapes=[pltpu.VMEM((2, MOE_BLOCK, d), BF16), pltpu.VMEM((2, MOE_BLOCK, d), BF16),
                            pltpu.VMEM((d, d), BF16), pltpu.VMEM((d, d), BF16), pltpu.VMEM((d, d), BF16),
                            pltpu.SemaphoreType.DMA((2,)), pltpu.SemaphoreType.DMA((2,))]),
        out_shape=jax.ShapeDtypeStruct(ys_init.shape, BF16),
        input_output_aliases={5: 0},
        compiler_params=_params(1),
        name="moe_experts",
    )(chunk_in, chunk_out, block_e, n_used, xs_local, ys_init, w_gate, w_up, w_down)


def _combine_kernel(x_ref, ys_ref, sl_ref, gw_ref, mod_ref, o_ref):
    sl = sl_ref[...]
    w = gw_ref[...]
    ys = ys_ref[...]
    local = lax.broadcasted_iota(jnp.int32, (sl.shape[0], ys.shape[0]), 1)
    pick = (jnp.where(local == sl[:, 0:1], w[:, 0:1], 0.0)
            + jnp.where(local == sl[:, 1:2], w[:, 1:2], 0.0)).astype(BF16)
    o_ref[...] = x_ref[...] + mod_ref[0, 5:6, :] * _dot(pick, ys)


def _combine(x1, ys_local, slot_cols, gw_cols, mods, rows):
    n, d = x1.shape
    tm = ROW_TILE
    return pl.pallas_call(
        _combine_kernel,
        grid=(n // tm,),
        in_specs=[pl.BlockSpec((tm, d), lambda i: (i, 0)),
                  pl.BlockSpec((MOE_TILE_ROWS, d), lambda i: (i, 0)),
                  pl.BlockSpec((tm, 2), lambda i: (i, 0)),
                  pl.BlockSpec((tm, 2), lambda i: (i, 0)),
                  pl.BlockSpec((1, 6, d), lambda i: (rows.cond_of_tile(i, tm), 0, 0))],
        out_specs=pl.BlockSpec((tm, d), lambda i: (i, 0)),
        out_shape=jax.ShapeDtypeStruct((n, d), F32),
        compiler_params=_params(1),
        name="moe_combine",
    )(x1, ys_local, slot_cols, gw_cols, mods)


def _chunk_tables(counts):
    n_tiles = counts.shape[0]
    cb, tc = MOE_BLOCK_CHUNKS, MOE_TILE_CHUNKS
    n_list = (n_tiles * (tc - 1) + N_EXPERTS * (cb - 1) + cb - 1) // cb * cb
    nch = (counts + MOE_CHUNK - 1) // MOE_CHUNK
    first = jnp.cumsum(nch, axis=1) - nch
    total = jnp.sum(nch, axis=0)
    padded = (total + cb - 1) // cb * cb
    e_end = jnp.cumsum(padded)
    e_start = e_end - padded
    before = (jnp.cumsum(nch, axis=0) - nch).T
    seg_start = (e_start[:, None] + before).reshape(-1)
    seg_len = nch.T.reshape(-1)
    seg_first = (jnp.arange(n_tiles, dtype=jnp.int32)[None, :] * tc + first.T).reshape(-1)
    j = jnp.arange(n_list, dtype=jnp.int32)
    seg = jnp.clip(jnp.sum((seg_start[None, :] <= j[:, None]).astype(jnp.int32), axis=1) - 1,
                   0, seg_start.shape[0] - 1)
    e_j = seg // n_tiles
    assert n_list < (1 << 13) and tc < (1 << 6) and n_tiles * tc < (1 << 13)
    packed = (seg_start.astype(jnp.uint32) | (seg_len.astype(jnp.uint32) << 13)
              | (seg_first.astype(jnp.uint32) << 19))[seg]
    seg_off = j - (packed & 0x1FFF).astype(jnp.int32)
    valid = jnp.logical_and(seg_off < ((packed >> 13) & 0x3F).astype(jnp.int32), j < e_end[-1])
    src = (packed >> 19).astype(jnp.int32) + seg_off
    experts = jnp.arange(N_EXPERTS, dtype=jnp.int32)
    real_end = jnp.sum(jnp.where(e_j[:, None] == experts[None, :], (e_start + total)[None, :], 0), axis=1)
    scratch = n_tiles * tc + e_j * cb + jnp.clip(j - real_end, 0, cb - 1)
    chunk_in = jnp.where(valid, src, tc - 1).astype(jnp.int32)
    chunk_out = jnp.where(valid, src, scratch).astype(jnp.int32)
    block_e = e_j[::cb]
    n_used = (e_end[-1] // cb).astype(jnp.int32).reshape(1)
    return chunk_in, chunk_out, block_e, n_used


def _moe_layer(x1, xs_local, ys_zero, slots, gw, counts, mods, w_gate, w_up, w_down, layer, rows):
    n_tiles = x1.shape[0] // ROW_TILE
    counts = counts[:, 0].astype(jnp.int32).reshape(n_tiles, N_EXPERTS)
    chunk_in, chunk_out, block_e, n_used = _chunk_tables(counts)
    ys_local = _moe_experts(xs_local, ys_zero, chunk_in, chunk_out, block_e, n_used, w_gate, w_up, w_down, layer)
    return _combine(x1, ys_local, slots[0:2].T, gw[0:2].T, mods, rows)


def kernel(x_prompt, x_sample, c, state_mlstm_C, state_mlstm_n, state_mlstm_m, state_ret_S, cache_k, cache_v, c_ctx, w_ada, b_ada, norm_mix_gain, norm_ffn_gain, w_in_even, mlstm_conv, mlstm_gate_bias, mlstm_out_gain, ret_decay_logit, ret_out_gain, w_out_even, w_qkv_odd, q_norm_gain, k_norm_gain, lambda_qk, attn_out_gain, w_out_odd, w_router, router_bias, moe_w_gate, moe_w_up, moe_w_down):
    bp, seq, d = x_prompt.shape
    bs, dec_seq, _ = x_sample.shape
    depth = w_ada.shape[0]
    past = cache_k.shape[2]
    rows = _Rows(bp, seq, bs, dec_seq)
    nu = N_DIR * H_A
    assert 1 + bs <= N_COND_PAD

    x = jnp.concatenate([x_prompt.reshape(bp * seq, d), x_sample.reshape(bs * dec_seq, d)], axis=0)
    cond = jnp.concatenate([c_ctx[None, :], c, jnp.zeros((N_COND_PAD - 1 - bs, d), F32)], axis=0)
    mods_all = _modulation_all(cond, w_ada, b_ada).reshape(depth, N_COND_PAD, 6, d)

    w_router_pad = jnp.pad(w_router, ((0, 0), (0, HEAD_W - N_EXPERTS))).astype(BF16)
    router_bias_col = jnp.broadcast_to(router_bias[:, None], (N_EXPERTS, HEAD_W))
    e64 = jnp.asarray(np.kron(np.eye(d // DH_C, dtype=np.float32), np.ones((DH_C, DH_C), np.float32)), BF16)
    cos, sin = _rope_tables(dec_seq)

    st_c, st_n, st_m, st_s, st_k, st_v = [], [], [], [], [], []
    for l in range(depth):
        j = l // 2
        mods = mods_all[l]
        gain_mix = norm_mix_gain[l][None, :]
        if l % 2 == 0:
            w = w_in_even[j]
            wa = 4 * HEAD_W * 4
            gcols = 4 * H_A
            ob = wa + gcols
            w_main = jnp.concatenate([w[:, 0:1536], w[:, ob:ob + 1536], w[:, 1536:2048],
                                      w[:, ob + 1536:ob + 2048]], axis=1).astype(BF16)
            w_g = w[:, wa:wa + gcols]
            zpad = jnp.zeros((d, HEAD_W - nu), F32)
            w_g32 = jnp.concatenate([w_g[:, 0:nu], zpad, w_g[:, nu:2 * nu], zpad], axis=1)
            w_g_hi = w_g32.astype(BF16)
            w_gates = jnp.concatenate([w_g_hi, (w_g32 - w_g_hi.astype(F32)).astype(BF16)], axis=1)
            gb = mlstm_gate_bias[j].reshape(2, nu)
            zb = jnp.zeros((HEAD_W - nu,), F32)
            gate_bias = jnp.concatenate([gb[0], zb, gb[1], zb])[None, :]
            qkva, qkvb, oz, gates = _proj_even(x, mods, gain_mix, w_main, w_gates, mlstm_conv[j], gate_bias, rows)
            c0 = state_mlstm_C[:, j].reshape(bs, nu, HEAD_W, HEAD_W)
            n0 = state_mlstm_n[:, j].reshape(bs, nu, HEAD_W)
            m0 = jnp.broadcast_to(state_mlstm_m[:, j].reshape(bs, nu, 1), (bs, nu, HEAD_W))
            s0 = state_ret_S[:, j].reshape(bs, nu, HEAD_W, HEAD_W)
            ret_logit = jnp.broadcast_to(ret_decay_logit[j].reshape(nu, 1), (nu, HEAD_W))
            hf, hb, cn, nn, mn, sn = _scan_even(qkva, qkvb, gates, c0, n0, m0, s0, ret_logit, rows)
            st_c.append(cn.reshape(bp, N_DIR, H_A, HEAD_W, HEAD_W))
            st_n.append(nn.reshape(bp, N_DIR, H_A, HEAD_W))
            st_m.append(mn[:, :, 0].reshape(bp, N_DIR, H_A))
            st_s.append(sn.reshape(bp, N_DIR, H_B, HEAD_W, HEAD_W))
            mix_in = (hf, hb, oz, mlstm_out_gain[j].reshape(1, H_A * HEAD_W),
                      ret_out_gain[j].reshape(1, H_B * HEAD_W))
            w_out = w_out_even[j].astype(BF16)
        else:
            lam_init = 0.8 - 0.6 * math.exp(-0.3 * l)
            w_qkv = w_qkv_odd[j].astype(BF16)
            qg = jnp.tile(q_norm_gain[j], d // DH_C)[None, :]
            kg = jnp.tile(k_norm_gain[j], d // DH_C)[None, :]
            og = attn_out_gain[j][None, :]
            q_p, k_p, v_p = _proj_odd(x, mods, gain_mix, w_qkv, e64, qg, kg, cos, sin, rows, sample=False)
            q_s, k_s, v_s = _proj_odd(x, mods, gain_mix, w_qkv, e64, qg, kg, cos, sin, rows, sample=True)
            o_p = _attention(q_p.reshape(bp, seq, d), k_p.reshape(bp, seq, d), v_p.reshape(bp, seq, d),
                             None, None, lambda_qk[j], og, lam_init)
            o_s = _attention(q_s.reshape(bs, dec_seq, d), k_s.reshape(bs, dec_seq, d), v_s.reshape(bs, dec_seq, d),
                             cache_k[:, j].reshape(bs, past, d), cache_v[:, j].reshape(bs, past, d),
                             lambda_qk[j], og, lam_init)
            st_k.append(k_p.reshape(bp, seq, H_C, 2, DH_C))
            st_v.append(v_p.reshape(bp, seq, H_C, 2 * DH_C))
            mix_in = (o_p.reshape(bp * seq, d), o_s.reshape(bs * dec_seq, d))
            w_out = w_out_odd[j].astype(BF16)
        x1, xs_local, ys_zero, slots, gw, counts = _out_and_route(
            mix_in, x, mods, w_out, norm_ffn_gain[l][None, :], w_router_pad, router_bias_col, rows,
            even=(l % 2 == 0))
        x = _moe_layer(x1, xs_local, ys_zero, slots, gw, counts, mods, moe_w_gate, moe_w_up, moe_w_down, l, rows)

    dt = x_prompt.dtype
    y_prompt = x[:rows.n_prompt].reshape(bp, seq, d)
    y_sample = x[rows.n_prompt:].reshape(bs, dec_seq, d)
    return (y_prompt, y_sample,
            jnp.stack(st_c, axis=1).astype(dt), jnp.stack(st_n, axis=1).astype(dt),
            jnp.stack(st_m, axis=1).astype(dt), jnp.stack(st_s, axis=1).astype(dt),
            jnp.stack(st_k, axis=1).astype(dt), jnp.stack(st_v, axis=1).astype(dt))
```
